```python
import math
import jax, jax.numpy as jnp
from jax import lax
import numpy as np

D_MODEL = 1024
BATCH = 2
SEQ = 8192
DEPTH = 4
DEC_BATCH = 32
DEC_SEQ = 8
PAST_LEN = 8192
PAGE_SIZE = 128

D_SSD = D_MODEL
SSD_HEAD_DIM = 64
H_SSD = D_SSD // SSD_HEAD_DIM
SSD_STATE = 128
SSD_GROUPS = 2
SSD_CONV = 4
SSD_CHUNK = 128
CONV_DIM = D_SSD + 2 * SSD_GROUPS * SSD_STATE
D_SC = D_MODEL
SC_WIDTH = 3
HEAD_DIM = 64
H_ATT = D_MODEL // HEAD_DIM
N_KV = H_ATT // 4
GQA = H_ATT // N_KV
D_ATT = H_ATT * HEAD_DIM
KV_DIM = N_KV * HEAD_DIM
CMP_LEN = 32
CMP_STRIDE = 16
CMP_HID = 4 * HEAD_DIM
SEL_BLK = 64
N_TOP = 16
WINDOW = 512
Q_BLK = 64
D_FF = 4 * D_MODEL
ROPE_THETA = 10000.0
EPS = 1e-6
NEG_INF = -1e30
BIG = 1e9
IN_SIZES = (D_SSD, CONV_DIM, H_SSD, D_SC, D_SC, D_SC, D_ATT, 6 * KV_DIM, 3 * H_ATT, 3 * D_MODEL)
N_IN = sum(IN_SIZES)

kernel_name = 'hybrid_ssd_shortconv_nsa_step'


def rms_norm(x, g):
    xf = x.astype(jnp.float32)
    y = xf * lax.rsqrt(jnp.mean(xf * xf, axis=-1, keepdims=True) + EPS) * g.astype(jnp.float32)
    return y.astype(x.dtype)


def rope(x, pos):
    half = HEAD_DIM // 2
    inv = ROPE_THETA ** (-jnp.arange(half, dtype=jnp.float32) / half)
    ang = pos.astype(jnp.float32)[:, None] * inv[None, :]
    shp = (1, pos.shape[0]) + (1,) * (x.ndim - 3) + (half,)
    cos = jnp.cos(ang).reshape(shp)
    sin = jnp.sin(ang).reshape(shp)
    xf = x.astype(jnp.float32)
    x1, x2 = xf[..., :half], xf[..., half:]
    return jnp.concatenate([x1 * cos - x2 * sin, x1 * sin + x2 * cos], axis=-1).astype(x.dtype)


def causal_dwconv(u, w, state):
    k_w = w.shape[0]
    t = u.shape[1]
    up = jnp.concatenate([state.astype(u.dtype), u], axis=1)
    y = up[:, 0:t] * w[0]
    for k in range(1, k_w):
        y = y + up[:, k:k + t] * w[k]
    return y, up[:, t:]


def ssd_chunked(x, dt, a, bm, cm, h0):
    f32 = jnp.float32
    bsz, t = x.shape[:2]
    ln = min(SSD_CHUNK, t)
    nc = t // ln
    r = H_SSD // SSD_GROUPS
    xr = x.astype(f32).reshape(bsz, nc, ln, SSD_GROUPS, r, SSD_HEAD_DIM)
    dtr = dt.astype(f32).reshape(bsz, nc, ln, SSD_GROUPS, r)
    br = bm.astype(f32).reshape(bsz, nc, ln, SSD_GROUPS, SSD_STATE)
    cr = cm.astype(f32).reshape(bsz, nc, ln, SSD_GROUPS, SSD_STATE)
    acum = jnp.cumsum(dtr * a.astype(f32).reshape(SSD_GROUPS, r), axis=2)
    causal = jnp.tril(jnp.ones((ln, ln), dtype=bool))[:, :, None, None]
    seg = jnp.where(causal, acum[:, :, :, None] - acum[:, :, None, :], -jnp.inf)
    cb = jnp.einsum('bclgn,bcsgn->bclsg', cr, br)
    wts = jnp.exp(seg) * cb[..., None] * dtr[:, :, None]
    y = jnp.einsum('bclsgr,bcsgrp->bclgrp', wts, xr)
    xw = (jnp.exp(acum[:, :, -1:] - acum) * dtr)[..., None] * xr
    states = jnp.einsum('bclgn,bclgrp->bcgrpn', br, xw)
    cdec = jnp.exp(acum[:, :, -1])

    def step(h, inp):
        s_c, d_c = inp
        return d_c[..., None, None] * h + s_c, h

    h_init = h0.astype(f32).reshape(bsz, SSD_GROUPS, r, SSD_HEAD_DIM, SSD_STATE)
    h_last, h_in = lax.scan(step, h_init, (jnp.moveaxis(states, 1, 0), jnp.moveaxis(cdec, 1, 0)))
    h_in = jnp.moveaxis(h_in, 0, 1)
    y = y + jnp.einsum('bclgn,bcgrpn->bclgrp', cr, h_in) * jnp.exp(acum)[..., None]
    y = y.reshape(bsz, t, H_SSD, SSD_HEAD_DIM).astype(x.dtype)
    return y, h_last.reshape(bsz, H_SSD, SSD_HEAD_DIM, SSD_STATE).astype(x.dtype)


def compress(rows, pe, w1, w2):
    bsz, s = rows.shape[:2]
    nc = (s - CMP_LEN) // CMP_STRIDE + 1
    n_part = CMP_LEN // CMP_STRIDE
    nh = nc + n_part - 1
    seg = rows[:, :nh * CMP_STRIDE].reshape(bsz, nh, CMP_STRIDE, N_KV, HEAD_DIM)
    seg = seg.transpose(0, 1, 3, 2, 4).reshape(bsz, nh, N_KV, CMP_STRIDE * HEAD_DIM)
    w1p = w1.reshape(n_part, CMP_STRIDE * HEAD_DIM, CMP_HID)
    hid = pe.reshape(-1) @ w1
    for m in range(n_part):
        hid = hid + seg[:, m:m + nc] @ w1p[m]
    return jax.nn.silu(hid) @ w2


def overlap_matrix(nc, n_sel):
    c0 = jnp.arange(nc)[:, None] * CMP_STRIDE
    s0 = jnp.arange(n_sel)[None, :] * SEL_BLK
    ov = jnp.minimum(c0 + CMP_LEN, s0 + SEL_BLK) - jnp.maximum(c0, s0)
    return jnp.maximum(ov, 0).astype(jnp.float32) / CMP_LEN


def masked_softmax(s, mask):
    s = jnp.where(mask, s.astype(jnp.float32), NEG_INF)
    p = jax.nn.softmax(s, axis=-1)
    return jnp.where(mask, p, 0.0)


def nsa_attention(q, gates, kc, vc, ks, vs, kw, vw, pos0, wpos0):
    bsz, t = q.shape[:2]
    qb = min(Q_BLK, t)
    nqb = t // qb
    s_len = ks.shape[1]
    n_sel = -(-s_len // SEL_BLK)
    nc = kc.shape[1]
    k_top = min(N_TOP, n_sel)
    pad = n_sel * SEL_BLK - s_len

    def to_blocks(a):
        a = jnp.pad(a, ((0, 0), (0, pad), (0, 0), (0, 0)))
        return a.reshape(bsz, n_sel, SEL_BLK, N_KV, HEAD_DIM).transpose(0, 3, 1, 2, 4)

    ksb, vsb = to_blocks(ks), to_blocks(vs)
    wpad = ((0, 0), (WINDOW, 0), (0, 0), (0, 0))
    kwp, vwp = jnp.pad(kw, wpad), jnp.pad(vw, wpad)
    ov = overlap_matrix(nc, n_sel)
    cmp_end = jnp.arange(nc) * CMP_STRIDE + CMP_LEN - 1
    sel_ids = jnp.arange(n_sel)
    b_ix = jnp.arange(bsz)[:, None, None, None]
    h_ix = jnp.arange(N_KV)[None, :, None, None]
    scale = HEAD_DIM ** -0.5
    qs = q.reshape(bsz, nqb, qb, N_KV, GQA, HEAD_DIM).swapaxes(0, 1)
    gs = gates.reshape(bsz, nqb, qb, N_KV, GQA, 3).swapaxes(0, 1)

    def one(args):
        qi, gi, bi = args
        q0 = pos0 + bi * qb
        qpos = q0 + jnp.arange(qb)
        s_c = jnp.einsum('bqhgd,bchd->bhgqc', qi, kc) * scale
        p_c = masked_softmax(s_c, cmp_end[None, :] <= qpos[:, None])
        o_c = jnp.einsum('bhgqc,bchd->bqhgd', p_c, vc)
        imp = jnp.einsum('bhgqc,cj->bhqj', p_c, ov)
        cur = (qpos // SEL_BLK)[:, None]
        forced = (sel_ids == 0) | (sel_ids == cur) | (sel_ids == cur - 1)
        imp = jnp.where(forced, BIG, jnp.where(sel_ids <= cur, imp, NEG_INF))
        _, idx = lax.top_k(imp, k_top)
        kg = ksb[b_ix, h_ix, idx]
        vg = vsb[b_ix, h_ix, idx]
        kpos = idx[..., None] * SEL_BLK + jnp.arange(SEL_BLK)
        m_s = (kpos <= qpos[None, None, :, None, None]).reshape(bsz, N_KV, 1, qb, k_top * SEL_BLK)
        s_s = jnp.einsum('bqhgd,bhqksd->bhgqks', qi, kg) * scale
        p_s = masked_softmax(s_s.reshape(bsz, N_KV, GQA, qb, k_top * SEL_BLK), m_s)
        o_s = jnp.einsum('bhgqm,bhqmd->bqhgd', p_s, vg.reshape(bsz, N_KV, qb, k_top * SEL_BLK, HEAD_DIM))
        start = q0 - wpos0
        kwi = lax.dynamic_slice_in_dim(kwp, start, WINDOW + qb, axis=1)
        vwi = lax.dynamic_slice_in_dim(vwp, start, WINDOW + qb, axis=1)
        kpos_w = q0 - WINDOW + jnp.arange(WINDOW + qb)
        dlt = qpos[:, None] - kpos_w[None, :]
        m_w = (dlt >= 0) & (dlt < WINDOW) & (kpos_w[None, :] >= 0)
        s_w = jnp.einsum('bqhgd,bkhd->bhgqk', qi, kwi) * scale
        p_w = masked_softmax(s_w, m_w)
        o_w = jnp.einsum('bhgqk,bkhd->bqhgd', p_w, vwi)
        out = gi[..., 0:1] * o_c + gi[..., 1:2] * o_s + gi[..., 2:3] * o_w
        return out.astype(q.dtype)

    out = lax.map(one, (qs, gs, jnp.arange(nqb, dtype=jnp.int32)))
    return out.swapaxes(0, 1).reshape(bsz, t, N_KV, GQA, HEAD_DIM)


def gather_pages(pool, page_table):
    g = pool[page_table]
    return g.reshape(page_table.shape[0], page_table.shape[1] * pool.shape[1], 2, N_KV, HEAD_DIM)


def trunk_layer(x, pos0, past, norm_mix, norm_mlp, w_in, ssd_conv_w, ssd_conv_b, ssd_dt_bias, ssd_a_log,
                ssd_d, ssd_norm, w_ssd_out, sconv_w, w_sconv_out, cmp_pe, cmp_w1, cmp_w2, w_nsa_out,
                w_out, w_mlp_up, w_mlp_down):
    bsz, t, _ = x.shape
    if past is None:
        ssm0 = jnp.zeros((bsz, H_SSD, SSD_HEAD_DIM, SSD_STATE), x.dtype)
        conv0 = jnp.zeros((bsz, SSD_CONV - 1, CONV_DIM), x.dtype)
        sc0 = jnp.zeros((bsz, SC_WIDTH - 1, D_SC), x.dtype)
        cmp_past = sel_past = win_past = None
    else:
        ssm0, conv0, sc0, cmp_past, sel_past, win_past = past
    h = rms_norm(x, norm_mix)
    u = h @ w_in
    bounds = np.cumsum(IN_SIZES)[:-1].tolist()
    z, xbc, dt_raw, sc_b, sc_c, sc_h, q, kv, att_g, mrg_g = jnp.split(u, bounds, axis=-1)

    xbc, conv_new = causal_dwconv(xbc, ssd_conv_w, conv0)
    xbc = jax.nn.silu(xbc + ssd_conv_b)
    xs, bm, cm = jnp.split(xbc, [D_SSD, D_SSD + SSD_GROUPS * SSD_STATE], axis=-1)
    xs = xs.reshape(bsz, t, H_SSD, SSD_HEAD_DIM)
    bm = bm.reshape(bsz, t, SSD_GROUPS, SSD_STATE)
    cm = cm.reshape(bsz, t, SSD_GROUPS, SSD_STATE)
    dt = jax.nn.softplus((dt_raw + ssd_dt_bias).astype(jnp.float32))
    a = -jnp.exp(ssd_a_log.astype(jnp.float32))
    y, ssm_new = ssd_chunked(xs, dt, a, bm, cm, ssm0)
    y = y + ssd_d[:, None] * xs
    y = rms_norm(y.reshape(bsz, t, D_SSD) * jax.nn.silu(z), ssd_norm)
    y_ssd = y @ w_ssd_out

    cv, sc_new = causal_dwconv(sc_c * sc_h, sconv_w, sc0)
    y_sc = (sc_b * cv) @ w_sconv_out

    pos = pos0 + jnp.arange(t)
    q = rope(q.reshape(bsz, t, H_ATT, HEAD_DIM), pos).reshape(bsz, t, N_KV, GQA, HEAD_DIM)
    kv = kv.reshape(bsz, t, 6, N_KV, HEAD_DIM)
    keys = rope(kv[:, :, 0::2], pos)
    vals = kv[:, :, 1::2]
    new_cmp = jnp.stack([keys[:, :, 0], vals[:, :, 0]], axis=2)
    new_sel = jnp.stack([keys[:, :, 1], vals[:, :, 1]], axis=2)
    new_win = jnp.stack([keys[:, :, 2], vals[:, :, 2]], axis=2)
    cmp_rows = new_cmp if cmp_past is None else jnp.concatenate([cmp_past.astype(x.dtype), new_cmp], axis=1)
    sel_rows = new_sel if sel_past is None else jnp.concatenate([sel_past.astype(x.dtype), new_sel], axis=1)
    win_rows = new_win if win_past is None else jnp.concatenate([win_past.astype(x.dtype), new_win], axis=1)
    wpos0 = pos0 + t - win_rows.shape[1]
    win_new = win_rows[:, win_rows.shape[1] - min(WINDOW, win_rows.shape[1]):]
    kc = compress(cmp_rows[:, :, 0], cmp_pe[0], cmp_w1[0], cmp_w2[0])
    vc = compress(cmp_rows[:, :, 1], cmp_pe[1], cmp_w1[1], cmp_w2[1])
    gates = jax.nn.sigmoid(att_g).reshape(bsz, t, N_KV, GQA, 3)
    o = nsa_attention(q, gates, kc, vc, sel_rows[:, :, 0], sel_rows[:, :, 1],
                      win_rows[:, :, 0], win_rows[:, :, 1], pos0, wpos0)
    y_nsa = o.reshape(bsz, t, D_ATT) @ w_nsa_out

    g_ssd, g_sc, g_nsa = jnp.split(jax.nn.sigmoid(mrg_g), 3, axis=-1)
    x = x + (g_ssd * y_ssd + g_sc * y_sc + g_nsa * y_nsa) @ w_out
    hm = rms_norm(x, norm_mlp)
    x = x + jnp.square(jax.nn.relu(hm @ w_mlp_up)) @ w_mlp_down
    return x, (new_cmp, new_sel, win_new, ssm_new, conv_new, sc_new)


def setup_inputs(seed: int = 0) -> dict:
    key = jax.random.key(seed)
    ks = jax.random.split(key, 32)
    f32 = jnp.float32
    n_pages = PAST_LEN // PAGE_SIZE
    n_pool = (DEC_BATCH * n_pages * 5) // 4
    w_buf = min(WINDOW, PAST_LEN)

    def nrm(k, shape, scale):
        return jax.random.normal(k, shape, f32) * scale

    x_prompt = nrm(ks[0], (BATCH, SEQ, D_MODEL), 1.0)
    x_sample = nrm(ks[1], (DEC_BATCH, DEC_SEQ, D_MODEL), 1.0)
    cache_cmp_kv = nrm(ks[2], (DEPTH, n_pool, PAGE_SIZE, 2, N_KV, HEAD_DIM), 1.0)
    cache_sel_kv = nrm(ks[3], (DEPTH, n_pool, PAGE_SIZE, 2, N_KV, HEAD_DIM), 1.0)
    cache_win_kv = nrm(ks[4], (DEPTH, DEC_BATCH, w_buf, 2, N_KV, HEAD_DIM), 1.0)
    state_ssm = nrm(ks[5], (DEPTH, DEC_BATCH, H_SSD, SSD_HEAD_DIM, SSD_STATE), 0.1)
    state_ssd_conv = nrm(ks[6], (DEPTH, DEC_BATCH, SSD_CONV - 1, CONV_DIM), 1.0)
    state_sconv = nrm(ks[7], (DEPTH, DEC_BATCH, SC_WIDTH - 1, D_SC), 1.0)
    page_table = jax.random.permutation(ks[8], n_pool)[:DEC_BATCH * n_pages].reshape(DEC_BATCH, n_pages).astype(jnp.int32)
    norm_mix = 1.0 + nrm(ks[9], (DEPTH, D_MODEL), 0.02)
    norm_mlp = 1.0 + nrm(ks[10], (DEPTH, D_MODEL), 0.02)
    norm_final = 1.0 + nrm(ks[11], (D_MODEL,), 0.02)
    w_in = nrm(ks[12], (DEPTH, D_MODEL, N_IN), D_MODEL ** -0.5)
    ssd_conv_w = nrm(ks[13], (DEPTH, SSD_CONV, CONV_DIM), SSD_CONV ** -0.5)
    ssd_conv_b = nrm(ks[14], (DEPTH, CONV_DIM), 0.02)
    dt0 = jnp.exp(jax.random.uniform(ks[15], (DEPTH, H_SSD), f32, math.log(1e-3), math.log(1e-1)))
    ssd_dt_bias = dt0 + jnp.log(-jnp.expm1(-dt0))
    ssd_a_log = jnp.log(jax.random.uniform(ks[16], (DEPTH, H_SSD), f32, 1.0, 16.0))
    ssd_d = 1.0 + nrm(ks[17], (DEPTH, H_SSD), 0.02)
    ssd_norm = 1.0 + nrm(ks[18], (DEPTH, D_SSD), 0.02)
    w_ssd_out = nrm(ks[19], (DEPTH, D_SSD, D_MODEL), D_SSD ** -0.5)
    sconv_w = nrm(ks[20], (DEPTH, SC_WIDTH, D_SC), SC_WIDTH ** -0.5)
    w_sconv_out = nrm(ks[21], (DEPTH, D_SC, D_MODEL), D_SC ** -0.5)
    cmp_pe = nrm(ks[22], (DEPTH, 2, CMP_LEN, HEAD_DIM), 0.1)
    cmp_w1 = nrm(ks[23], (DEPTH, 2, CMP_LEN * HEAD_DIM, CMP_HID), (CMP_LEN * HEAD_DIM) ** -0.5)
    cmp_w2 = nrm(ks[24], (DEPTH, 2, CMP_HID, HEAD_DIM), CMP_HID ** -0.5)
    w_nsa_out = nrm(ks[25], (DEPTH, D_ATT, D_MODEL), D_ATT ** -0.5)
    w_out = nrm(ks[26], (DEPTH, D_MODEL, D_MODEL), D_MODEL ** -0.5)
    w_mlp_up = nrm(ks[27], (DEPTH, D_MODEL, D_FF), D_MODEL ** -0.5)
    w_mlp_down = nrm(ks[28], (DEPTH, D_FF, D_MODEL), D_FF ** -0.5)
    return {'x_prompt': x_prompt, 'x_sample': x_sample, 'cache_cmp_kv': cache_cmp_kv,
            'cache_sel_kv': cache_sel_kv, 'cache_win_kv': cache_win_kv, 'state_ssm': state_ssm,
            'state_ssd_conv': state_ssd_conv, 'state_sconv': state_sconv, 'page_table': page_table,
            'norm_mix': norm_mix, 'norm_mlp': norm_mlp, 'norm_final': norm_final, 'w_in': w_in,
            'ssd_conv_w': ssd_conv_w, 'ssd_conv_b': ssd_conv_b, 'ssd_dt_bias': ssd_dt_bias,
            'ssd_a_log': ssd_a_log, 'ssd_d': ssd_d, 'ssd_norm': ssd_norm, 'w_ssd_out': w_ssd_out,
            'sconv_w': sconv_w, 'w_sconv_out': w_sconv_out, 'cmp_pe': cmp_pe, 'cmp_w1': cmp_w1,
            'cmp_w2': cmp_w2, 'w_nsa_out': w_nsa_out, 'w_out': w_out, 'w_mlp_up': w_mlp_up,
            'w_mlp_down': w_mlp_down}


def reference(x_prompt, x_sample, cache_cmp_kv, cache_sel_kv, cache_win_kv, state_ssm, state_ssd_conv,
              state_sconv, page_table, norm_mix, norm_mlp, norm_final, w_in, ssd_conv_w, ssd_conv_b,
              ssd_dt_bias, ssd_a_log, ssd_d, ssd_norm, w_ssd_out, sconv_w, w_sconv_out, cmp_pe, cmp_w1,
              cmp_w2, w_nsa_out, w_out, w_mlp_up, w_mlp_down):
    past_len = page_table.shape[1] * cache_cmp_kv.shape[2]
    xp, xs = x_prompt, x_sample
    p_new = [[] for _ in range(6)]
    s_new = [[] for _ in range(6)]
    for l in range(DEPTH):
        wl = (norm_mix[l], norm_mlp[l], w_in[l], ssd_conv_w[l], ssd_conv_b[l], ssd_dt_bias[l], ssd_a_log[l],
              ssd_d[l], ssd_norm[l], w_ssd_out[l], sconv_w[l], w_sconv_out[l], cmp_pe[l], cmp_w1[l],
              cmp_w2[l], w_nsa_out[l], w_out[l], w_mlp_up[l], w_mlp_down[l])
        xp, st_p = trunk_layer(xp, 0, None, *wl)
        past = (state_ssm[l], state_ssd_conv[l], state_sconv[l],
                gather_pages(cache_cmp_kv[l], page_table), gather_pages(cache_sel_kv[l], page_table),
                cache_win_kv[l])
        xs, st_s = trunk_layer(xs, past_len, past, *wl)
        for i in range(6):
            p_new[i].append(st_p[i])
            s_new[i].append(st_s[i])
    y_prompt = rms_norm(xp, norm_final)
    y_sample = rms_norm(xs, norm_final)
    p_cmp, p_sel, p_win, p_ssm, p_conv, p_sc = [jnp.stack(a) for a in p_new]
    s_cmp, s_sel, s_win, s_ssm, s_conv, s_sc = [jnp.stack(a) for a in s_new]
    return (y_prompt, y_sample, p_cmp, p_sel, p_win, p_ssm, p_conv, p_sc, s_cmp, s_sel, s_win, s_ssm, s_conv, s_sc)
```

```python
import functools
import math

import numpy as np
import jax
import jax.numpy as jnp
from jax import lax
from jax.experimental import pallas as pl
from jax.experimental.pallas import tpu as pltpu

F32 = jnp.float32
BF16 = jnp.bfloat16

D_MODEL = 1024
SSD_HEAD_DIM = 64
H_SSD = D_MODEL // SSD_HEAD_DIM
SSD_STATE = 128
SSD_GROUPS = 2
SSD_CONV = 4
SSD_CHUNK = 128
CONV_DIM = D_MODEL + 2 * SSD_GROUPS * SSD_STATE
SC_WIDTH = 3
HEAD_DIM = 64
H_ATT = D_MODEL // HEAD_DIM
N_KV = 4
GQA = H_ATT // N_KV
KV_DIM = N_KV * HEAD_DIM
CMP_LEN = 32
CMP_STRIDE = 16
CMP_HID = 4 * HEAD_DIM
SEL_BLK = 64
N_TOP = 16
WINDOW = 512
D_FF = 4 * D_MODEL
ROPE_THETA = 10000.0
EPS = 1e-6
NEG_INF = -1e30
BIG = 1e9
TAKEN = -3e38
IN_SIZES = (D_MODEL, CONV_DIM, H_SSD, D_MODEL, D_MODEL, D_MODEL, D_MODEL, 6 * KV_DIM, 3 * H_ATT, 3 * D_MODEL)

LANES = 128
SUBLANES = 8
VMEM_LIMIT = 56 * 1024 * 1024

U_XBC, U_KV, U_Z, U_SCB, U_SCC, U_SCH, U_Q, U_G = 0, 1536, 3072, 4096, 5120, 6144, 7168, 8192
N_MAIN = 11264
GATE_COL0 = H_SSD

_NT = (((1,), (1,)), ((), ()))
_TN = (((0,), (0,)), ((), ()))


def _cparams(sem):
    return pltpu.CompilerParams(dimension_semantics=sem, vmem_limit_bytes=VMEM_LIMIT)


def _silu(x):
    return x * (1.0 / (1.0 + jnp.exp(-x)))


def _sigmoid(x):
    return 1.0 / (1.0 + jnp.exp(-x))


def _norm_matmul_kernel(x_ref, g_ref, w_ref, o_ref, h_ref):
    @pl.when(pl.program_id(1) == 0)
    def _():
        x = x_ref[...]
        ms = jnp.mean(x * x, axis=-1, keepdims=True)
        h_ref[...] = (x * lax.rsqrt(ms + EPS) * g_ref[...]).astype(h_ref.dtype)

    o_ref[...] = jnp.dot(h_ref[...], w_ref[...], preferred_element_type=F32).astype(o_ref.dtype)


def _norm_matmul(x, g, w, tm, tn):
    m, d = x.shape
    n = w.shape[1]
    return pl.pallas_call(
        _norm_matmul_kernel,
        grid=(m // tm, n // tn),
        in_specs=[pl.BlockSpec((tm, d), lambda i, j: (i, 0)),
                  pl.BlockSpec((1, d), lambda i, j: (0, 0)),
                  pl.BlockSpec((d, tn), lambda i, j: (0, j))],
        out_specs=pl.BlockSpec((tm, tn), lambda i, j: (i, j)),
        out_shape=jax.ShapeDtypeStruct((m, n), F32),
        scratch_shapes=[pltpu.VMEM((tm, d), BF16)],
        compiler_params=_cparams(("parallel", "arbitrary")),
        name="norm_matmul",
    )(x, g, w)


def _final_norm_kernel(x_ref, g_ref, o_ref):
    x = x_ref[...]
    ms = jnp.mean(x * x, axis=-1, keepdims=True)
    o_ref[...] = x * lax.rsqrt(ms + EPS) * g_ref[...]


def _final_norm(x, g, tm):
    m, d = x.shape
    return pl.pallas_call(
        _final_norm_kernel,
        grid=(m // tm,),
        in_specs=[pl.BlockSpec((tm, d), lambda i: (i, 0)), pl.BlockSpec((1, d), lambda i: (0, 0))],
        out_specs=pl.BlockSpec((tm, d), lambda i: (i, 0)),
        out_shape=jax.ShapeDtypeStruct((m, d), F32),
        compiler_params=_cparams(("parallel",)),
        name="final_norm",
    )(x, g)


def _rot_half(x, first_half):
    return jnp.where(first_half, pltpu.roll(x, LANES - HEAD_DIM // 2, axis=1), pltpu.roll(x, HEAD_DIM // 2, axis=1))


def _rope_kernel(q_ref, kv_ref, cos_ref, sin_ref, qo_ref, cmp_ref, sel_ref, win_ref, *, scale):
    cos = cos_ref[...]
    sin = sin_ref[...]
    lane = lax.broadcasted_iota(jnp.int32, cos.shape, 1)
    first_half = (lane % HEAD_DIM) < (HEAD_DIM // 2)

    def rope(x):
        return x * cos + _rot_half(x, first_half) * sin

    for c in range(D_MODEL // LANES):
        sl = slice(c * LANES, (c + 1) * LANES)
        qo_ref[:, sl] = rope(q_ref[:, sl]) * scale
    for br, o_ref in enumerate((cmp_ref, sel_ref, win_ref)):
        base = br * 2 * KV_DIM
        for c in range(KV_DIM // LANES):
            o_ref[:, c * LANES:(c + 1) * LANES] = rope(kv_ref[:, base + c * LANES: base + (c + 1) * LANES])
        o_ref[:, KV_DIM:] = kv_ref[:, base + KV_DIM: base + 2 * KV_DIM]


def _rope(u, cos, sin, tm):
    m = u.shape[0]
    kv_out = jax.ShapeDtypeStruct((m, 2 * KV_DIM), F32)
    return pl.pallas_call(
        functools.partial(_rope_kernel, scale=HEAD_DIM ** -0.5),
        grid=(m // tm,),
        in_specs=[pl.BlockSpec((tm, D_MODEL), lambda i: (i, U_Q // D_MODEL)),
                  pl.BlockSpec((tm, 6 * KV_DIM), lambda i: (i, U_KV // (6 * KV_DIM))),
                  pl.BlockSpec((tm, LANES), lambda i: (i, 0)),
                  pl.BlockSpec((tm, LANES), lambda i: (i, 0))],
        out_specs=[pl.BlockSpec((tm, D_MODEL), lambda i: (i, 0)),
                   pl.BlockSpec((tm, 2 * KV_DIM), lambda i: (i, 0)),
                   pl.BlockSpec((tm, 2 * KV_DIM), lambda i: (i, 0)),
                   pl.BlockSpec((tm, 2 * KV_DIM), lambda i: (i, 0))],
        out_shape=[jax.ShapeDtypeStruct((m, D_MODEL), F32), kv_out, kv_out, kv_out],
        compiler_params=_cparams(("parallel",)),
        name="rope",
    )(u, u, cos, sin)


def _seqmix_kernel(xbc_ref, z_ref, scb_ref, scc_ref, sch_ref, dtr_ref, conv0_ref, sc0_ref, h0_ref,
                   convw_ref, convb_ref, dtb_ref, alog_ref, dvec_ref, ssdn_ref, scw_ref,
                   yn_ref, ysc_ref, hout_ref, chlast_ref,
                   ext_ref, chext_ref, h_ref, y_ref, *, chunk, valid):
    c = pl.program_id(1)
    L = chunk
    P = SSD_HEAD_DIM
    N = SSD_STATE

    @pl.when(c == 0)
    def _():
        ext_ref[0:SUBLANES, :] = conv0_ref[0]
        chext_ref[0:SUBLANES, :] = sc0_ref[0]
        h_ref[...] = h0_ref[0]

    ext_ref[SUBLANES:SUBLANES + L, :] = xbc_ref[...]
    conv = jnp.broadcast_to(convb_ref[...], (L, CONV_DIM))
    for k in range(SSD_CONV):
        off = SUBLANES - (SSD_CONV - 1) + k
        conv = conv + convw_ref[k:k + 1, :] * ext_ref[off:off + L, :]
    ext_ref[0:SUBLANES, :] = ext_ref[L:L + SUBLANES, :]
    xbc = _silu(conv)

    ch = scc_ref[...] * sch_ref[...]
    chext_ref[SUBLANES:SUBLANES + L, :] = ch
    cv = jnp.zeros((L, D_MODEL), F32)
    for k in range(SC_WIDTH):
        off = SUBLANES - (SC_WIDTH - 1) + k
        cv = cv + scw_ref[k:k + 1, :] * chext_ref[off:off + L, :]
    ysc_ref[...] = (scb_ref[...] * cv).astype(ysc_ref.dtype)
    chlast_ref[0] = chext_ref[valid:valid + SUBLANES, :]
    chext_ref[0:SUBLANES, :] = chext_ref[L:L + SUBLANES, :]

    x = dtr_ref[...] + dtb_ref[...]
    dt = jnp.maximum(x, 0.0) + jnp.log1p(jnp.exp(-jnp.abs(x)))
    row = lax.broadcasted_iota(jnp.int32, (L, LANES), 0)
    if valid < L:
        dt = jnp.where(row < valid, dt, 0.0)
    a = -jnp.exp(alog_ref[...])
    da = dt * a
    ri = lax.broadcasted_iota(jnp.int32, (L, L), 0)
    ci = lax.broadcasted_iota(jnp.int32, (L, L), 1)
    tril = ri >= ci
    acum = jnp.dot(tril.astype(F32), da, precision=lax.Precision.HIGHEST, preferred_element_type=F32)
    acum_t = acum.T
    a_last = acum[L - 1:L, :]

    for g in range(SSD_GROUPS):
        bg = xbc[:, D_MODEL + g * N: D_MODEL + (g + 1) * N]
        cg = xbc[:, D_MODEL + SSD_GROUPS * N + g * N: D_MODEL + SSD_GROUPS * N + (g + 1) * N]
        cb = lax.dot_general(cg, bg, _NT, preferred_element_type=F32)
        for r in range(H_SSD // SSD_GROUPS):
            h = g * (H_SSD // SSD_GROUPS) + r
            a_col = acum[:, h:h + 1]
            a_row = acum_t[h:h + 1, :]
            seg = jnp.where(tril, a_col - a_row, NEG_INF)
            wts = jnp.exp(seg) * cb
            xs_h = xbc[:, h * P:(h + 1) * P]
            xdt = xs_h * dt[:, h:h + 1]
            hprev = h_ref[h * P:(h + 1) * P, :]
            y_h = jnp.dot(wts, xdt, preferred_element_type=F32)
            y_in = lax.dot_general(cg, hprev, _NT, preferred_element_type=F32)
            y_h = y_h + y_in * jnp.exp(a_col)
            y_h = y_h + dvec_ref[:, h:h + 1] * xs_h
            y_ref[:, h * P:(h + 1) * P] = y_h
            al = a_last[:, h:h + 1]
            xw = xdt * jnp.exp(al - a_col)
            s_h = lax.dot_general(xw, bg, _TN, preferred_element_type=F32)
            h_ref[h * P:(h + 1) * P, :] = jnp.exp(al) * hprev + s_h

    y = y_ref[...] * _silu(z_ref[...])
    ms = jnp.mean(y * y, axis=-1, keepdims=True)
    yn_ref[...] = (y * lax.rsqrt(ms + EPS) * ssdn_ref[...]).astype(yn_ref.dtype)
    hout_ref[0] = h_ref[...]


def _seqmix(u, us, conv0, sc0, h0, prm, bsz, t, valid):
    L = SSD_CHUNK
    nc = t // L
    m = bsz * t
    row = lambda b, c: b * nc + c
    spec_u = lambda w, col: pl.BlockSpec((L, w), lambda b, c: (row(b, c), col // w))
    full = lambda shape: pl.BlockSpec(shape, lambda b, c: (0,) * len(shape))
    perb = lambda shape: pl.BlockSpec((1,) + shape, lambda b, c: (b,) + (0,) * len(shape))
    return pl.pallas_call(
        functools.partial(_seqmix_kernel, chunk=L, valid=valid),
        grid=(bsz, nc),
        in_specs=[spec_u(CONV_DIM, U_XBC), spec_u(D_MODEL, U_Z), spec_u(D_MODEL, U_SCB), spec_u(D_MODEL, U_SCC),
                  spec_u(D_MODEL, U_SCH),
                  pl.BlockSpec((L, LANES), lambda b, c: (row(b, c), 0)),
                  perb((SUBLANES, CONV_DIM)), perb((SUBLANES, D_MODEL)), perb((D_MODEL, SSD_STATE)),
                  full((SUBLANES, CONV_DIM)), full((1, CONV_DIM)), full((1, LANES)), full((1, LANES)),
                  full((1, LANES)), full((1, D_MODEL)), full((SUBLANES, D_MODEL))],
        out_specs=[pl.BlockSpec((L, D_MODEL), lambda b, c: (row(b, c), 0)),
                   pl.BlockSpec((L, D_MODEL), lambda b, c: (row(b, c), 0)),
                   perb((D_MODEL, SSD_STATE)), perb((SUBLANES, D_MODEL))],
        out_shape=[jax.ShapeDtypeStruct((m, D_MODEL), BF16), jax.ShapeDtypeStruct((m, D_MODEL), BF16),
                   jax.ShapeDtypeStruct((bsz, D_MODEL, SSD_STATE), F32),
                   jax.ShapeDtypeStruct((bsz, SUBLANES, D_MODEL), F32)],
        scratch_shapes=[pltpu.VMEM((L + SUBLANES, CONV_DIM), F32), pltpu.VMEM((L + SUBLANES, D_MODEL), F32),
                        pltpu.VMEM((D_MODEL, SSD_STATE), F32), pltpu.VMEM((L, D_MODEL), F32)],
        compiler_params=_cparams(("parallel", "arbitrary")),
        name="seqmix",
    )(u, u, u, u, u, us, conv0, sc0, h0, prm["convw"], prm["convb"], prm["dtb"], prm["alog"], prm["dvec"],
      prm["ssdn"], prm["scw"])


CMP_TILE = 128
CMP_PAGE = 128


N_COMBO = 2 * N_KV


def _compress_fill(z_ref, rows, g0):
    R = rows.shape[0]
    lo = pl.ds(g0 + CMP_STRIDE, R)
    hi = pl.ds(g0, R)
    for pr in range(N_COMBO // 2):
        x2 = rows[:, pr * LANES:(pr + 1) * LANES]
        sw = pltpu.roll(x2, HEAD_DIM, axis=1)
        z_ref[2 * pr, lo, 0:HEAD_DIM] = x2[:, 0:HEAD_DIM]
        z_ref[2 * pr, hi, HEAD_DIM:LANES] = sw[:, HEAD_DIM:LANES]
        z_ref[2 * pr + 1, lo, 0:HEAD_DIM] = sw[:, 0:HEAD_DIM]
        z_ref[2 * pr + 1, hi, HEAD_DIM:LANES] = x2[:, HEAD_DIM:LANES]


def _compress_tile(z_ref, row0, pe_ref, w1_ref, w2_ref, out_refs, out_row0):
    for c in range(N_COMBO):
        kv, h = divmod(c, N_KV)
        acc = jnp.zeros((CMP_TILE, CMP_HID), F32)
        for r in range(CMP_STRIDE):
            zr = z_ref[c, pl.ds(row0 + CMP_STRIDE + r, CMP_TILE, stride=CMP_STRIDE), :] + pe_ref[kv, r:r + 1, :]
            acc = acc + jnp.dot(zr, w1_ref[kv, r], preferred_element_type=F32)
        o = jnp.dot(_silu(acc), w2_ref[kv], preferred_element_type=F32)
        out_refs[kv][0, pl.ds(out_row0, CMP_TILE), pl.ds(h * HEAD_DIM, HEAD_DIM)] = o


def _compress_kernel(pt_ref, page_ref, pe_ref, w1_ref, w2_ref, kc_ref, vc_ref, z_ref, *, n_pages, page):
    p = pl.program_id(1)
    _compress_fill(z_ref, page_ref[0], pl.multiple_of(p * page, page))

    @pl.when(p == n_pages - 1)
    def _():
        rows = n_pages * page
        z_ref[:, rows:rows + CMP_STRIDE, HEAD_DIM:LANES] = jnp.zeros((N_COMBO, CMP_STRIDE, LANES - HEAD_DIM), F32)

        def body(j, carry):
            _compress_tile(z_ref, pl.multiple_of(j * CMP_TILE * CMP_STRIDE, CMP_TILE * CMP_STRIDE),
                           pe_ref, w1_ref, w2_ref, (kc_ref, vc_ref), pl.multiple_of(j * CMP_TILE, CMP_TILE))
            return carry

        lax.fori_loop(0, rows // (CMP_TILE * CMP_STRIDE), body, 0)


def _compress(pool, page_table, pe2, w1p, w2):
    bsz, n_pages = page_table.shape
    page = pool.shape[1]
    rows = n_pages * page
    out = jax.ShapeDtypeStruct((bsz, rows // CMP_STRIDE, KV_DIM), F32)
    full = lambda shape: pl.BlockSpec(shape, lambda b, p, pt: (0,) * len(shape))
    grid_spec = pltpu.PrefetchScalarGridSpec(
        num_scalar_prefetch=1,
        grid=(bsz, n_pages),
        in_specs=[pl.BlockSpec((1, page, 2 * KV_DIM), lambda b, p, pt: (pt[b, p], 0, 0)),
                  full(pe2.shape), full(w1p.shape), full(w2.shape)],
        out_specs=[pl.BlockSpec((1, rows // CMP_STRIDE, KV_DIM), lambda b, p, pt: (b, 0, 0)),
                   pl.BlockSpec((1, rows // CMP_STRIDE, KV_DIM), lambda b, p, pt: (b, 0, 0))],
        scratch_shapes=[pltpu.VMEM((N_COMBO, rows + CMP_STRIDE, LANES), F32)],
    )
    return pl.pallas_call(
        functools.partial(_compress_kernel, n_pages=n_pages, page=page),
        grid_spec=grid_spec,
        out_shape=[out, out],
        compiler_params=_cparams(("parallel", "arbitrary")),
        name="compress",
    )(page_table, pool, pe2, w1p, w2)


def _stack_gqa(q_ref, h):
    return jnp.concatenate([q_ref[:, (h * GQA + g) * HEAD_DIM:(h * GQA + g + 1) * HEAD_DIM] for g in range(GQA)],
                           axis=0)


def _cmp_attn_kernel(q_ref, kc_ref, vc_ref, gate_ref, ov_ref, o_ref, sel_ref, *, tq, pos0, n_iter, n_past_blk):
    i = pl.program_id(1)
    ncmp = kc_ref.shape[1]
    qpos = pos0 + i * tq + lax.broadcasted_iota(jnp.int32, (tq, 1), 0)
    cend = lax.broadcasted_iota(jnp.int32, (1, ncmp), 1) * CMP_STRIDE + (CMP_LEN - 1)
    mask = cend <= qpos
    gate = _sigmoid(gate_ref[...])
    lane = lax.broadcasted_iota(jnp.int32, (tq, LANES), 1)
    lane_f = lane.astype(F32)
    if n_past_blk is None:
        cur = qpos // SEL_BLK
        forced = (lane == 0) | (lane == cur) | (lane == cur - 1)
        allowed = lane <= cur
    else:
        forced = (lane == 0) | (lane == n_past_blk - 1)
        allowed = lane < n_past_blk
    for h in range(N_KV):
        kh = kc_ref[0, :, h * HEAD_DIM:(h + 1) * HEAD_DIM]
        vh = vc_ref[0, :, h * HEAD_DIM:(h + 1) * HEAD_DIM]
        q = _stack_gqa(q_ref, h)
        s = lax.dot_general(q, kh, _NT, preferred_element_type=F32).reshape(GQA, tq, ncmp)
        s = jnp.where(mask[None], s, NEG_INF)
        m = jnp.max(s, axis=-1, keepdims=True)
        e = jnp.where(mask[None], jnp.exp(s - m), 0.0)
        l = jnp.sum(e, axis=-1, keepdims=True)
        p = e * jnp.where(l > 0.0, 1.0 / l, 0.0)
        o = jnp.dot(p.reshape(GQA * tq, ncmp), vh, preferred_element_type=F32)
        for g in range(GQA):
            hq = h * GQA + g
            col = GATE_COL0 + hq * 3
            o_ref[:, hq * HEAD_DIM:(hq + 1) * HEAD_DIM] = o[g * tq:(g + 1) * tq] * gate[:, col:col + 1]
        psum = jnp.sum(p, axis=0)
        imp = jnp.dot(psum, ov_ref[...], precision=lax.Precision.HIGHEST, preferred_element_type=F32)
        vals = jnp.where(forced, BIG, jnp.where(allowed, imp, NEG_INF))

        def body(_, carry):
            vals, sel = carry
            mx = jnp.max(vals, axis=-1, keepdims=True)
            cand = jnp.where(vals == mx, lane_f, float(LANES))
            first = jnp.min(cand, axis=-1, keepdims=True)
            hit = lane_f == first
            return jnp.where(hit, TAKEN, vals), jnp.where(hit, 1.0, sel)

        _, sel = lax.fori_loop(0, n_iter, body, (vals, jnp.zeros((tq, LANES), F32)))
        sel_ref[0, h] = sel


def _cmp_attn(q, kc, vc, us, ov, bsz, t, tq, pos0, n_iter, n_past_blk):
    nq = t // tq
    ncmp = kc.shape[1]
    return pl.pallas_call(
        functools.partial(_cmp_attn_kernel, tq=tq, pos0=pos0, n_iter=n_iter, n_past_blk=n_past_blk),
        grid=(bsz, nq),
        in_specs=[pl.BlockSpec((tq, D_MODEL), lambda b, i: (b * nq + i, 0)),
                  pl.BlockSpec((1, ncmp, KV_DIM), lambda b, i: (b, 0, 0)),
                  pl.BlockSpec((1, ncmp, KV_DIM), lambda b, i: (b, 0, 0)),
                  pl.BlockSpec((tq, LANES), lambda b, i: (b * nq + i, 0)),
                  pl.BlockSpec(ov.shape, lambda b, i: (0, 0))],
        out_specs=[pl.BlockSpec((tq, D_MODEL), lambda b, i: (b * nq + i, 0)),
                   pl.BlockSpec((1, N_KV, tq, LANES), lambda b, i: (b, 0, i, 0))],
        out_shape=[jax.ShapeDtypeStruct((bsz * t, D_MODEL), F32),
                   jax.ShapeDtypeStruct((bsz, N_KV, t, LANES), F32)],
        compiler_params=_cparams(("parallel", "parallel")),
        name="cmp_attn",
    )(q, kc, vc, us, ov)


def _flash_init(m_ref, l_ref, acc_ref):
    m_ref[...] = jnp.full(m_ref.shape, NEG_INF, F32)
    l_ref[...] = jnp.zeros(l_ref.shape, F32)
    acc_ref[...] = jnp.zeros(acc_ref.shape, F32)


def _flash_update(h, q, k, v, mask, m_ref, l_ref, acc_ref):
    tq, tk = mask.shape
    s = lax.dot_general(q, k, _NT, preferred_element_type=F32).reshape(GQA, tq, tk)
    s = jnp.where(mask[None], s, NEG_INF)
    m_old = m_ref[h]
    m_new = jnp.maximum(m_old, jnp.max(s, axis=-1, keepdims=True))
    e = jnp.where(mask[None], jnp.exp(s - m_new), 0.0)
    alpha = jnp.exp(m_old - m_new)
    l_ref[h] = alpha * l_ref[h] + jnp.sum(e, axis=-1, keepdims=True)
    pv = jnp.dot(e.reshape(GQA * tq, tk), v, preferred_element_type=F32).reshape(GQA, tq, HEAD_DIM)
    acc_ref[h] = alpha * acc_ref[h] + pv
    m_ref[h] = m_new


def _flash_finish(o_ref, gate, branch, m_ref, l_ref, acc_ref):
    for h in range(N_KV):
        l = l_ref[h]
        o = acc_ref[h] * jnp.where(l > 0.0, 1.0 / l, 0.0)
        for g in range(GQA):
            hq = h * GQA + g
            col = GATE_COL0 + hq * 3 + branch
            o_ref[:, hq * HEAD_DIM:(hq + 1) * HEAD_DIM] = o[g] * gate[:, col:col + 1]


def _sel_expand(sel, k0, tk):
    blk = lax.broadcasted_iota(jnp.int32, (LANES, tk), 0)
    col = lax.broadcasted_iota(jnp.int32, (LANES, tk), 1)
    expand = (blk == (k0 + col) // SEL_BLK).astype(BF16)
    return jnp.dot(sel.astype(BF16), expand, preferred_element_type=F32) > 0.5


def _band_attn_kernel(q_ref, kv_ref, gate_ref, *rest, tq, tk, nk, nback, branch, use_sel):
    if use_sel:
        sel_ref, o_ref, m_ref, l_ref, acc_ref = rest
    else:
        o_ref, m_ref, l_ref, acc_ref = rest
    i = pl.program_id(1)
    kk = pl.program_id(2)

    @pl.when(kk == 0)
    def _():
        _flash_init(m_ref, l_ref, acc_ref)

    if use_sel:
        kt = kk
        live = kt * tk <= i * tq + tq - 1
    else:
        kt = i * (tq // tk) - nback + kk
        live = kt >= 0

    @pl.when(live)
    def _():
        k0 = kt * tk
        qpos = i * tq + lax.broadcasted_iota(jnp.int32, (tq, 1), 0)
        kpos = k0 + lax.broadcasted_iota(jnp.int32, (1, tk), 1)
        if use_sel:
            mask = kpos <= qpos
        else:
            dlt = qpos - kpos
            mask = (dlt >= 0) & (dlt < WINDOW)
        for h in range(N_KV):
            mh = mask & _sel_expand(sel_ref[0, h], k0, tk) if use_sel else mask
            _flash_update(h, _stack_gqa(q_ref, h), kv_ref[:, h * HEAD_DIM:(h + 1) * HEAD_DIM],
                          kv_ref[:, KV_DIM + h * HEAD_DIM: KV_DIM + (h + 1) * HEAD_DIM], mh, m_ref, l_ref, acc_ref)

    @pl.when(kk == nk - 1)
    def _():
        _flash_finish(o_ref, _sigmoid(gate_ref[...]), branch, m_ref, l_ref, acc_ref)


def _band_attn(q, kv, us, sel, bsz, t, tq, tk, use_sel):
    nq = t // tq
    nkt = t // tk
    if use_sel:
        nk, nback, branch = nkt, 0, 1
        kv_idx = lambda b, i, kk: (b * nkt + jnp.minimum(kk, (i * tq + tq - 1) // tk), 0)
    else:
        nback = -(-(WINDOW - 1) // tk)
        nk, branch = nback + tq // tk, 2
        kv_idx = lambda b, i, kk: (b * nkt + jnp.maximum(i * (tq // tk) - nback + kk, 0), 0)
    in_specs = [pl.BlockSpec((tq, D_MODEL), lambda b, i, kk: (b * nq + i, 0)),
                pl.BlockSpec((tk, 2 * KV_DIM), kv_idx),
                pl.BlockSpec((tq, LANES), lambda b, i, kk: (b * nq + i, 0))]
    args = [q, kv, us]
    if use_sel:
        in_specs.append(pl.BlockSpec((1, N_KV, tq, LANES), lambda b, i, kk: (b, 0, i, 0)))
        args.append(sel)
    return pl.pallas_call(
        functools.partial(_band_attn_kernel, tq=tq, tk=tk, nk=nk, nback=nback, branch=branch, use_sel=use_sel),
        grid=(bsz, nq, nk),
        in_specs=in_specs,
        out_specs=pl.BlockSpec((tq, D_MODEL), lambda b, i, kk: (b * nq + i, 0)),
        out_shape=jax.ShapeDtypeStruct((bsz * t, D_MODEL), F32),
        scratch_shapes=[pltpu.VMEM((N_KV, GQA, tq, 1), F32), pltpu.VMEM((N_KV, GQA, tq, 1), F32),
                        pltpu.VMEM((N_KV, GQA, tq, HEAD_DIM), F32)],
        compiler_params=_cparams(("parallel", "parallel", "arbitrary")),
        name="sel_attn" if use_sel else "win_attn",
    )(*args)


def _sel_paged_kernel(pt_ref, q_ref, page_ref, new_ref, gate_ref, sel_ref, o_ref, m_ref, l_ref, acc_ref,
                      *, tq, n_pages, page):
    p = pl.program_id(1)

    @pl.when(p == 0)
    def _():
        _flash_init(m_ref, l_ref, acc_ref)

    @pl.when(p < n_pages)
    def _():
        for h in range(N_KV):
            mh = _sel_expand(sel_ref[0, h], p * page, page)
            _flash_update(h, _stack_gqa(q_ref, h), page_ref[0, :, h * HEAD_DIM:(h + 1) * HEAD_DIM],
                          page_ref[0, :, KV_DIM + h * HEAD_DIM: KV_DIM + (h + 1) * HEAD_DIM], mh,
                          m_ref, l_ref, acc_ref)

    @pl.when(p == n_pages)
    def _():
        qi = lax.broadcasted_iota(jnp.int32, (tq, tq), 0)
        ki = lax.broadcasted_iota(jnp.int32, (tq, tq), 1)
        mask = ki <= qi
        for h in range(N_KV):
            _flash_update(h, _stack_gqa(q_ref, h), new_ref[:, h * HEAD_DIM:(h + 1) * HEAD_DIM],
                          new_ref[:, KV_DIM + h * HEAD_DIM: KV_DIM + (h + 1) * HEAD_DIM], mask,
                          m_ref, l_ref, acc_ref)
        _flash_finish(o_ref, _sigmoid(gate_ref[...]), 1, m_ref, l_ref, acc_ref)


def _sel_paged(q, pool, page_table, new_rows, us, sel, tq):
    bsz, n_pages = page_table.shape
    page = pool.shape[1]
    grid_spec = pltpu.PrefetchScalarGridSpec(
        num_scalar_prefetch=1,
        grid=(bsz, n_pages + 1),
        in_specs=[pl.BlockSpec((tq, D_MODEL), lambda b, p, pt: (b, 0)),
                  pl.BlockSpec((1, page, 2 * KV_DIM), lambda b, p, pt: (pt[b, jnp.minimum(p, n_pages - 1)], 0, 0)),
                  pl.BlockSpec((tq, 2 * KV_DIM), lambda b, p, pt: (b, 0)),
                  pl.BlockSpec((tq, LANES), lambda b, p, pt: (b, 0)),
                  pl.BlockSpec((1, N_KV, tq, LANES), lambda b, p, pt: (b, 0, 0, 0))],
        out_specs=pl.BlockSpec((tq, D_MODEL), lambda b, p, pt: (b, 0)),
        scratch_shapes=[pltpu.VMEM((N_KV, GQA, tq, 1), F32), pltpu.VMEM((N_KV, GQA, tq, 1), F32),
                        pltpu.VMEM((N_KV, GQA, tq, HEAD_DIM), F32)],
    )
    return pl.pallas_call(
        functools.partial(_sel_paged_kernel, tq=tq, n_pages=n_pages, page=page),
        grid_spec=grid_spec,
        out_shape=jax.ShapeDtypeStruct((bsz * tq, D_MODEL), F32),
        compiler_params=_cparams(("parallel", "arbitrary")),
        name="sel_paged",
    )(page_table, q, pool, new_rows, us, sel)


def _win_cached_kernel(q_ref, wc_ref, new_ref, gate_ref, o_ref, m_ref, l_ref, acc_ref, *, tq, wlen):
    _flash_init(m_ref, l_ref, acc_ref)
    qi = lax.broadcasted_iota(jnp.int32, (tq, wlen), 0)
    kj = lax.broadcasted_iota(jnp.int32, (tq, wlen), 1)
    mask_c = (wlen - kj + qi) < WINDOW
    qn = lax.broadcasted_iota(jnp.int32, (tq, tq), 0)
    kn = lax.broadcasted_iota(jnp.int32, (tq, tq), 1)
    mask_n = kn <= qn
    for h in range(N_KV):
        q = _stack_gqa(q_ref, h)
        _flash_update(h, q, wc_ref[0, :, h * HEAD_DIM:(h + 1) * HEAD_DIM],
                      wc_ref[0, :, KV_DIM + h * HEAD_DIM: KV_DIM + (h + 1) * HEAD_DIM], mask_c, m_ref, l_ref, acc_ref)
        _flash_update(h, q, new_ref[:, h * HEAD_DIM:(h + 1) * HEAD_DIM],
                      new_ref[:, KV_DIM + h * HEAD_DIM: KV_DIM + (h + 1) * HEAD_DIM], mask_n, m_ref, l_ref, acc_ref)
    _flash_finish(o_ref, _sigmoid(gate_ref[...]), 2, m_ref, l_ref, acc_ref)


def _win_cached(q, wcache, new_rows, us, tq):
    bsz, wlen = wcache.shape[:2]
    return pl.pallas_call(
        functools.partial(_win_cached_kernel, tq=tq, wlen=wlen),
        grid=(bsz,),
        in_specs=[pl.BlockSpec((tq, D_MODEL), lambda b: (b, 0)),
                  pl.BlockSpec((1, wlen, 2 * KV_DIM), lambda b: (b, 0, 0)),
                  pl.BlockSpec((tq, 2 * KV_DIM), lambda b: (b, 0)),
                  pl.BlockSpec((tq, LANES), lambda b: (b, 0))],
        out_specs=pl.BlockSpec((tq, D_MODEL), lambda b: (b, 0)),
        out_shape=jax.ShapeDtypeStruct((bsz * tq, D_MODEL), F32),
        scratch_shapes=[pltpu.VMEM((N_KV, GQA, tq, 1), F32), pltpu.VMEM((N_KV, GQA, tq, 1), F32),
                        pltpu.VMEM((N_KV, GQA, tq, HEAD_DIM), F32)],
        compiler_params=_cparams(("parallel",)),
        name="win_cached",
    )(q, wcache, new_rows, us)


def _merge_kernel(x_ref, yn_ref, ysc_ref, oc_ref, os_ref, ow_ref, g1_ref, g2_ref, g3_ref,
                  wssd_ref, wsc_ref, wnsa_ref, wout_ref, o_ref):
    y_ssd = jnp.dot(yn_ref[...], wssd_ref[...], preferred_element_type=F32)
    y_sc = jnp.dot(ysc_ref[...], wsc_ref[...], preferred_element_type=F32)
    o = (oc_ref[...] + os_ref[...] + ow_ref[...]).astype(BF16)
    y_nsa = jnp.dot(o, wnsa_ref[...], preferred_element_type=F32)
    mix = _sigmoid(g1_ref[...]) * y_ssd + _sigmoid(g2_ref[...]) * y_sc + _sigmoid(g3_ref[...]) * y_nsa
    o_ref[...] = x_ref[...] + jnp.dot(mix.astype(BF16), wout_ref[...], preferred_element_type=F32)


def _merge(x, yn, ysc, oc, osel, ow, u, w, tm):
    m = x.shape[0]
    row = lambda c: pl.BlockSpec((tm, D_MODEL), lambda i: (i, c))
    wspec = pl.BlockSpec((D_MODEL, D_MODEL), lambda i: (0, 0))
    gcol = U_G // D_MODEL
    return pl.pallas_call(
        _merge_kernel,
        grid=(m // tm,),
        in_specs=[row(0)] * 6 + [row(gcol), row(gcol + 1), row(gcol + 2)] + [wspec] * 4,
        out_specs=row(0),
        out_shape=jax.ShapeDtypeStruct((m, D_MODEL), F32),
        compiler_params=_cparams(("parallel",)),
        name="merge",
    )(x, yn, ysc, oc, osel, ow, u, u, u, w["w_ssd_out"], w["w_sconv_out"], w["w_nsa_out"], w["w_out"])


def _mlp_kernel(x_ref, g_ref, wup_ref, wdn_ref, o_ref, h_ref, acc_ref, *, nf):
    j = pl.program_id(1)

    @pl.when(j == 0)
    def _():
        x = x_ref[...]
        ms = jnp.mean(x * x, axis=-1, keepdims=True)
        h_ref[...] = (x * lax.rsqrt(ms + EPS) * g_ref[...]).astype(h_ref.dtype)
        acc_ref[...] = jnp.zeros(acc_ref.shape, F32)

    up = jnp.dot(h_ref[...], wup_ref[...], preferred_element_type=F32)
    a = jnp.square(jnp.maximum(up, 0.0)).astype(BF16)
    acc_ref[...] += jnp.dot(a, wdn_ref[...], preferred_element_type=F32)

    @pl.when(j == nf - 1)
    def _():
        o_ref[...] = x_ref[...] + acc_ref[...]


def _mlp(x, g, wup, wdn, tm, tf):
    m = x.shape[0]
    nf = D_FF // tf
    return pl.pallas_call(
        functools.partial(_mlp_kernel, nf=nf),
        grid=(m // tm, nf),
        in_specs=[pl.BlockSpec((tm, D_MODEL), lambda i, j: (i, 0)),
                  pl.BlockSpec((1, D_MODEL), lambda i, j: (0, 0)),
                  pl.BlockSpec((D_MODEL, tf), lambda i, j: (0, j)),
                  pl.BlockSpec((tf, D_MODEL), lambda i, j: (j, 0))],
        out_specs=pl.BlockSpec((tm, D_MODEL), lambda i, j: (i, 0)),
        out_shape=jax.ShapeDtypeStruct((m, D_MODEL), F32),
        scratch_shapes=[pltpu.VMEM((tm, D_MODEL), BF16), pltpu.VMEM((tm, D_MODEL), F32)],
        compiler_params=_cparams(("parallel", "arbitrary")),
        name="mlp",
    )(x, g, wup, wdn)


def _pad_lanes(v, width=LANES):
    v = v.reshape(1, -1).astype(F32)
    return jnp.pad(v, ((0, 0), (0, width - v.shape[1])))


def _pad_rows(v, rows=SUBLANES, front=False):
    pad = rows - v.shape[-2]
    cfg = [(0, 0)] * (v.ndim - 2) + [((pad, 0) if front else (0, pad)), (0, 0)]
    return jnp.pad(v, cfg)


def _layer_weights(l, w_in, norm_mix, norm_mlp, ssd_conv_w, ssd_conv_b, ssd_dt_bias, ssd_a_log, ssd_d, ssd_norm,
                   w_ssd_out, sconv_w, w_sconv_out, cmp_pe, cmp_w1, cmp_w2, w_nsa_out, w_out, w_mlp_up, w_mlp_down):
    b = np.cumsum((0,) + IN_SIZES)
    wi = w_in[l]
    seg = lambda k: wi[:, b[k]:b[k + 1]]
    w_main = jnp.concatenate([seg(1), seg(7), seg(0), seg(3), seg(4), seg(5), seg(6), seg(9)], axis=1).astype(BF16)
    w_small = jnp.concatenate([seg(2), seg(8)], axis=1)
    w_small = jnp.pad(w_small, ((0, 0), (0, LANES - w_small.shape[1]))).astype(BF16)
    return dict(
        w_main=w_main, w_small=w_small,
        norm_mix=norm_mix[l].reshape(1, -1), norm_mlp=norm_mlp[l].reshape(1, -1),
        convw=_pad_rows(ssd_conv_w[l]), convb=ssd_conv_b[l].reshape(1, -1),
        dtb=_pad_lanes(ssd_dt_bias[l]), alog=_pad_lanes(ssd_a_log[l]), dvec=_pad_lanes(ssd_d[l]),
        ssdn=ssd_norm[l].reshape(1, -1), scw=_pad_rows(sconv_w[l]),
        w_ssd_out=w_ssd_out[l].astype(BF16), w_sconv_out=w_sconv_out[l].astype(BF16),
        w_nsa_out=w_nsa_out[l].astype(BF16), w_out=w_out[l].astype(BF16),
        cmp_pe=cmp_pe[l].reshape(2, 2, CMP_STRIDE, HEAD_DIM).transpose(0, 2, 1, 3).reshape(2, CMP_STRIDE, LANES),
        cmp_w1=cmp_w1[l].reshape(2, 2, CMP_STRIDE, HEAD_DIM, CMP_HID).transpose(0, 2, 1, 3, 4).reshape(
            2, CMP_STRIDE, LANES, CMP_HID),
        cmp_w2=cmp_w2[l],
        w_mlp_up=w_mlp_up[l].astype(BF16), w_mlp_down=w_mlp_down[l].astype(BF16),
    )


def _rope_tables(pos):
    half = HEAD_DIM // 2
    inv = ROPE_THETA ** (-jnp.arange(half, dtype=F32) / half)
    ang = pos.astype(F32)[:, None] * inv[None, :]
    cos, sin = jnp.cos(ang), jnp.sin(ang)
    cos_t = jnp.concatenate([cos, cos, cos, cos], axis=1)
    sin_t = jnp.concatenate([-sin, sin, -sin, sin], axis=1)
    return cos_t, sin_t


def _overlap(ncmp_pad, ncmp):
    c0 = np.arange(ncmp_pad)[:, None] * CMP_STRIDE
    s0 = np.arange(LANES)[None, :] * SEL_BLK
    ov = np.maximum(np.minimum(c0 + CMP_LEN, s0 + SEL_BLK) - np.maximum(c0, s0), 0).astype(np.float32) / CMP_LEN
    ov[ncmp:] = 0.0
    return jnp.asarray(ov)


def _tile(m, pref):
    t = pref
    while m % t:
        t //= 2
    return t


def _front(x2d, w, tm):
    u = _norm_matmul(x2d, w["norm_mix"], w["w_main"], tm, N_MAIN // 8)
    us = _norm_matmul(x2d, w["norm_mix"], w["w_small"], tm, LANES)
    return u, us


def _prompt_layer(x2d, w, bsz, t, tabs):
    m = bsz * t
    u, us = _front(x2d, w, _tile(m, 1024))
    cos_t, sin_t, ov = tabs
    q, cmp_rows, sel_rows, win_rows = _rope(u, cos_t, sin_t, _tile(m, 512))
    zeros = lambda *s: jnp.zeros(s, F32)
    yn, ysc, h_new, ch_last = _seqmix(u, us, zeros(bsz, SUBLANES, CONV_DIM), zeros(bsz, SUBLANES, D_MODEL),
                                      zeros(bsz, D_MODEL, SSD_STATE), w, bsz, t, SSD_CHUNK)
    n_pg = t // CMP_PAGE
    kc, vc = _compress(cmp_rows.reshape(bsz * n_pg, CMP_PAGE, 2 * KV_DIM),
                       jnp.arange(bsz * n_pg, dtype=jnp.int32).reshape(bsz, n_pg),
                       w["cmp_pe"], w["cmp_w1"], w["cmp_w2"])
    tq = _tile(t, 128)
    oc, sel = _cmp_attn(q, kc, vc, us, ov, bsz, t, tq, 0, N_TOP, None)
    ta = _tile(t, 256)
    osel = _band_attn(q, sel_rows, us, sel, bsz, t, ta, ta, True)
    ow = _band_attn(q, win_rows, us, None, bsz, t, ta, ta, False)
    x1 = _merge(x2d, yn, ysc, oc, osel, ow, u, w, _tile(m, 256))
    x2 = _mlp(x1, w["norm_mlp"], w["w_mlp_up"], w["w_mlp_down"], _tile(m, 512), 1024)
    wl = min(WINDOW, t)
    u3 = u.reshape(bsz, t, N_MAIN)
    state = (cmp_rows.reshape(bsz, t, 2, N_KV, HEAD_DIM), sel_rows.reshape(bsz, t, 2, N_KV, HEAD_DIM),
             win_rows.reshape(bsz, t, 2 * KV_DIM)[:, t - wl:].reshape(bsz, wl, 2, N_KV, HEAD_DIM),
             h_new.reshape(bsz, H_SSD, SSD_HEAD_DIM, SSD_STATE),
             u3[:, t - (SSD_CONV - 1):, U_XBC:U_XBC + CONV_DIM],
             ch_last[:, SUBLANES - (SC_WIDTH - 1):])
    return x2, state


def _sample_layer(x2d, w, bsz, t, tabs, past):
    ssm0, conv0, sc0, cmp_pool, sel_pool, win_cache, page_table = past
    m = bsz * t
    u, us = _front(x2d, w, m)
    cos_t, sin_t, ov = tabs
    q, cmp_rows, sel_rows, win_rows = _rope(u, cos_t, sin_t, m)
    L = SSD_CHUNK
    padr = lambda a: jnp.pad(a.reshape(bsz, t, a.shape[-1]), ((0, 0), (0, L - t), (0, 0))).reshape(bsz * L, a.shape[-1])
    yn, ysc, h_new, ch_last = _seqmix(padr(u), padr(us), _pad_rows(conv0, front=True), _pad_rows(sc0, front=True),
                                      ssm0.reshape(bsz, D_MODEL, SSD_STATE), w, bsz, L, t)
    yn = yn.reshape(bsz, L, D_MODEL)[:, :t].reshape(m, D_MODEL)
    ysc = ysc.reshape(bsz, L, D_MODEL)[:, :t].reshape(m, D_MODEL)
    kc, vc = _compress(cmp_pool, page_table, w["cmp_pe"], w["cmp_w1"], w["cmp_w2"])
    n_past_blk = page_table.shape[1] * cmp_pool.shape[1] // SEL_BLK
    oc, sel = _cmp_attn(q, kc, vc, us, ov, bsz, t, t, page_table.shape[1] * cmp_pool.shape[1], N_TOP - 1, n_past_blk)
    osel = _sel_paged(q, sel_pool, page_table, sel_rows, us, sel, t)
    ow = _win_cached(q, win_cache, win_rows, us, t)
    x1 = _merge(x2d, yn, ysc, oc, osel, ow, u, w, m)
    x2 = _mlp(x1, w["norm_mlp"], w["w_mlp_up"], w["w_mlp_down"], m, 1024)
    wl = win_cache.shape[1]
    win_all = jnp.concatenate([win_cache, win_rows.reshape(bsz, t, 2 * KV_DIM)], axis=1)
    keep = min(WINDOW, wl + t)
    u3 = u.reshape(bsz, t, N_MAIN)
    state = (cmp_rows.reshape(bsz, t, 2, N_KV, HEAD_DIM), sel_rows.reshape(bsz, t, 2, N_KV, HEAD_DIM),
             win_all[:, wl + t - keep:].reshape(bsz, keep, 2, N_KV, HEAD_DIM),
             h_new.reshape(bsz, H_SSD, SSD_HEAD_DIM, SSD_STATE),
             u3[:, t - (SSD_CONV - 1):, U_XBC:U_XBC + CONV_DIM],
             ch_last[:, SUBLANES - (SC_WIDTH - 1):])
    return x2, state


def kernel(x_prompt, x_sample, cache_cmp_kv, cache_sel_kv, cache_win_kv, state_ssm, state_ssd_conv, state_sconv,
           page_table, norm_mix, norm_mlp, norm_final, w_in, ssd_conv_w, ssd_conv_b, ssd_dt_bias, ssd_a_log, ssd_d,
           ssd_norm, w_ssd_out, sconv_w, w_sconv_out, cmp_pe, cmp_w1, cmp_w2, w_nsa_out, w_out, w_mlp_up,
           w_mlp_down):
    depth = w_in.shape[0]
    bp, tp, _ = x_prompt.shape
    bs, ts, _ = x_sample.shape
    n_pool, page = cache_cmp_kv.shape[1:3]
    past_len = page_table.shape[1] * page
    wl = cache_win_kv.shape[2]
    assert tp % (CMP_TILE * CMP_STRIDE) == 0 and tp // SEL_BLK <= LANES
    assert ts == SUBLANES and past_len % (CMP_TILE * CMP_STRIDE) == 0 and past_len // SEL_BLK <= LANES
    assert wl == WINDOW and past_len >= WINDOW
    assert (past_len + ts - CMP_LEN) // CMP_STRIDE + 1 == (past_len - CMP_LEN) // CMP_STRIDE + 1

    tabs_p = _rope_tables(jnp.tile(jnp.arange(tp), bp)) + (_overlap(tp // CMP_STRIDE, (tp - CMP_LEN) // CMP_STRIDE + 1),)
    tabs_s = _rope_tables(jnp.tile(past_len + jnp.arange(ts), bs)) + (
        _overlap(past_len // CMP_STRIDE, (past_len - CMP_LEN) // CMP_STRIDE + 1),)

    xp = x_prompt.reshape(bp * tp, D_MODEL)
    xs = x_sample.reshape(bs * ts, D_MODEL)
    p_new = [[] for _ in range(6)]
    s_new = [[] for _ in range(6)]
    for l in range(depth):
        w = _layer_weights(l, w_in, norm_mix, norm_mlp, ssd_conv_w, ssd_conv_b, ssd_dt_bias, ssd_a_log, ssd_d,
                           ssd_norm, w_ssd_out, sconv_w, w_sconv_out, cmp_pe, cmp_w1, cmp_w2, w_nsa_out, w_out,
                           w_mlp_up, w_mlp_down)
        xp, st_p = _prompt_layer(xp, w, bp, tp, tabs_p)
        past = (state_ssm[l], state_ssd_conv[l], state_sconv[l],
                cache_cmp_kv[l].reshape(n_pool, page, 2 * KV_DIM), cache_sel_kv[l].reshape(n_pool, page, 2 * KV_DIM),
                cache_win_kv[l].reshape(bs, wl, 2 * KV_DIM), page_table)
        xs, st_s = _sample_layer(xs, w, bs, ts, tabs_s, past)
        for i in range(6):
            p_new[i].append(st_p[i])
            s_new[i].append(st_s[i])
    gf = norm_final.reshape(1, -1)
    y_prompt = _final_norm(xp, gf, _tile(bp * tp, 1024)).reshape(bp, tp, D_MODEL)
    y_sample = _final_norm(xs, gf, bs * ts).reshape(bs, ts, D_MODEL)
    return (y_prompt, y_sample) + tuple(jnp.stack(a) for a in p_new) + tuple(jnp.stack(a) for a in s_new)
```

```python
import functools
import math

import numpy as np
import jax
import jax.numpy as jnp
from jax import lax
from jax.experimental import pallas as pl
from jax.experimental.pallas import tpu as pltpu

F32 = jnp.float32
BF16 = jnp.bfloat16

D_MODEL = 1024
SSD_HEAD_DIM = 64
H_SSD = D_MODEL // SSD_HEAD_DIM
SSD_STATE = 128
SSD_GROUPS = 2
SSD_CONV = 4
SSD_CHUNK = 128
CONV_DIM = D_MODEL + 2 * SSD_GROUPS * SSD_STATE
SC_WIDTH = 3
HEAD_DIM = 64
H_ATT = D_MODEL // HEAD_DIM
N_KV = 4
GQA = H_ATT // N_KV
KV_DIM = N_KV * HEAD_DIM
CMP_LEN = 32
CMP_STRIDE = 16
CMP_HID = 4 * HEAD_DIM
SEL_BLK = 64
N_TOP = 16
WINDOW = 512
D_FF = 4 * D_MODEL
ROPE_THETA = 10000.0
EPS = 1e-6
NEG_INF = -1e30
BIG = 1e9
TAKEN = -3e38
LOG2E = 1.4426950408889634
MASKV = -1e30
M_INIT = -1e29
IN_SIZES = (D_MODEL, CONV_DIM, H_SSD, D_MODEL, D_MODEL, D_MODEL, D_MODEL, 6 * KV_DIM, 3 * H_ATT, 3 * D_MODEL)

LANES = 128
SUBLANES = 8
VMEM_LIMIT = 56 * 1024 * 1024

U_XBC, U_KV, U_Z, U_SCB, U_SCC, U_SCH, U_Q, U_G = 0, 1536, 3072, 4096, 5120, 6144, 7168, 8192
N_MAIN = 11264
GATE_COL0 = H_SSD

_NT = (((1,), (1,)), ((), ()))
_TN = (((0,), (0,)), ((), ()))


def _cparams(sem):
    return pltpu.CompilerParams(dimension_semantics=sem, vmem_limit_bytes=VMEM_LIMIT)


def _silu(x):
    return x * (1.0 / (1.0 + jnp.exp(-x)))


def _sigmoid(x):
    return 1.0 / (1.0 + jnp.exp(-x))


def _norm_matmul_kernel(x_ref, g_ref, w_ref, o_ref, h_ref):
    @pl.when(pl.program_id(1) == 0)
    def _():
        x = x_ref[...]
        ms = jnp.mean(x * x, axis=-1, keepdims=True)
        h_ref[...] = (x * lax.rsqrt(ms + EPS) * g_ref[...]).astype(h_ref.dtype)

    o_ref[...] = jnp.dot(h_ref[...], w_ref[...], preferred_element_type=F32).astype(o_ref.dtype)


def _norm_matmul(x, g, w, tm, tn):
    m, d = x.shape
    n = w.shape[1]
    return pl.pallas_call(
        _norm_matmul_kernel,
        grid=(m // tm, n // tn),
        in_specs=[pl.BlockSpec((tm, d), lambda i, j: (i, 0)),
                  pl.BlockSpec((1, d), lambda i, j: (0, 0)),
                  pl.BlockSpec((d, tn), lambda i, j: (0, j))],
        out_specs=pl.BlockSpec((tm, tn), lambda i, j: (i, j)),
        out_shape=jax.ShapeDtypeStruct((m, n), F32),
        scratch_shapes=[pltpu.VMEM((tm, d), BF16)],
        compiler_params=_cparams(("parallel", "arbitrary")),
        name="norm_matmul",
    )(x, g, w)


def _final_norm_kernel(x_ref, g_ref, o_ref):
    x = x_ref[...]
    ms = jnp.mean(x * x, axis=-1, keepdims=True)
    o_ref[...] = x * lax.rsqrt(ms + EPS) * g_ref[...]


def _final_norm(x, g, tm):
    m, d = x.shape
    return pl.pallas_call(
        _final_norm_kernel,
        grid=(m // tm,),
        in_specs=[pl.BlockSpec((tm, d), lambda i: (i, 0)), pl.BlockSpec((1, d), lambda i: (0, 0))],
        out_specs=pl.BlockSpec((tm, d), lambda i: (i, 0)),
        out_shape=jax.ShapeDtypeStruct((m, d), F32),
        compiler_params=_cparams(("parallel",)),
        name="final_norm",
    )(x, g)


def _rot_half(x, first_half):
    return jnp.where(first_half, pltpu.roll(x, LANES - HEAD_DIM // 2, axis=1), pltpu.roll(x, HEAD_DIM // 2, axis=1))


def _rope_kernel(q_ref, kv_ref, cos_ref, sin_ref, qz_ref, cmp_ref, sel_ref, win_ref, selb_ref, winb_ref, *, scale):
    cos = cos_ref[...]
    sin = sin_ref[...]
    lane = lax.broadcasted_iota(jnp.int32, cos.shape, 1)
    first_half = (lane % HEAD_DIM) < (HEAD_DIM // 2)
    left = lane < HEAD_DIM

    def rope(x):
        return x * cos + _rot_half(x, first_half) * sin

    for c in range(D_MODEL // LANES):
        r = rope(q_ref[:, c * LANES:(c + 1) * LANES]) * scale
        sw = pltpu.roll(r, HEAD_DIM, axis=1)
        h = (2 * c) // GQA
        g = (2 * c) % GQA
        if h % 2 == 0:
            qz_ref[h, g] = jnp.where(left, r, 0.0).astype(qz_ref.dtype)
            qz_ref[h, g + 1] = jnp.where(left, sw, 0.0).astype(qz_ref.dtype)
        else:
            qz_ref[h, g] = jnp.where(left, 0.0, sw).astype(qz_ref.dtype)
            qz_ref[h, g + 1] = jnp.where(left, 0.0, r).astype(qz_ref.dtype)
    for br, (o_ref, b_ref) in enumerate(((cmp_ref, None), (sel_ref, selb_ref), (win_ref, winb_ref))):
        base = br * 2 * KV_DIM
        for c in range(2 * KV_DIM // LANES):
            x = kv_ref[:, base + c * LANES: base + (c + 1) * LANES]
            if c < KV_DIM // LANES:
                x = rope(x)
            o_ref[:, c * LANES:(c + 1) * LANES] = x
            if b_ref is not None:
                b_ref[:, c * LANES:(c + 1) * LANES] = x.astype(b_ref.dtype)


def _rope(u, cos, sin, tm, q_dtype):
    m = u.shape[0]
    kv_out = jax.ShapeDtypeStruct((m, 2 * KV_DIM), F32)
    kv_b = jax.ShapeDtypeStruct((m, 2 * KV_DIM), BF16)
    kv_spec = pl.BlockSpec((tm, 2 * KV_DIM), lambda i: (i, 0))
    return pl.pallas_call(
        functools.partial(_rope_kernel, scale=HEAD_DIM ** -0.5 * LOG2E),
        grid=(m // tm,),
        in_specs=[pl.BlockSpec((tm, D_MODEL), lambda i: (i, U_Q // D_MODEL)),
                  pl.BlockSpec((tm, 6 * KV_DIM), lambda i: (i, U_KV // (6 * KV_DIM))),
                  pl.BlockSpec((tm, LANES), lambda i: (i, 0)),
                  pl.BlockSpec((tm, LANES), lambda i: (i, 0))],
        out_specs=[pl.BlockSpec((N_KV, GQA, tm, LANES), lambda i: (0, 0, i, 0)),
                   kv_spec, kv_spec, kv_spec, kv_spec, kv_spec],
        out_shape=[jax.ShapeDtypeStruct((N_KV, GQA, m, LANES), q_dtype), kv_out, kv_out, kv_out, kv_b, kv_b],
        compiler_params=_cparams(("parallel",)),
        name="rope",
    )(u, u, cos, sin)


def _seqmix_kernel(xbc_ref, z_ref, scb_ref, scc_ref, sch_ref, dtr_ref, conv0_ref, sc0_ref, h0_ref,
                   convw_ref, convb_ref, dtb_ref, alog_ref, dvec_ref, ssdn_ref, scw_ref,
                   yn_ref, ysc_ref, hout_ref, chlast_ref,
                   ext_ref, chext_ref, h_ref, y_ref, *, chunk, valid):
    c = pl.program_id(1)
    L = chunk
    P = SSD_HEAD_DIM
    N = SSD_STATE

    @pl.when(c == 0)
    def _():
        ext_ref[0:SUBLANES, :] = conv0_ref[0]
        chext_ref[0:SUBLANES, :] = sc0_ref[0]
        h_ref[...] = h0_ref[0]

    ext_ref[SUBLANES:SUBLANES + L, :] = xbc_ref[...]
    conv = jnp.broadcast_to(convb_ref[...], (L, CONV_DIM))
    for k in range(SSD_CONV):
        off = SUBLANES - (SSD_CONV - 1) + k
        conv = conv + convw_ref[k:k + 1, :] * ext_ref[off:off + L, :]
    ext_ref[0:SUBLANES, :] = ext_ref[L:L + SUBLANES, :]
    xbc = _silu(conv)

    ch = scc_ref[...] * sch_ref[...]
    chext_ref[SUBLANES:SUBLANES + L, :] = ch
    cv = jnp.zeros((L, D_MODEL), F32)
    for k in range(SC_WIDTH):
        off = SUBLANES - (SC_WIDTH - 1) + k
        cv = cv + scw_ref[k:k + 1, :] * chext_ref[off:off + L, :]
    ysc_ref[...] = (scb_ref[...] * cv).astype(ysc_ref.dtype)
    chlast_ref[0] = chext_ref[valid:valid + SUBLANES, :]
    chext_ref[0:SUBLANES, :] = chext_ref[L:L + SUBLANES, :]

    x = dtr_ref[...] + dtb_ref[...]
    dt = jnp.maximum(x, 0.0) + jnp.log1p(jnp.exp(-jnp.abs(x)))
    row = lax.broadcasted_iota(jnp.int32, (L, LANES), 0)
    if valid < L:
        dt = jnp.where(row < valid, dt, 0.0)
    a = -jnp.exp(alog_ref[...])
    da = dt * a
    ri = lax.broadcasted_iota(jnp.int32, (L, L), 0)
    ci = lax.broadcasted_iota(jnp.int32, (L, L), 1)
    tril = ri >= ci
    acum = jnp.dot(tril.astype(F32), da, precision=lax.Precision.HIGHEST, preferred_element_type=F32)
    acum_t = acum.T
    a_last = acum[L - 1:L, :]

    for g in range(SSD_GROUPS):
        bg = xbc[:, D_MODEL + g * N: D_MODEL + (g + 1) * N]
        cg = xbc[:, D_MODEL + SSD_GROUPS * N + g * N: D_MODEL + SSD_GROUPS * N + (g + 1) * N]
        cb = lax.dot_general(cg, bg, _NT, preferred_element_type=F32)
        for r in range(H_SSD // SSD_GROUPS):
            h = g * (H_SSD // SSD_GROUPS) + r
            a_col = acum[:, h:h + 1]
            a_row = acum_t[h:h + 1, :]
            seg = jnp.where(tril, a_col - a_row, NEG_INF)
            wts = jnp.exp(seg) * cb
            xs_h = xbc[:, h * P:(h + 1) * P]
            xdt = xs_h * dt[:, h:h + 1]
            hprev = h_ref[h * P:(h + 1) * P, :]
            y_h = jnp.dot(wts, xdt, preferred_element_type=F32)
            y_in = lax.dot_general(cg, hprev, _NT, preferred_element_type=F32)
            y_h = y_h + y_in * jnp.exp(a_col)
            y_h = y_h + dvec_ref[:, h:h + 1] * xs_h
            y_ref[:, h * P:(h + 1) * P] = y_h
            al = a_last[:, h:h + 1]
            xw = xdt * jnp.exp(al - a_col)
            s_h = lax.dot_general(xw, bg, _TN, preferred_element_type=F32)
            h_ref[h * P:(h + 1) * P, :] = jnp.exp(al) * hprev + s_h

    y = y_ref[...] * _silu(z_ref[...])
    ms = jnp.mean(y * y, axis=-1, keepdims=True)
    yn_ref[...] = (y * lax.rsqrt(ms + EPS) * ssdn_ref[...]).astype(yn_ref.dtype)
    hout_ref[0] = h_ref[...]


def _seqmix(u, us, conv0, sc0, h0, prm, bsz, t, valid):
    L = SSD_CHUNK
    nc = t // L
    m = bsz * t
    row = lambda b, c: b * nc + c
    spec_u = lambda w, col: pl.BlockSpec((L, w), lambda b, c: (row(b, c), col // w))
    full = lambda shape: pl.BlockSpec(shape, lambda b, c: (0,) * len(shape))
    perb = lambda shape: pl.BlockSpec((1,) + shape, lambda b, c: (b,) + (0,) * len(shape))
    return pl.pallas_call(
        functools.partial(_seqmix_kernel, chunk=L, valid=valid),
        grid=(bsz, nc),
        in_specs=[spec_u(CONV_DIM, U_XBC), spec_u(D_MODEL, U_Z), spec_u(D_MODEL, U_SCB), spec_u(D_MODEL, U_SCC),
                  spec_u(D_MODEL, U_SCH),
                  pl.BlockSpec((L, LANES), lambda b, c: (row(b, c), 0)),
                  perb((SUBLANES, CONV_DIM)), perb((SUBLANES, D_MODEL)), perb((D_MODEL, SSD_STATE)),
                  full((SUBLANES, CONV_DIM)), full((1, CONV_DIM)), full((1, LANES)), full((1, LANES)),
                  full((1, LANES)), full((1, D_MODEL)), full((SUBLANES, D_MODEL))],
        out_specs=[pl.BlockSpec((L, D_MODEL), lambda b, c: (row(b, c), 0)),
                   pl.BlockSpec((L, D_MODEL), lambda b, c: (row(b, c), 0)),
                   perb((D_MODEL, SSD_STATE)), perb((SUBLANES, D_MODEL))],
        out_shape=[jax.ShapeDtypeStruct((m, D_MODEL), BF16), jax.ShapeDtypeStruct((m, D_MODEL), BF16),
                   jax.ShapeDtypeStruct((bsz, D_MODEL, SSD_STATE), F32),
                   jax.ShapeDtypeStruct((bsz, SUBLANES, D_MODEL), F32)],
        scratch_shapes=[pltpu.VMEM((L + SUBLANES, CONV_DIM), F32), pltpu.VMEM((L + SUBLANES, D_MODEL), F32),
                        pltpu.VMEM((D_MODEL, SSD_STATE), F32), pltpu.VMEM((L, D_MODEL), F32)],
        compiler_params=_cparams(("parallel", "arbitrary")),
        name="seqmix",
    )(u, u, u, u, u, us, conv0, sc0, h0, prm["convw"], prm["convb"], prm["dtb"], prm["alog"], prm["dvec"],
      prm["ssdn"], prm["scw"])


CMP_TILE = 128
CMP_PAGE = 128
PAGE_GROUP = 8


N_COMBO = 2 * N_KV


def _compress_fill(z_ref, rows, g0):
    R = rows.shape[0]
    lo = pl.ds(g0 + CMP_STRIDE, R)
    hi = pl.ds(g0, R)
    for pr in range(N_COMBO // 2):
        x2 = rows[:, pr * LANES:(pr + 1) * LANES]
        sw = pltpu.roll(x2, HEAD_DIM, axis=1)
        z_ref[2 * pr, lo, 0:HEAD_DIM] = x2[:, 0:HEAD_DIM]
        z_ref[2 * pr, hi, HEAD_DIM:LANES] = sw[:, HEAD_DIM:LANES]
        z_ref[2 * pr + 1, lo, 0:HEAD_DIM] = sw[:, 0:HEAD_DIM]
        z_ref[2 * pr + 1, hi, HEAD_DIM:LANES] = x2[:, HEAD_DIM:LANES]


def _compress_tile(z_ref, row0, pe_ref, w1_ref, w2_ref, out_refs, out_row0):
    for c in range(N_COMBO):
        kv, h = divmod(c, N_KV)
        acc = jnp.zeros((CMP_TILE, CMP_HID), F32)
        for r in range(CMP_STRIDE):
            zr = z_ref[c, pl.ds(row0 + CMP_STRIDE + r, CMP_TILE, stride=CMP_STRIDE), :] + pe_ref[kv, r:r + 1, :]
            acc = acc + jnp.dot(zr.astype(BF16), w1_ref[kv, r], preferred_element_type=F32)
        o = jnp.dot(_silu(acc).astype(BF16), w2_ref[kv], preferred_element_type=F32)
        out_refs[kv][0, pl.ds(out_row0, CMP_TILE), pl.ds(h * HEAD_DIM, HEAD_DIM)] = o.astype(out_refs[kv].dtype)


def _compress_kernel(pt_ref, *refs, n_pages, page, group):
    page_refs = refs[:group]
    pe_ref, w1_ref, w2_ref, kc_ref, vc_ref, z_ref = refs[group:]
    p = pl.program_id(1)
    for k in range(group):
        _compress_fill(z_ref, page_refs[k][0], pl.multiple_of((p * group + k) * page, page))

    @pl.when(p == n_pages // group - 1)
    def _():
        rows = n_pages * page
        z_ref[:, rows:rows + CMP_STRIDE, HEAD_DIM:LANES] = jnp.zeros((N_COMBO, CMP_STRIDE, LANES - HEAD_DIM), F32)

        def body(j, carry):
            _compress_tile(z_ref, pl.multiple_of(j * CMP_TILE * CMP_STRIDE, CMP_TILE * CMP_STRIDE),
                           pe_ref, w1_ref, w2_ref, (kc_ref, vc_ref), pl.multiple_of(j * CMP_TILE, CMP_TILE))
            return carry

        lax.fori_loop(0, rows // (CMP_TILE * CMP_STRIDE), body, 0)


def _compress(pool, page_table, pe2, w1p, w2):
    bsz, n_pages = page_table.shape
    page = pool.shape[1]
    rows = n_pages * page
    group = PAGE_GROUP
    out = jax.ShapeDtypeStruct((bsz, rows // CMP_STRIDE, KV_DIM), BF16)
    full = lambda shape: pl.BlockSpec(shape, lambda b, p, pt: (0,) * len(shape))
    page_spec = lambda k: pl.BlockSpec((1, page, 2 * KV_DIM), lambda b, p, pt: (pt[b, p * group + k], 0, 0))
    grid_spec = pltpu.PrefetchScalarGridSpec(
        num_scalar_prefetch=1,
        grid=(bsz, n_pages // group),
        in_specs=[page_spec(k) for k in range(group)] + [full(pe2.shape), full(w1p.shape), full(w2.shape)],
        out_specs=[pl.BlockSpec((1, rows // CMP_STRIDE, KV_DIM), lambda b, p, pt: (b, 0, 0)),
                   pl.BlockSpec((1, rows // CMP_STRIDE, KV_DIM), lambda b, p, pt: (b, 0, 0))],
        scratch_shapes=[pltpu.VMEM((N_COMBO, rows + CMP_STRIDE, LANES), F32)],
    )
    return pl.pallas_call(
        functools.partial(_compress_kernel, n_pages=n_pages, page=page, group=group),
        grid_spec=grid_spec,
        out_shape=[out, out],
        compiler_params=_cparams(("parallel", "arbitrary")),
        name="compress",
    )(page_table, *([pool] * group), pe2, w1p, w2)


BF16_ROWS = 16


def _act_dtype(block_rows):
    return BF16 if block_rows % BF16_ROWS == 0 else F32


def _left_lanes(shape):
    return lax.broadcasted_iota(jnp.int32, shape, len(shape) - 1) < HEAD_DIM


def _own_half(left, h):
    return left if h % 2 == 0 else jnp.logical_not(left)


def _value_with_ones(v2, own):
    return jnp.where(own, v2, jnp.ones_like(v2))


def _normalise(acc, own):
    l = jnp.where(own, pltpu.roll(acc, HEAD_DIM, axis=1), acc)
    rinv = jnp.where(l > 0.0, 1.0 / l, 0.0)
    return acc * rinv, rinv


def _emit_heads(o_ref, pieces, halves):
    left = _left_lanes(pieces[0].shape)
    for c in range(H_ATT // 2):
        a, b = pieces[2 * c], pieces[2 * c + 1]
        a = a if halves[2 * c] == 0 else pltpu.roll(a, HEAD_DIM, axis=1)
        b = b if halves[2 * c + 1] == 1 else pltpu.roll(b, HEAD_DIM, axis=1)
        o_ref[:, c * LANES:(c + 1) * LANES] = jnp.where(left, a, b).astype(o_ref.dtype)


def _gate_col(gate, hq, branch):
    col = GATE_COL0 + hq * 3 + branch
    return gate[:, col:col + 1]


def _topk_lanes(vals, n_iter):
    lane_f = lax.broadcasted_iota(jnp.int32, vals.shape, 1).astype(F32)

    def body(_, carry):
        vals, sel = carry
        mx = jnp.max(vals, axis=-1, keepdims=True)
        first = jnp.min(jnp.where(vals == mx, lane_f, float(LANES)), axis=-1, keepdims=True)
        hit = lane_f == first
        return jnp.where(hit, TAKEN, vals), jnp.where(hit, 1.0, sel)

    return lax.fori_loop(0, n_iter, body, (vals, jnp.zeros(vals.shape, F32)))[1]


def _topk_rows(vals, n_iter):
    row_f = lax.broadcasted_iota(jnp.int32, vals.shape, 0).astype(F32)

    def body(_, carry):
        vals, sel = carry
        mx = jnp.max(vals, axis=0, keepdims=True)
        first = jnp.min(jnp.where(vals == mx, row_f, float(LANES)), axis=0, keepdims=True)
        hit = row_f == first
        return jnp.where(hit, TAKEN, vals), jnp.where(hit, 1.0, sel)

    return lax.fori_loop(0, n_iter, body, (vals, jnp.zeros(vals.shape, F32)))[1]


def _cmp_attn_kernel(qz_ref, kc_ref, vc_ref, gate_ref, ov_ref, o_ref, selb_ref, *, tq, pos0, n_iter, n_past_blk):
    i = pl.program_id(1)
    ncmp = kc_ref.shape[1]
    rows = GQA * tq
    qpos = pos0 + i * tq + lax.broadcasted_iota(jnp.int32, (tq, 1), 0)
    cend = lax.broadcasted_iota(jnp.int32, (1, ncmp), 1) * CMP_STRIDE + (CMP_LEN - 1)
    bias = jnp.where(cend <= qpos, 0.0, MASKV)
    gate = _sigmoid(gate_ref[...])
    left = _left_lanes((rows, LANES))
    transposed = tq % LANES == 0
    shape = (LANES, tq) if transposed else (tq, LANES)
    blk = lax.broadcasted_iota(jnp.int32, shape, 0 if transposed else 1)
    if n_past_blk is None:
        cur = (pos0 + i * tq + lax.broadcasted_iota(jnp.int32, shape, 1 if transposed else 0)) // SEL_BLK
        forced = (blk == 0) | (blk == cur) | (blk == cur - 1)
        allowed = blk <= cur
    else:
        forced = (blk == 0) | (blk == n_past_blk - 1)
        allowed = blk < n_past_blk
    pieces, halves = [None] * H_ATT, [0] * H_ATT
    for h in range(N_KV):
        pr = h // 2
        own = _own_half(left, h)
        k2 = kc_ref[0, :, pr * LANES:(pr + 1) * LANES]
        v2 = vc_ref[0, :, pr * LANES:(pr + 1) * LANES]
        q = qz_ref[h].reshape(rows, LANES).astype(BF16)
        s = lax.dot_general(q, k2, _NT, preferred_element_type=F32).reshape(GQA, tq, ncmp) + bias[None]
        s = s.reshape(rows, ncmp)
        m = jnp.maximum(jnp.max(s, axis=-1, keepdims=True), M_INIT)
        e = jnp.exp2(s - m)
        acc = jnp.dot(e.astype(BF16), _value_with_ones(v2, _own_half(_left_lanes(v2.shape), h)),
                      preferred_element_type=F32)
        o, rinv = _normalise(acc, own)
        for g in range(GQA):
            hq = h * GQA + g
            pieces[hq] = o[g * tq:(g + 1) * tq] * _gate_col(gate, hq, 0)
            halves[hq] = h % 2
        p = e * jnp.concatenate([rinv] * (ncmp // LANES), axis=1)
        psum = jnp.sum(p.reshape(GQA, tq, ncmp), axis=0)
        imp = jnp.dot(psum, ov_ref[...], precision=lax.Precision.HIGHEST, preferred_element_type=F32)
        if transposed:
            vals = jnp.where(forced, BIG, jnp.where(allowed, imp.T, NEG_INF))
            sel = _topk_rows(vals, n_iter).T
        else:
            vals = jnp.where(forced, BIG, jnp.where(allowed, imp, NEG_INF))
            sel = _topk_lanes(vals, n_iter)
        selb_ref[0, h] = jnp.where(sel > 0.5, 0.0, MASKV).astype(selb_ref.dtype)
    _emit_heads(o_ref, pieces, halves)


def _cmp_attn(qz, kc, vc, us, ov, bsz, t, tq, pos0, n_iter, n_past_blk):
    nq = t // tq
    ncmp = kc.shape[1]
    return pl.pallas_call(
        functools.partial(_cmp_attn_kernel, tq=tq, pos0=pos0, n_iter=n_iter, n_past_blk=n_past_blk),
        grid=(bsz, nq),
        in_specs=[pl.BlockSpec((N_KV, GQA, tq, LANES), lambda b, i: (0, 0, b * nq + i, 0)),
                  pl.BlockSpec((1, ncmp, KV_DIM), lambda b, i: (b, 0, 0)),
                  pl.BlockSpec((1, ncmp, KV_DIM), lambda b, i: (b, 0, 0)),
                  pl.BlockSpec((tq, LANES), lambda b, i: (b * nq + i, 0)),
                  pl.BlockSpec(ov.shape, lambda b, i: (0, 0))],
        out_specs=[pl.BlockSpec((tq, D_MODEL), lambda b, i: (b * nq + i, 0)),
                   pl.BlockSpec((1, N_KV, tq, LANES), lambda b, i: (b, 0, i, 0))],
        out_shape=[jax.ShapeDtypeStruct((bsz * t, D_MODEL), _act_dtype(tq)),
                   jax.ShapeDtypeStruct((bsz, N_KV, t, LANES), BF16)],
        compiler_params=_cparams(("parallel", "parallel")),
        name="cmp_attn",
    )(qz, kc, vc, us, ov)


def _block_expand(k0, tk):
    blk = lax.broadcasted_iota(jnp.int32, (LANES, tk), 0)
    col = lax.broadcasted_iota(jnp.int32, (LANES, tk), 1)
    return (blk == (k0 + col) // SEL_BLK).astype(BF16)


def _band_attn_kernel(qz_ref, kv_ref, gate_ref, *rest, tq, tk, nk, nback, branch, use_sel):
    if use_sel:
        selb_ref, o_ref, m_ref, acc_ref = rest
    else:
        o_ref, m_ref, acc_ref = rest
    i = pl.program_id(1)
    kk = pl.program_id(2)
    rows = GQA * tq

    @pl.when(kk == 0)
    def _():
        m_ref[...] = jnp.full(m_ref.shape, M_INIT, F32)
        acc_ref[...] = jnp.zeros(acc_ref.shape, F32)

    if use_sel:
        kt = kk
        live = kt * tk <= i * tq + tq - 1
    else:
        kt = i * (tq // tk) - nback + kk
        live = kt >= 0

    @pl.when(live)
    def _():
        k0 = kt * tk
        dlt = (i * tq + lax.broadcasted_iota(jnp.int32, (tq, tk), 0)) - (k0 + lax.broadcasted_iota(jnp.int32, (tq, tk), 1))
        if use_sel:
            pos_bias = jnp.where(dlt >= 0, 0.0, MASKV)
            expand = _block_expand(k0, tk)
        else:
            pos_bias = jnp.where((dlt >= 0) & (dlt < WINDOW), 0.0, MASKV)
        half_v = lax.broadcasted_iota(jnp.int32, (tk, LANES), 1) // HEAD_DIM

        def head(h, carry):
            pr = pl.multiple_of((h // 2) * LANES, LANES)
            own_v = half_v == h % 2
            k2 = kv_ref[:, pl.ds(pr, LANES)]
            v2 = kv_ref[:, pl.ds(KV_DIM + pr, LANES)]
            bias = pos_bias
            if use_sel:
                bias = bias + jnp.dot(selb_ref[0, h], expand, preferred_element_type=F32)
            q = qz_ref[h].reshape(rows, LANES)
            s = lax.dot_general(q, k2, _NT, preferred_element_type=F32).reshape(GQA, tq, tk) + bias[None]
            s = s.reshape(rows, tk)
            m_old = m_ref[h]
            m_new = jnp.maximum(m_old, jnp.max(s, axis=-1, keepdims=True))
            e = jnp.exp2(s - m_new[:, 0:1]).astype(BF16)
            alpha = jnp.exp2(m_old - m_new)
            acc_ref[h] = alpha * acc_ref[h] + jnp.dot(e, _value_with_ones(v2, own_v), preferred_element_type=F32)
            m_ref[h] = m_new
            return carry

        lax.fori_loop(0, N_KV, head, 0)

    @pl.when(kk == nk - 1)
    def _():
        gate = _sigmoid(gate_ref[...])
        left = _left_lanes((rows, LANES))
        pieces, halves = [None] * H_ATT, [0] * H_ATT
        for h in range(N_KV):
            o, _ = _normalise(acc_ref[h], _own_half(left, h))
            for g in range(GQA):
                hq = h * GQA + g
                pieces[hq] = o[g * tq:(g + 1) * tq] * _gate_col(gate, hq, branch)
                halves[hq] = h % 2
        _emit_heads(o_ref, pieces, halves)


def _band_attn(qz, kvb, us, selb, bsz, t, tq, tk, use_sel):
    nq = t // tq
    nkt = t // tk
    if use_sel:
        nk, nback, branch = nkt, 0, 1
        kv_idx = lambda b, i, kk: (b * nkt + jnp.minimum(kk, (i * tq + tq - 1) // tk), 0)
    else:
        nback = -(-(WINDOW - 1) // tk)
        nk, branch = nback + tq // tk, 2
        kv_idx = lambda b, i, kk: (b * nkt + jnp.maximum(i * (tq // tk) - nback + kk, 0), 0)
    in_specs = [pl.BlockSpec((N_KV, GQA, tq, LANES), lambda b, i, kk: (0, 0, b * nq + i, 0)),
                pl.BlockSpec((tk, 2 * KV_DIM), kv_idx),
                pl.BlockSpec((tq, LANES), lambda b, i, kk: (b * nq + i, 0))]
    args = [qz, kvb, us]
    if use_sel:
        in_specs.append(pl.BlockSpec((1, N_KV, tq, LANES), lambda b, i, kk: (b, 0, i, 0)))
        args.append(selb)
    return pl.pallas_call(
        functools.partial(_band_attn_kernel, tq=tq, tk=tk, nk=nk, nback=nback, branch=branch, use_sel=use_sel),
        grid=(bsz, nq, nk),
        in_specs=in_specs,
        out_specs=pl.BlockSpec((tq, D_MODEL), lambda b, i, kk: (b * nq + i, 0)),
        out_shape=jax.ShapeDtypeStruct((bsz * t, D_MODEL), BF16),
        scratch_shapes=[pltpu.VMEM((N_KV, GQA * tq, LANES), F32), pltpu.VMEM((N_KV, GQA * tq, LANES), F32)],
        compiler_params=_cparams(("parallel", "parallel", "arbitrary")),
        name="sel_attn" if use_sel else "win_attn",
    )(*args)


N_PAIR = N_KV // 2


def _pair_init(m_ref, l_ref, acc_ref):
    m_ref[...] = jnp.full(m_ref.shape, M_INIT, F32)
    l_ref[...] = jnp.zeros(l_ref.shape, F32)
    acc_ref[...] = jnp.zeros(acc_ref.shape, F32)


def _pair_queries(qz_ref, pr, tq):
    return jnp.concatenate([qz_ref[2 * pr].reshape(GQA * tq, LANES), qz_ref[2 * pr + 1].reshape(GQA * tq, LANES)],
                           axis=0).astype(BF16)


def _pair_update(pr, q, kv, bias_a, bias_b, m_ref, l_ref, acc_ref):
    k2 = kv[:, pr * LANES:(pr + 1) * LANES]
    v2 = kv[:, KV_DIM + pr * LANES: KV_DIM + (pr + 1) * LANES]
    bias = jnp.concatenate([bias_a] * GQA + [bias_b] * GQA, axis=0)
    s = lax.dot_general(q, k2, _NT, preferred_element_type=F32) + bias
    m_old = m_ref[pr]
    m_new = jnp.maximum(m_old, jnp.max(s, axis=-1, keepdims=True))
    e = jnp.exp2(s - m_new[:, 0:1])
    alpha = jnp.exp2(m_old - m_new)
    l_ref[pr] = alpha * l_ref[pr] + jnp.sum(e, axis=-1, keepdims=True)
    acc_ref[pr] = alpha * acc_ref[pr] + jnp.dot(e.astype(BF16), v2, preferred_element_type=F32)
    m_ref[pr] = m_new


def _pair_finish(o_ref, gate, branch, tq, l_ref, acc_ref):
    pieces, halves = [None] * H_ATT, [0] * H_ATT
    for pr in range(N_PAIR):
        l = l_ref[pr]
        o = acc_ref[pr] * jnp.where(l > 0.0, 1.0 / l, 0.0)
        for side in range(2):
            for g in range(GQA):
                hq = (2 * pr + side) * GQA + g
                r0 = (side * GQA + g) * tq
                pieces[hq] = o[r0:r0 + tq] * _gate_col(gate, hq, branch)
                halves[hq] = side
    _emit_heads(o_ref, pieces, halves)


def _causal_bias(tq):
    qi = lax.broadcasted_iota(jnp.int32, (tq, tq), 0)
    ki = lax.broadcasted_iota(jnp.int32, (tq, tq), 1)
    return jnp.where(ki <= qi, 0.0, MASKV)


def _pair_scratch(tq):
    rows = 2 * GQA * tq
    return [pltpu.VMEM((N_PAIR, rows, LANES), F32)] * 3


def _sel_paged_kernel(pt_ref, *refs, tq, n_steps, page, group):
    page_refs = refs[:group]
    qz_ref, new_ref, gate_ref, selb_ref, o_ref, m_ref, l_ref, acc_ref = refs[group:]
    p = pl.program_id(1)

    @pl.when(p == 0)
    def _():
        _pair_init(m_ref, l_ref, acc_ref)

    @pl.when(p < n_steps)
    def _():
        kv = jnp.concatenate([r[0] for r in page_refs], axis=0).astype(BF16)
        expand = _block_expand(p * (group * page), group * page)
        bias = [jnp.dot(selb_ref[0, h], expand, preferred_element_type=F32) for h in range(N_KV)]
        for pr in range(N_PAIR):
            _pair_update(pr, _pair_queries(qz_ref, pr, tq), kv, bias[2 * pr], bias[2 * pr + 1], m_ref, l_ref, acc_ref)

    @pl.when(p == n_steps)
    def _():
        kv = new_ref[...].astype(BF16)
        bias = _causal_bias(tq)
        for pr in range(N_PAIR):
            _pair_update(pr, _pair_queries(qz_ref, pr, tq), kv, bias, bias, m_ref, l_ref, acc_ref)
        _pair_finish(o_ref, _sigmoid(gate_ref[...]), 1, tq, l_ref, acc_ref)


def _sel_paged(qz, pool, page_table, new_rows, us, selb, tq):
    bsz, n_pages = page_table.shape
    page = pool.shape[1]
    group = PAGE_GROUP
    n_steps = n_pages // group
    page_spec = lambda k: pl.BlockSpec(
        (1, page, 2 * KV_DIM), lambda b, p, pt: (pt[b, jnp.minimum(p, n_steps - 1) * group + k], 0, 0))
    grid_spec = pltpu.PrefetchScalarGridSpec(
        num_scalar_prefetch=1,
        grid=(bsz, n_steps + 1),
        in_specs=[page_spec(k) for k in range(group)] + [
            pl.BlockSpec((N_KV, GQA, tq, LANES), lambda b, p, pt: (0, 0, b, 0)),
            pl.BlockSpec((tq, 2 * KV_DIM), lambda b, p, pt: (b, 0)),
            pl.BlockSpec((tq, LANES), lambda b, p, pt: (b, 0)),
            pl.BlockSpec((1, N_KV, tq, LANES), lambda b, p, pt: (b, 0, 0, 0))],
        out_specs=pl.BlockSpec((tq, D_MODEL), lambda b, p, pt: (b, 0)),
        scratch_shapes=_pair_scratch(tq),
    )
    return pl.pallas_call(
        functools.partial(_sel_paged_kernel, tq=tq, n_steps=n_steps, page=page, group=group),
        grid_spec=grid_spec,
        out_shape=jax.ShapeDtypeStruct((bsz * tq, D_MODEL), F32),
        compiler_params=_cparams(("parallel", "arbitrary")),
        name="sel_paged",
    )(page_table, *([pool] * group), qz, new_rows, us, selb)


def _win_cached_kernel(qz_ref, wc_ref, new_ref, gate_ref, o_ref, m_ref, l_ref, acc_ref, *, tq, wlen):
    _pair_init(m_ref, l_ref, acc_ref)
    qi = lax.broadcasted_iota(jnp.int32, (tq, wlen), 0)
    kj = lax.broadcasted_iota(jnp.int32, (tq, wlen), 1)
    bias_c = jnp.where((wlen - kj + qi) < WINDOW, 0.0, MASKV)
    bias_n = _causal_bias(tq)
    kv_c = wc_ref[0].astype(BF16)
    kv_n = new_ref[...].astype(BF16)
    for pr in range(N_PAIR):
        q = _pair_queries(qz_ref, pr, tq)
        _pair_update(pr, q, kv_c, bias_c, bias_c, m_ref, l_ref, acc_ref)
        _pair_update(pr, q, kv_n, bias_n, bias_n, m_ref, l_ref, acc_ref)
    _pair_finish(o_ref, _sigmoid(gate_ref[...]), 2, tq, l_ref, acc_ref)


def _win_cached(qz, wcache, new_rows, us, tq):
    bsz, wlen = wcache.shape[:2]
    return pl.pallas_call(
        functools.partial(_win_cached_kernel, tq=tq, wlen=wlen),
        grid=(bsz,),
        in_specs=[pl.BlockSpec((N_KV, GQA, tq, LANES), lambda b: (0, 0, b, 0)),
                  pl.BlockSpec((1, wlen, 2 * KV_DIM), lambda b: (b, 0, 0)),
                  pl.BlockSpec((tq, 2 * KV_DIM), lambda b: (b, 0)),
                  pl.BlockSpec((tq, LANES), lambda b: (b, 0))],
        out_specs=pl.BlockSpec((tq, D_MODEL), lambda b: (b, 0)),
        out_shape=jax.ShapeDtypeStruct((bsz * tq, D_MODEL), F32),
        scratch_shapes=_pair_scratch(tq),
        compiler_params=_cparams(("parallel",)),
        name="win_cached",
    )(qz, wcache, new_rows, us)


def _merge_kernel(x_ref, yn_ref, ysc_ref, oc_ref, os_ref, ow_ref, g1_ref, g2_ref, g3_ref,
                  wssd_ref, wsc_ref, wnsa_ref, wout_ref, o_ref):
    y_ssd = jnp.dot(yn_ref[...], wssd_ref[...], preferred_element_type=F32)
    y_sc = jnp.dot(ysc_ref[...], wsc_ref[...], preferred_element_type=F32)
    o = (oc_ref[...].astype(F32) + os_ref[...].astype(F32) + ow_ref[...].astype(F32)).astype(BF16)
    y_nsa = jnp.dot(o, wnsa_ref[...], preferred_element_type=F32)
    mix = _sigmoid(g1_ref[...]) * y_ssd + _sigmoid(g2_ref[...]) * y_sc + _sigmoid(g3_ref[...]) * y_nsa
    o_ref[...] = x_ref[...] + jnp.dot(mix.astype(BF16), wout_ref[...], preferred_element_type=F32)


def _merge(x, yn, ysc, oc, osel, ow, u, w, tm):
    m = x.shape[0]
    row = lambda c: pl.BlockSpec((tm, D_MODEL), lambda i: (i, c))
    wspec = pl.BlockSpec((D_MODEL, D_MODEL), lambda i: (0, 0))
    gcol = U_G // D_MODEL
    return pl.pallas_call(
        _merge_kernel,
        grid=(m // tm,),
        in_specs=[row(0)] * 6 + [row(gcol), row(gcol + 1), row(gcol + 2)] + [wspec] * 4,
        out_specs=row(0),
        out_shape=jax.ShapeDtypeStruct((m, D_MODEL), F32),
        compiler_params=_cparams(("parallel",)),
        name="merge",
    )(x, yn, ysc, oc, osel, ow, u, u, u, w["w_ssd_out"], w["w_sconv_out"], w["w_nsa_out"], w["w_out"])


def _mlp_kernel(x_ref, g_ref, wup_ref, wdn_ref, o_ref, h_ref, acc_ref, *, nf):
    j = pl.program_id(1)

    @pl.when(j == 0)
    def _():
        x = x_ref[...]
        ms = jnp.mean(x * x, axis=-1, keepdims=True)
        h_ref[...] = (x * lax.rsqrt(ms + EPS) * g_ref[...]).astype(h_ref.dtype)
        acc_ref[...] = jnp.zeros(acc_ref.shape, F32)

    up = jnp.dot(h_ref[...], wup_ref[...], preferred_element_type=F32)
    a = jnp.square(jnp.maximum(up, 0.0)).astype(BF16)
    acc_ref[...] += jnp.dot(a, wdn_ref[...], preferred_element_type=F32)

    @pl.when(j == nf - 1)
    def _():
        o_ref[...] = x_ref[...] + acc_ref[...]


def _mlp(x, g, wup, wdn, tm, tf):
    m = x.shape[0]
    nf = D_FF // tf
    return pl.pallas_call(
        functools.partial(_mlp_kernel, nf=nf),
        grid=(m // tm, nf),
        in_specs=[pl.BlockSpec((tm, D_MODEL), lambda i, j: (i, 0)),
                  pl.BlockSpec((1, D_MODEL), lambda i, j: (0, 0)),
                  pl.BlockSpec((D_MODEL, tf), lambda i, j: (0, j)),
                  pl.BlockSpec((tf, D_MODEL), lambda i, j: (j, 0))],
        out_specs=pl.BlockSpec((tm, D_MODEL), lambda i, j: (i, 0)),
        out_shape=jax.ShapeDtypeStruct((m, D_MODEL), F32),
        scratch_shapes=[pltpu.VMEM((tm, D_MODEL), BF16), pltpu.VMEM((tm, D_MODEL), F32)],
        compiler_params=_cparams(("parallel", "arbitrary")),
        name="mlp",
    )(x, g, wup, wdn)


def _pad_lanes(v, width=LANES):
    v = v.reshape(1, -1).astype(F32)
    return jnp.pad(v, ((0, 0), (0, width - v.shape[1])))


def _pad_rows(v, rows=SUBLANES, front=False):
    pad = rows - v.shape[-2]
    cfg = [(0, 0)] * (v.ndim - 2) + [((pad, 0) if front else (0, pad)), (0, 0)]
    return jnp.pad(v, cfg)


def _layer_weights(l, w_in, norm_mix, norm_mlp, ssd_conv_w, ssd_conv_b, ssd_dt_bias, ssd_a_log, ssd_d, ssd_norm,
                   w_ssd_out, sconv_w, w_sconv_out, cmp_pe, cmp_w1, cmp_w2, w_nsa_out, w_out, w_mlp_up, w_mlp_down):
    b = np.cumsum((0,) + IN_SIZES)
    wi = w_in[l]
    seg = lambda k: wi[:, b[k]:b[k + 1]]
    w_main = jnp.concatenate([seg(1), seg(7), seg(0), seg(3), seg(4), seg(5), seg(6), seg(9)], axis=1).astype(BF16)
    w_small = jnp.concatenate([seg(2), seg(8)], axis=1)
    w_small = jnp.pad(w_small, ((0, 0), (0, LANES - w_small.shape[1]))).astype(BF16)
    return dict(
        w_main=w_main, w_small=w_small,
        norm_mix=norm_mix[l].reshape(1, -1), norm_mlp=norm_mlp[l].reshape(1, -1),
        convw=_pad_rows(ssd_conv_w[l]), convb=ssd_conv_b[l].reshape(1, -1),
        dtb=_pad_lanes(ssd_dt_bias[l]), alog=_pad_lanes(ssd_a_log[l]), dvec=_pad_lanes(ssd_d[l]),
        ssdn=ssd_norm[l].reshape(1, -1), scw=_pad_rows(sconv_w[l]),
        w_ssd_out=w_ssd_out[l].astype(BF16), w_sconv_out=w_sconv_out[l].astype(BF16),
        w_nsa_out=w_nsa_out[l].astype(BF16), w_out=w_out[l].astype(BF16),
        cmp_pe=cmp_pe[l].reshape(2, 2, CMP_STRIDE, HEAD_DIM).transpose(0, 2, 1, 3).reshape(2, CMP_STRIDE, LANES),
        cmp_w1=cmp_w1[l].reshape(2, 2, CMP_STRIDE, HEAD_DIM, CMP_HID).transpose(0, 2, 1, 3, 4).reshape(
            2, CMP_STRIDE, LANES, CMP_HID).astype(BF16),
        cmp_w2=cmp_w2[l].astype(BF16),
        w_mlp_up=w_mlp_up[l].astype(BF16), w_mlp_down=w_mlp_down[l].astype(BF16),
    )


def _rope_tables(pos):
    half = HEAD_DIM // 2
    inv = ROPE_THETA ** (-jnp.arange(half, dtype=F32) / half)
    ang = pos.astype(F32)[:, None] * inv[None, :]
    cos, sin = jnp.cos(ang), jnp.sin(ang)
    cos_t = jnp.concatenate([cos, cos, cos, cos], axis=1)
    sin_t = jnp.concatenate([-sin, sin, -sin, sin], axis=1)
    return cos_t, sin_t


def _overlap(ncmp_pad, ncmp):
    c0 = np.arange(ncmp_pad)[:, None] * CMP_STRIDE
    s0 = np.arange(LANES)[None, :] * SEL_BLK
    ov = np.maximum(np.minimum(c0 + CMP_LEN, s0 + SEL_BLK) - np.maximum(c0, s0), 0).astype(np.float32) / CMP_LEN
    ov[ncmp:] = 0.0
    return jnp.asarray(ov)


def _tile(m, pref):
    t = pref
    while m % t:
        t //= 2
    return t


def _front(x2d, w, tm):
    u = _norm_matmul(x2d, w["norm_mix"], w["w_main"], tm, N_MAIN // 8)
    us = _norm_matmul(x2d, w["norm_mix"], w["w_small"], tm, LANES)
    return u, us


def _prompt_layer(x2d, w, bsz, t, tabs):
    m = bsz * t
    u, us = _front(x2d, w, _tile(m, 1024))
    cos_t, sin_t, ov = tabs
    qz, cmp_rows, sel_rows, win_rows, sel_b, win_b = _rope(u, cos_t, sin_t, _tile(m, 512), BF16)
    zeros = lambda *s: jnp.zeros(s, F32)
    yn, ysc, h_new, ch_last = _seqmix(u, us, zeros(bsz, SUBLANES, CONV_DIM), zeros(bsz, SUBLANES, D_MODEL),
                                      zeros(bsz, D_MODEL, SSD_STATE), w, bsz, t, SSD_CHUNK)
    n_pg = t // CMP_PAGE
    kc, vc = _compress(cmp_rows.reshape(bsz * n_pg, CMP_PAGE, 2 * KV_DIM),
                       jnp.arange(bsz * n_pg, dtype=jnp.int32).reshape(bsz, n_pg),
                       w["cmp_pe"], w["cmp_w1"], w["cmp_w2"])
    tq = _tile(t, 128)
    oc, selb = _cmp_attn(qz, kc, vc, us, ov, bsz, t, tq, 0, N_TOP, None)
    ta = _tile(t, 512)
    osel = _band_attn(qz, sel_b, us, selb, bsz, t, ta, ta, True)
    ow = _band_attn(qz, win_b, us, None, bsz, t, ta, ta, False)
    x1 = _merge(x2d, yn, ysc, oc, osel, ow, u, w, _tile(m, 256))
    x2 = _mlp(x1, w["norm_mlp"], w["w_mlp_up"], w["w_mlp_down"], _tile(m, 512), 1024)
    wl = min(WINDOW, t)
    u3 = u.reshape(bsz, t, N_MAIN)
    state = (cmp_rows.reshape(bsz, t, 2, N_KV, HEAD_DIM), sel_rows.reshape(bsz, t, 2, N_KV, HEAD_DIM),
             win_rows.reshape(bsz, t, 2 * KV_DIM)[:, t - wl:].reshape(bsz, wl, 2, N_KV, HEAD_DIM),
             h_new.reshape(bsz, H_SSD, SSD_HEAD_DIM, SSD_STATE),
             u3[:, t - (SSD_CONV - 1):, U_XBC:U_XBC + CONV_DIM],
             ch_last[:, SUBLANES - (SC_WIDTH - 1):])
    return x2, state


def _sample_layer(x2d, w, bsz, t, tabs, past):
    ssm0, conv0, sc0, cmp_pool, sel_pool, win_cache, page_table = past
    m = bsz * t
    u, us = _front(x2d, w, m)
    cos_t, sin_t, ov = tabs
    qz, cmp_rows, sel_rows, win_rows, _, _ = _rope(u, cos_t, sin_t, m, F32)
    L = SSD_CHUNK
    padr = lambda a: jnp.pad(a.reshape(bsz, t, a.shape[-1]), ((0, 0), (0, L - t), (0, 0))).reshape(bsz * L, a.shape[-1])
    yn, ysc, h_new, ch_last = _seqmix(padr(u), padr(us), _pad_rows(conv0, front=True), _pad_rows(sc0, front=True),
                                      ssm0.reshape(bsz, D_MODEL, SSD_STATE), w, bsz, L, t)
    yn = yn.reshape(bsz, L, D_MODEL)[:, :t].reshape(m, D_MODEL)
    ysc = ysc.reshape(bsz, L, D_MODEL)[:, :t].reshape(m, D_MODEL)
    kc, vc = _compress(cmp_pool, page_table, w["cmp_pe"], w["cmp_w1"], w["cmp_w2"])
    n_past_blk = page_table.shape[1] * cmp_pool.shape[1] // SEL_BLK
    oc, selb = _cmp_attn(qz, kc, vc, us, ov, bsz, t, t, page_table.shape[1] * cmp_pool.shape[1], N_TOP - 1,
                         n_past_blk)
    osel = _sel_paged(qz, sel_pool, page_table, sel_rows, us, selb, t)
    ow = _win_cached(qz, win_cache, win_rows, us, t)
    x1 = _merge(x2d, yn, ysc, oc, osel, ow, u, w, m)
    x2 = _mlp(x1, w["norm_mlp"], w["w_mlp_up"], w["w_mlp_down"], m, 1024)
    wl = win_cache.shape[1]
    win_all = jnp.concatenate([win_cache, win_rows.reshape(bsz, t, 2 * KV_DIM)], axis=1)
    keep = min(WINDOW, wl + t)
    u3 = u.reshape(bsz, t, N_MAIN)
    state = (cmp_rows.reshape(bsz, t, 2, N_KV, HEAD_DIM), sel_rows.reshape(bsz, t, 2, N_KV, HEAD_DIM),
             win_all[:, wl + t - keep:].reshape(bsz, keep, 2, N_KV, HEAD_DIM),
             h_new.reshape(bsz, H_SSD, SSD_HEAD_DIM, SSD_STATE),
             u3[:, t - (SSD_CONV - 1):, U_XBC:U_XBC + CONV_DIM],
             ch_last[:, SUBLANES - (SC_WIDTH - 1):])
    return x2, state


def kernel(x_prompt, x_sample, cache_cmp_kv, cache_sel_kv, cache_win_kv, state_ssm, state_ssd_conv, state_sconv,
           page_table, norm_mix, norm_mlp, norm_final, w_in, ssd_conv_w, ssd_conv_b, ssd_dt_bias, ssd_a_log, ssd_d,
           ssd_norm, w_ssd_out, sconv_w, w_sconv_out, cmp_pe, cmp_w1, cmp_w2, w_nsa_out, w_out, w_mlp_up,
           w_mlp_down):
    depth = w_in.shape[0]
    bp, tp, _ = x_prompt.shape
    bs, ts, _ = x_sample.shape
    n_pool, page = cache_cmp_kv.shape[1:3]
    past_len = page_table.shape[1] * page
    wl = cache_win_kv.shape[2]
    assert tp % (CMP_TILE * CMP_STRIDE) == 0 and tp // SEL_BLK <= LANES
    assert ts == SUBLANES and past_len % (CMP_TILE * CMP_STRIDE) == 0 and past_len // SEL_BLK <= LANES
    assert wl == WINDOW and past_len >= WINDOW
    assert page_table.shape[1] % PAGE_GROUP == 0 and (tp // CMP_PAGE) % PAGE_GROUP == 0
    assert (past_len + ts - CMP_LEN) // CMP_STRIDE + 1 == (past_len - CMP_LEN) // CMP_STRIDE + 1

    tabs_p = _rope_tables(jnp.tile(jnp.arange(tp), bp)) + (_overlap(tp // CMP_STRIDE, (tp - CMP_LEN) // CMP_STRIDE + 1),)
    tabs_s = _rope_tables(jnp.tile(past_len + jnp.arange(ts), bs)) + (
        _overlap(past_len // CMP_STRIDE, (past_len - CMP_LEN) // CMP_STRIDE + 1),)

    xp = x_prompt.reshape(bp * tp, D_MODEL)
    xs = x_sample.reshape(bs * ts, D_MODEL)
    p_new = [[] for _ in range(6)]
    s_new = [[] for _ in range(6)]
    for l in range(depth):
        w = _layer_weights(l, w_in, norm_mix, norm_mlp, ssd_conv_w, ssd_conv_b, ssd_dt_bias, ssd_a_log, ssd_d,
                           ssd_norm, w_ssd_out, sconv_w, w_sconv_out, cmp_pe, cmp_w1, cmp_w2, w_nsa_out, w_out,
                           w_mlp_up, w_mlp_down)
        xp, st_p = _prompt_layer(xp, w, bp, tp, tabs_p)
        past = (state_ssm[l], state_ssd_conv[l], state_sconv[l],
                cache_cmp_kv[l].reshape(n_pool, page, 2 * KV_DIM), cache_sel_kv[l].reshape(n_pool, page, 2 * KV_DIM),
                cache_win_kv[l].reshape(bs, wl, 2 * KV_DIM), page_table)
        xs, st_s = _sample_layer(xs, w, bs, ts, tabs_s, past)
        for i in range(6):
            p_new[i].append(st_p[i])
            s_new[i].append(st_s[i])
    gf = norm_final.reshape(1, -1)
    y_prompt = _final_norm(xp, gf, _tile(bp * tp, 1024)).reshape(bp, tp, D_MODEL)
    y_sample = _final_norm(xs, gf, bs * ts).reshape(bs, ts, D_MODEL)
    return (y_prompt, y_sample) + tuple(jnp.stack(a) for a in p_new) + tuple(jnp.stack(a) for a in s_new)
```

```python
import functools
import math

import numpy as np
import jax
import jax.numpy as jnp
from jax import lax
from jax.experimental import pallas as pl
from jax.experimental.pallas import tpu as pltpu

F32 = jnp.float32
BF16 = jnp.bfloat16

D_MODEL = 1024
SSD_HEAD_DIM = 64
H_SSD = D_MODEL // SSD_HEAD_DIM
SSD_STATE = 128
SSD_GROUPS = 2
SSD_CONV = 4
SSD_CHUNK = 128
CONV_DIM = D_MODEL + 2 * SSD_GROUPS * SSD_STATE
SC_WIDTH = 3
HEAD_DIM = 64
H_ATT = D_MODEL // HEAD_DIM
N_KV = 4
GQA = H_ATT // N_KV
KV_DIM = N_KV * HEAD_DIM
CMP_LEN = 32
CMP_STRIDE = 16
CMP_HID = 4 * HEAD_DIM
SEL_BLK = 64
N_TOP = 16
WINDOW = 512
D_FF = 4 * D_MODEL
ROPE_THETA = 10000.0
EPS = 1e-6
NEG_INF = -1e30
BIG = 1e9
TAKEN = -3e38
LOG2E = 1.4426950408889634
MASKV = -1e30
M_INIT = -1e29
IN_SIZES = (D_MODEL, CONV_DIM, H_SSD, D_MODEL, D_MODEL, D_MODEL, D_MODEL, 6 * KV_DIM, 3 * H_ATT, 3 * D_MODEL)

LANES = 128
SUBLANES = 8
VMEM_LIMIT = 56 * 1024 * 1024

U_XBC, U_KV, U_Z, U_SCB, U_SCC, U_SCH, U_Q, U_G = 0, 1536, 3072, 4096, 5120, 6144, 7168, 8192
N_MAIN = 11264
GATE_COL0 = H_SSD

_NT = (((1,), (1,)), ((), ()))
_TN = (((0,), (0,)), ((), ()))


def _cparams(sem):
    return pltpu.CompilerParams(dimension_semantics=sem, vmem_limit_bytes=VMEM_LIMIT)


def _silu(x):
    return x * (1.0 / (1.0 + jnp.exp(-x)))


def _sigmoid(x):
    return 1.0 / (1.0 + jnp.exp(-x))


def _norm_matmul_kernel(x_ref, g_ref, w_ref, o_ref, h_ref):
    @pl.when(pl.program_id(1) == 0)
    def _():
        x = x_ref[...]
        ms = jnp.mean(x * x, axis=-1, keepdims=True)
        h_ref[...] = (x * lax.rsqrt(ms + EPS) * g_ref[...]).astype(h_ref.dtype)

    o_ref[...] = jnp.dot(h_ref[...], w_ref[...], preferred_element_type=F32).astype(o_ref.dtype)


def _norm_matmul(x, g, w, tm, tn):
    m, d = x.shape
    n = w.shape[1]
    return pl.pallas_call(
        _norm_matmul_kernel,
        grid=(m // tm, n // tn),
        in_specs=[pl.BlockSpec((tm, d), lambda i, j: (i, 0)),
                  pl.BlockSpec((1, d), lambda i, j: (0, 0)),
                  pl.BlockSpec((d, tn), lambda i, j: (0, j))],
        out_specs=pl.BlockSpec((tm, tn), lambda i, j: (i, j)),
        out_shape=jax.ShapeDtypeStruct((m, n), F32),
        scratch_shapes=[pltpu.VMEM((tm, d), BF16)],
        compiler_params=_cparams(("parallel", "arbitrary")),
        name="norm_matmul",
    )(x, g, w)


def _final_norm_kernel(x_ref, g_ref, o_ref):
    x = x_ref[...]
    ms = jnp.mean(x * x, axis=-1, keepdims=True)
    o_ref[...] = x * lax.rsqrt(ms + EPS) * g_ref[...]


def _final_norm(x, g, tm):
    m, d = x.shape
    return pl.pallas_call(
        _final_norm_kernel,
        grid=(m // tm,),
        in_specs=[pl.BlockSpec((tm, d), lambda i: (i, 0)), pl.BlockSpec((1, d), lambda i: (0, 0))],
        out_specs=pl.BlockSpec((tm, d), lambda i: (i, 0)),
        out_shape=jax.ShapeDtypeStruct((m, d), F32),
        compiler_params=_cparams(("parallel",)),
        name="final_norm",
    )(x, g)


def _rot_half(x, first_half):
    return jnp.where(first_half, pltpu.roll(x, LANES - HEAD_DIM // 2, axis=1), pltpu.roll(x, HEAD_DIM // 2, axis=1))


def _rope_kernel(q_ref, kv_ref, cos_ref, sin_ref, qz_ref, *out_refs, scale, feature_major):
    row_refs, t_refs = out_refs[:3], out_refs[3:]
    cos = cos_ref[...]
    sin = sin_ref[...]
    lane = lax.broadcasted_iota(jnp.int32, cos.shape, 1)
    first_half = (lane % HEAD_DIM) < (HEAD_DIM // 2)
    left = lane < HEAD_DIM

    def rope(x):
        return x * cos + _rot_half(x, first_half) * sin

    for c in range(D_MODEL // LANES):
        r = rope(q_ref[:, c * LANES:(c + 1) * LANES]) * scale
        sw = pltpu.roll(r, HEAD_DIM, axis=1)
        h = (2 * c) // GQA
        g = (2 * c) % GQA
        if h % 2 == 0:
            qz_ref[h, g] = jnp.where(left, r, 0.0).astype(qz_ref.dtype)
            qz_ref[h, g + 1] = jnp.where(left, sw, 0.0).astype(qz_ref.dtype)
        else:
            qz_ref[h, g] = jnp.where(left, 0.0, sw).astype(qz_ref.dtype)
            qz_ref[h, g + 1] = jnp.where(left, 0.0, r).astype(qz_ref.dtype)
    for br in range(3):
        base = br * 2 * KV_DIM
        for c in range(2 * KV_DIM // LANES):
            x = kv_ref[:, base + c * LANES: base + (c + 1) * LANES]
            if c < KV_DIM // LANES:
                x = rope(x)
            row_refs[br][:, c * LANES:(c + 1) * LANES] = x.astype(row_refs[br].dtype)
            if feature_major:
                t_refs[br][0, c * LANES:(c + 1) * LANES, :] = x.T


def _rope(u, cos, sin, tm, bsz, t, prompt):
    m = u.shape[0]
    nt = t // tm if prompt else 1
    act = BF16 if prompt else F32
    kv_spec = pl.BlockSpec((tm, 2 * KV_DIM), lambda i: (i, 0))
    out_specs = [pl.BlockSpec((N_KV, GQA, tm, LANES), lambda i: (0, 0, i, 0)), kv_spec, kv_spec, kv_spec]
    out_shape = [jax.ShapeDtypeStruct((N_KV, GQA, m, LANES), act)] + [jax.ShapeDtypeStruct((m, 2 * KV_DIM), act)] * 3
    if prompt:
        out_specs += [pl.BlockSpec((1, 2 * KV_DIM, tm), lambda i: (i // nt, 0, i % nt))] * 3
        out_shape += [jax.ShapeDtypeStruct((bsz, 2 * KV_DIM, t), F32)] * 3
    return pl.pallas_call(
        functools.partial(_rope_kernel, scale=HEAD_DIM ** -0.5 * LOG2E, feature_major=prompt),
        grid=(m // tm,),
        in_specs=[pl.BlockSpec((tm, D_MODEL), lambda i: (i, U_Q // D_MODEL)),
                  pl.BlockSpec((tm, 6 * KV_DIM), lambda i: (i, U_KV // (6 * KV_DIM))),
                  pl.BlockSpec((tm, LANES), lambda i: (i, 0)),
                  pl.BlockSpec((tm, LANES), lambda i: (i, 0))],
        out_specs=out_specs,
        out_shape=out_shape,
        compiler_params=_cparams(("parallel",)),
        name="rope",
    )(u, u, cos, sin)


def _seqmix_kernel(xbc_ref, z_ref, scb_ref, scc_ref, sch_ref, dtr_ref, conv0_ref, sc0_ref, h0_ref,
                   convw_ref, convb_ref, dtb_ref, alog_ref, dvec_ref, ssdn_ref, scw_ref,
                   yn_ref, ysc_ref, hout_ref, chlast_ref,
                   ext_ref, chext_ref, h_ref, y_ref, *, chunk, valid):
    c = pl.program_id(1)
    L = chunk
    P = SSD_HEAD_DIM
    N = SSD_STATE

    @pl.when(c == 0)
    def _():
        ext_ref[0:SUBLANES, :] = conv0_ref[0]
        chext_ref[0:SUBLANES, :] = sc0_ref[0]
        h_ref[...] = h0_ref[0]

    def load(ref):
        x = ref[...]
        if valid < L:
            x = jnp.concatenate([x, jnp.zeros((L - valid, x.shape[1]), x.dtype)], axis=0)
        return x

    ext_ref[SUBLANES:SUBLANES + L, :] = load(xbc_ref)
    conv = jnp.broadcast_to(convb_ref[...], (L, CONV_DIM))
    for k in range(SSD_CONV):
        off = SUBLANES - (SSD_CONV - 1) + k
        conv = conv + convw_ref[k:k + 1, :] * ext_ref[off:off + L, :]
    ext_ref[0:SUBLANES, :] = ext_ref[L:L + SUBLANES, :]
    xbc = _silu(conv)

    ch = load(scc_ref) * load(sch_ref)
    chext_ref[SUBLANES:SUBLANES + L, :] = ch
    cv = jnp.zeros((L, D_MODEL), F32)
    for k in range(SC_WIDTH):
        off = SUBLANES - (SC_WIDTH - 1) + k
        cv = cv + scw_ref[k:k + 1, :] * chext_ref[off:off + L, :]
    ysc_ref[...] = (load(scb_ref) * cv)[:valid].astype(ysc_ref.dtype)
    chlast_ref[0] = chext_ref[valid:valid + SUBLANES, :]
    chext_ref[0:SUBLANES, :] = chext_ref[L:L + SUBLANES, :]

    x = load(dtr_ref) + dtb_ref[...]
    dt = jnp.maximum(x, 0.0) + jnp.log1p(jnp.exp(-jnp.abs(x)))
    row = lax.broadcasted_iota(jnp.int32, (L, LANES), 0)
    if valid < L:
        dt = jnp.where(row < valid, dt, 0.0)
    a = -jnp.exp(alog_ref[...])
    da = dt * a
    ri = lax.broadcasted_iota(jnp.int32, (L, L), 0)
    ci = lax.broadcasted_iota(jnp.int32, (L, L), 1)
    tril = ri >= ci
    acum = jnp.dot(tril.astype(F32), da, precision=lax.Precision.HIGHEST, preferred_element_type=F32)
    acum_t = acum.T
    a_last = acum[L - 1:L, :]

    for g in range(SSD_GROUPS):
        bg = xbc[:, D_MODEL + g * N: D_MODEL + (g + 1) * N]
        cg = xbc[:, D_MODEL + SSD_GROUPS * N + g * N: D_MODEL + SSD_GROUPS * N + (g + 1) * N]
        cb = lax.dot_general(cg, bg, _NT, preferred_element_type=F32)
        for r in range(H_SSD // SSD_GROUPS):
            h = g * (H_SSD // SSD_GROUPS) + r
            a_col = acum[:, h:h + 1]
            a_row = acum_t[h:h + 1, :]
            seg = jnp.where(tril, a_col - a_row, NEG_INF)
            wts = jnp.exp(seg) * cb
            xs_h = xbc[:, h * P:(h + 1) * P]
            xdt = xs_h * dt[:, h:h + 1]
            hprev = h_ref[h * P:(h + 1) * P, :]
            y_h = jnp.dot(wts, xdt, preferred_element_type=F32)
            y_in = lax.dot_general(cg, hprev, _NT, preferred_element_type=F32)
            y_h = y_h + y_in * jnp.exp(a_col)
            y_h = y_h + dvec_ref[:, h:h + 1] * xs_h
            y_ref[:, h * P:(h + 1) * P] = y_h
            al = a_last[:, h:h + 1]
            xw = xdt * jnp.exp(al - a_col)
            s_h = lax.dot_general(xw, bg, _TN, preferred_element_type=F32)
            h_ref[h * P:(h + 1) * P, :] = jnp.exp(al) * hprev + s_h

    y = y_ref[0:valid, :] * _silu(z_ref[...])
    ms = jnp.mean(y * y, axis=-1, keepdims=True)
    yn_ref[...] = (y * lax.rsqrt(ms + EPS) * ssdn_ref[...]).astype(yn_ref.dtype)
    hout_ref[0] = h_ref[...]


def _seqmix(u, us, conv0, sc0, h0, prm, bsz, t):
    L = SSD_CHUNK
    valid = min(t, L)
    nc = t // valid
    m = bsz * t
    act = _act_dtype(valid)
    row = lambda b, c: b * nc + c
    spec_u = lambda w, col: pl.BlockSpec((valid, w), lambda b, c: (row(b, c), col // w))
    full = lambda shape: pl.BlockSpec(shape, lambda b, c: (0,) * len(shape))
    perb = lambda shape: pl.BlockSpec((1,) + shape, lambda b, c: (b,) + (0,) * len(shape))
    return pl.pallas_call(
        functools.partial(_seqmix_kernel, chunk=L, valid=valid),
        grid=(bsz, nc),
        in_specs=[spec_u(CONV_DIM, U_XBC), spec_u(D_MODEL, U_Z), spec_u(D_MODEL, U_SCB), spec_u(D_MODEL, U_SCC),
                  spec_u(D_MODEL, U_SCH),
                  pl.BlockSpec((valid, LANES), lambda b, c: (row(b, c), 0)),
                  perb((SUBLANES, CONV_DIM)), perb((SUBLANES, D_MODEL)), perb((D_MODEL, SSD_STATE)),
                  full((SUBLANES, CONV_DIM)), full((1, CONV_DIM)), full((1, LANES)), full((1, LANES)),
                  full((1, LANES)), full((1, D_MODEL)), full((SUBLANES, D_MODEL))],
        out_specs=[pl.BlockSpec((valid, D_MODEL), lambda b, c: (row(b, c), 0)),
                   pl.BlockSpec((valid, D_MODEL), lambda b, c: (row(b, c), 0)),
                   perb((D_MODEL, SSD_STATE)), perb((SUBLANES, D_MODEL))],
        out_shape=[jax.ShapeDtypeStruct((m, D_MODEL), act), jax.ShapeDtypeStruct((m, D_MODEL), act),
                   jax.ShapeDtypeStruct((bsz, D_MODEL, SSD_STATE), F32),
                   jax.ShapeDtypeStruct((bsz, SUBLANES, D_MODEL), F32)],
        scratch_shapes=[pltpu.VMEM((L + SUBLANES, CONV_DIM), F32), pltpu.VMEM((L + SUBLANES, D_MODEL), F32),
                        pltpu.VMEM((D_MODEL, SSD_STATE), F32), pltpu.VMEM((L, D_MODEL), F32)],
        compiler_params=_cparams(("parallel", "arbitrary")),
        name="seqmix",
    )(u, u, u, u, u, us, conv0, sc0, h0, prm["convw"], prm["convb"], prm["dtb"], prm["alog"], prm["dvec"],
      prm["ssdn"], prm["scw"])


CMP_TILE = 128
CMP_PAGE = 128
PAGE_GROUP = 8


N_COMBO = 2 * N_KV


def _compress_fill(z_ref, page, g0, feature_major):
    R = page.shape[1] if feature_major else page.shape[0]
    lo = pl.ds(g0 + CMP_STRIDE, R)
    hi = pl.ds(g0, R)
    for pr in range(N_COMBO // 2):
        if feature_major:
            x2 = page[pr * LANES:(pr + 1) * LANES, :].T
        else:
            x2 = page[:, pr * LANES:(pr + 1) * LANES].astype(F32)
        sw = pltpu.roll(x2, HEAD_DIM, axis=1)
        z_ref[2 * pr, lo, 0:HEAD_DIM] = x2[:, 0:HEAD_DIM]
        z_ref[2 * pr, hi, HEAD_DIM:LANES] = sw[:, HEAD_DIM:LANES]
        z_ref[2 * pr + 1, lo, 0:HEAD_DIM] = sw[:, 0:HEAD_DIM]
        z_ref[2 * pr + 1, hi, HEAD_DIM:LANES] = x2[:, HEAD_DIM:LANES]


def _compress_tile(z_ref, row0, pe_ref, w1_ref, w2_ref, out_refs, out_row0):
    for c in range(N_COMBO):
        kv, h = divmod(c, N_KV)
        acc = jnp.zeros((CMP_TILE, CMP_HID), F32)
        for r in range(CMP_STRIDE):
            zr = z_ref[c, pl.ds(row0 + CMP_STRIDE + r, CMP_TILE, stride=CMP_STRIDE), :] + pe_ref[kv, r:r + 1, :]
            acc = acc + jnp.dot(zr.astype(BF16), w1_ref[kv, r], preferred_element_type=F32)
        o = jnp.dot(_silu(acc).astype(BF16), w2_ref[kv], preferred_element_type=F32)
        out_refs[kv][0, pl.ds(out_row0, CMP_TILE), pl.ds(h * HEAD_DIM, HEAD_DIM)] = o.astype(out_refs[kv].dtype)


def _compress_kernel(pt_ref, *refs, n_pages, page, group, feature_major):
    page_refs = refs[:group]
    pe_ref, w1_ref, w2_ref, kc_ref, vc_ref, z_ref = refs[group:]
    p = pl.program_id(1)
    for k in range(group):
        _compress_fill(z_ref, page_refs[k][0, 0], pl.multiple_of((p * group + k) * page, page), feature_major)

    @pl.when(p == n_pages // group - 1)
    def _():
        rows = n_pages * page
        z_ref[:, rows:rows + CMP_STRIDE, HEAD_DIM:LANES] = jnp.zeros((N_COMBO, CMP_STRIDE, LANES - HEAD_DIM), F32)

        def body(j, carry):
            _compress_tile(z_ref, pl.multiple_of(j * CMP_TILE * CMP_STRIDE, CMP_TILE * CMP_STRIDE),
                           pe_ref, w1_ref, w2_ref, (kc_ref, vc_ref), pl.multiple_of(j * CMP_TILE, CMP_TILE))
            return carry

        lax.fori_loop(0, rows // (CMP_TILE * CMP_STRIDE), body, 0)


def _page_shape(pool, feature_major):
    return (1, 1) + pool.shape[2:], pool.shape[3] if feature_major else pool.shape[2]


def _compress(pool, layer, page_table, pe2, w1p, w2, feature_major):
    bsz, n_pages = page_table.shape
    blk, page = _page_shape(pool, feature_major)
    rows = n_pages * page
    group = PAGE_GROUP
    out = jax.ShapeDtypeStruct((bsz, rows // CMP_STRIDE, KV_DIM), BF16)
    full = lambda shape: pl.BlockSpec(shape, lambda b, p, pt: (0,) * len(shape))
    page_spec = lambda k: pl.BlockSpec(blk, lambda b, p, pt: (layer, pt[b, p * group + k], 0, 0))
    grid_spec = pltpu.PrefetchScalarGridSpec(
        num_scalar_prefetch=1,
        grid=(bsz, n_pages // group),
        in_specs=[page_spec(k) for k in range(group)] + [full(pe2.shape), full(w1p.shape), full(w2.shape)],
        out_specs=[pl.BlockSpec((1, rows // CMP_STRIDE, KV_DIM), lambda b, p, pt: (b, 0, 0)),
                   pl.BlockSpec((1, rows // CMP_STRIDE, KV_DIM), lambda b, p, pt: (b, 0, 0))],
        scratch_shapes=[pltpu.VMEM((N_COMBO, rows + CMP_STRIDE, LANES), F32)],
    )
    return pl.pallas_call(
        functools.partial(_compress_kernel, n_pages=n_pages, page=page, group=group, feature_major=feature_major),
        grid_spec=grid_spec,
        out_shape=[out, out],
        compiler_params=_cparams(("parallel", "arbitrary")),
        name="compress",
    )(page_table, *([pool] * group), pe2, w1p, w2)


BF16_ROWS = 16


def _act_dtype(block_rows):
    return BF16 if block_rows % BF16_ROWS == 0 else F32


def _left_lanes(shape):
    return lax.broadcasted_iota(jnp.int32, shape, len(shape) - 1) < HEAD_DIM


def _own_half(left, h):
    return left if h % 2 == 0 else jnp.logical_not(left)


def _value_with_ones(v2, own):
    return jnp.where(own, v2, jnp.ones_like(v2))


def _normalise(acc, own):
    l = jnp.where(own, pltpu.roll(acc, HEAD_DIM, axis=1), acc)
    rinv = jnp.where(l > 0.0, 1.0 / l, 0.0)
    return acc * rinv, rinv


def _emit_heads(o_ref, pieces, halves):
    left = _left_lanes(pieces[0].shape)
    for c in range(H_ATT // 2):
        a, b = pieces[2 * c], pieces[2 * c + 1]
        a = a if halves[2 * c] == 0 else pltpu.roll(a, HEAD_DIM, axis=1)
        b = b if halves[2 * c + 1] == 1 else pltpu.roll(b, HEAD_DIM, axis=1)
        o_ref[:, c * LANES:(c + 1) * LANES] = jnp.where(left, a, b).astype(o_ref.dtype)


def _gate_col(gate, hq, branch):
    col = GATE_COL0 + hq * 3 + branch
    return gate[:, col:col + 1]


def _topk_lanes(vals, n_iter):
    lane_f = lax.broadcasted_iota(jnp.int32, vals.shape, 1).astype(F32)

    def body(_, carry):
        vals, sel = carry
        mx = jnp.max(vals, axis=-1, keepdims=True)
        first = jnp.min(jnp.where(vals == mx, lane_f, float(LANES)), axis=-1, keepdims=True)
        hit = lane_f == first
        return jnp.where(hit, TAKEN, vals), jnp.where(hit, 1.0, sel)

    return lax.fori_loop(0, n_iter, body, (vals, jnp.zeros(vals.shape, F32)))[1]


def _topk_rows(vals, n_iter):
    row_f = lax.broadcasted_iota(jnp.int32, vals.shape, 0).astype(F32)

    def body(_, carry):
        vals, sel = carry
        mx = jnp.max(vals, axis=0, keepdims=True)
        first = jnp.min(jnp.where(vals == mx, row_f, float(LANES)), axis=0, keepdims=True)
        hit = row_f == first
        return jnp.where(hit, TAKEN, vals), jnp.where(hit, 1.0, sel)

    return lax.fori_loop(0, n_iter, body, (vals, jnp.zeros(vals.shape, F32)))[1]


def _cmp_attn_kernel(qz_ref, kc_ref, vc_ref, gate_ref, ov_ref, o_ref, selb_ref, *, tq, pos0, n_iter, n_past_blk):
    i = pl.program_id(1)
    ncmp = kc_ref.shape[1]
    rows = GQA * tq
    qpos = pos0 + i * tq + lax.broadcasted_iota(jnp.int32, (tq, 1), 0)
    cend = lax.broadcasted_iota(jnp.int32, (1, ncmp), 1) * CMP_STRIDE + (CMP_LEN - 1)
    bias = jnp.where(cend <= qpos, 0.0, MASKV)
    gate = _sigmoid(gate_ref[...])
    left = _left_lanes((rows, LANES))
    transposed = tq % LANES == 0
    shape = (LANES, tq) if transposed else (tq, LANES)
    blk = lax.broadcasted_iota(jnp.int32, shape, 0 if transposed else 1)
    if n_past_blk is None:
        cur = (pos0 + i * tq + lax.broadcasted_iota(jnp.int32, shape, 1 if transposed else 0)) // SEL_BLK
        forced = (blk == 0) | (blk == cur) | (blk == cur - 1)
        allowed = blk <= cur
    else:
        forced = (blk == 0) | (blk == n_past_blk - 1)
        allowed = blk < n_past_blk
    pieces, halves = [None] * H_ATT, [0] * H_ATT
    for h in range(N_KV):
        pr = h // 2
        own = _own_half(left, h)
        k2 = kc_ref[0, :, pr * LANES:(pr + 1) * LANES]
        v2 = vc_ref[0, :, pr * LANES:(pr + 1) * LANES]
        q = qz_ref[h].reshape(rows, LANES).astype(BF16)
        s = lax.dot_general(q, k2, _NT, preferred_element_type=F32).reshape(GQA, tq, ncmp) + bias[None]
        s = s.reshape(rows, ncmp)
        m = jnp.maximum(jnp.max(s, axis=-1, keepdims=True), M_INIT)
        e = jnp.exp2(s - m)
        acc = jnp.dot(e.astype(BF16), _value_with_ones(v2, _own_half(_left_lanes(v2.shape), h)),
                      preferred_element_type=F32)
        o, rinv = _normalise(acc, own)
        for g in range(GQA):
            hq = h * GQA + g
            pieces[hq] = o[g * tq:(g + 1) * tq] * _gate_col(gate, hq, 0)
            halves[hq] = h % 2
        p = e * jnp.concatenate([rinv] * (ncmp // LANES), axis=1)
        psum = jnp.sum(p.reshape(GQA, tq, ncmp), axis=0)
        imp = jnp.dot(psum, ov_ref[...], precision=lax.Precision.HIGHEST, preferred_element_type=F32)
        if transposed:
            vals = jnp.where(forced, BIG, jnp.where(allowed, imp.T, NEG_INF))
            sel = _topk_rows(vals, n_iter).T
        else:
            vals = jnp.where(forced, BIG, jnp.where(allowed, imp, NEG_INF))
            sel = _topk_lanes(vals, n_iter)
        selb_ref[0, h] = jnp.where(sel > 0.5, 0.0, MASKV).astype(selb_ref.dtype)
    _emit_heads(o_ref, pieces, halves)


def _cmp_attn(qz, kc, vc, us, ov, bsz, t, tq, pos0, n_iter, n_past_blk):
    nq = t // tq
    ncmp = kc.shape[1]
    return pl.pallas_call(
        functools.partial(_cmp_attn_kernel, tq=tq, pos0=pos0, n_iter=n_iter, n_past_blk=n_past_blk),
        grid=(bsz, nq),
        in_specs=[pl.BlockSpec((N_KV, GQA, tq, LANES), lambda b, i: (0, 0, b * nq + i, 0)),
                  pl.BlockSpec((1, ncmp, KV_DIM), lambda b, i: (b, 0, 0)),
                  pl.BlockSpec((1, ncmp, KV_DIM), lambda b, i: (b, 0, 0)),
                  pl.BlockSpec((tq, LANES), lambda b, i: (b * nq + i, 0)),
                  pl.BlockSpec(ov.shape, lambda b, i: (0, 0))],
        out_specs=[pl.BlockSpec((tq, D_MODEL), lambda b, i: (b * nq + i, 0)),
                   pl.BlockSpec((1, N_KV, tq, LANES), lambda b, i: (b, 0, i, 0))],
        out_shape=[jax.ShapeDtypeStruct((bsz * t, D_MODEL), _act_dtype(tq)),
                   jax.ShapeDtypeStruct((bsz, N_KV, t, LANES), BF16)],
        compiler_params=_cparams(("parallel", "parallel")),
        name="cmp_attn",
    )(qz, kc, vc, us, ov)


def _block_expand(k0, tk):
    blk = lax.broadcasted_iota(jnp.int32, (LANES, tk), 0)
    col = lax.broadcasted_iota(jnp.int32, (LANES, tk), 1)
    return (blk == (k0 + col) // SEL_BLK).astype(BF16)


def _band_attn_kernel(qz_ref, kv_ref, gate_ref, *rest, tq, tk, nk, nback, branch, use_sel):
    if use_sel:
        selb_ref, o_ref, m_ref, acc_ref = rest
    else:
        o_ref, m_ref, acc_ref = rest
    i = pl.program_id(1)
    kk = pl.program_id(2)
    rows = GQA * tq

    @pl.when(kk == 0)
    def _():
        m_ref[...] = jnp.full(m_ref.shape, M_INIT, F32)
        acc_ref[...] = jnp.zeros(acc_ref.shape, F32)

    if use_sel:
        kt = kk
        live = kt * tk <= i * tq + tq - 1
    else:
        kt = i * (tq // tk) - nback + kk
        live = kt >= 0

    @pl.when(live)
    def _():
        k0 = kt * tk
        dlt = (i * tq + lax.broadcasted_iota(jnp.int32, (tq, tk), 0)) - (k0 + lax.broadcasted_iota(jnp.int32, (tq, tk), 1))
        if use_sel:
            pos_bias = jnp.where(dlt >= 0, 0.0, MASKV)
            expand = _block_expand(k0, tk)
        else:
            pos_bias = jnp.where((dlt >= 0) & (dlt < WINDOW), 0.0, MASKV)
        half_v = lax.broadcasted_iota(jnp.int32, (tk, LANES), 1) // HEAD_DIM

        def head(h, carry):
            pr = pl.multiple_of((h // 2) * LANES, LANES)
            own_v = half_v == h % 2
            k2 = kv_ref[:, pl.ds(pr, LANES)]
            vaug = _value_with_ones(kv_ref[:, pl.ds(KV_DIM + pr, LANES)], own_v)
            bias = pos_bias
            if use_sel:
                bias = bias + jnp.dot(selb_ref[0, h], expand, preferred_element_type=F32)
            q = qz_ref[h].reshape(rows, LANES)
            s = lax.dot_general(q, k2, _NT, preferred_element_type=F32).reshape(GQA, tq, tk) + bias[None]
            s = s.reshape(rows, tk)
            m_old = m_ref[h]
            m_new = jnp.maximum(m_old, jnp.max(s, axis=-1, keepdims=True))
            e = jnp.exp2(s - m_new[:, 0:1]).astype(BF16)
            alpha = jnp.exp2(m_old - m_new)
            acc_ref[h] = alpha * acc_ref[h] + jnp.dot(e, vaug, preferred_element_type=F32)
            m_ref[h] = m_new
            return carry

        lax.fori_loop(0, N_KV, head, 0)

    @pl.when(kk == nk - 1)
    def _():
        gate = _sigmoid(gate_ref[...])
        left = _left_lanes((rows, LANES))
        pieces, halves = [None] * H_ATT, [0] * H_ATT
        for h in range(N_KV):
            o, _ = _normalise(acc_ref[h], _own_half(left, h))
            for g in range(GQA):
                hq = h * GQA + g
                pieces[hq] = o[g * tq:(g + 1) * tq] * _gate_col(gate, hq, branch)
                halves[hq] = h % 2
        _emit_heads(o_ref, pieces, halves)


def _band_attn(qz, kvb, us, selb, bsz, t, tq, tk, use_sel):
    nq = t // tq
    nkt = t // tk
    if use_sel:
        nk, nback, branch = nkt, 0, 1
        kv_idx = lambda b, i, kk: (b * nkt + jnp.minimum(kk, (i * tq + tq - 1) // tk), 0)
    else:
        nback = -(-(WINDOW - 1) // tk)
        nk, branch = nback + tq // tk, 2
        kv_idx = lambda b, i, kk: (b * nkt + jnp.maximum(i * (tq // tk) - nback + kk, 0), 0)
    in_specs = [pl.BlockSpec((N_KV, GQA, tq, LANES), lambda b, i, kk: (0, 0, b * nq + i, 0)),
                pl.BlockSpec((tk, 2 * KV_DIM), kv_idx),
                pl.BlockSpec((tq, LANES), lambda b, i, kk: (b * nq + i, 0))]
    args = [qz, kvb, us]
    if use_sel:
        in_specs.append(pl.BlockSpec((1, N_KV, tq, LANES), lambda b, i, kk: (b, 0, i, 0)))
        args.append(selb)
    return pl.pallas_call(
        functools.partial(_band_attn_kernel, tq=tq, tk=tk, nk=nk, nback=nback, branch=branch, use_sel=use_sel),
        grid=(bsz, nq, nk),
        in_specs=in_specs,
        out_specs=pl.BlockSpec((tq, D_MODEL), lambda b, i, kk: (b * nq + i, 0)),
        out_shape=jax.ShapeDtypeStruct((bsz * t, D_MODEL), BF16),
        scratch_shapes=[pltpu.VMEM((N_KV, GQA * tq, LANES), F32), pltpu.VMEM((N_KV, GQA * tq, LANES), F32)],
        compiler_params=_cparams(("parallel", "parallel", "arbitrary")),
        name="sel_attn" if use_sel else "win_attn",
    )(*args)


N_PAIR = N_KV // 2


def _pair_init(m_ref, l_ref, acc_ref):
    m_ref[...] = jnp.full(m_ref.shape, M_INIT, F32)
    l_ref[...] = jnp.zeros(l_ref.shape, F32)
    acc_ref[...] = jnp.zeros(acc_ref.shape, F32)


def _pair_queries(qz_ref, pr, tq):
    return jnp.concatenate([qz_ref[2 * pr].reshape(GQA * tq, LANES), qz_ref[2 * pr + 1].reshape(GQA * tq, LANES)],
                           axis=0).astype(BF16)


def _pair_update(pr, q, kv, bias_a, bias_b, m_ref, l_ref, acc_ref, feature_major=False):
    bias = jnp.concatenate([bias_a] * GQA + [bias_b] * GQA, axis=0)
    if feature_major:
        k2 = kv[pr * LANES:(pr + 1) * LANES, :]
        v2 = kv[KV_DIM + pr * LANES: KV_DIM + (pr + 1) * LANES, :]
        s = jnp.dot(q, k2, preferred_element_type=F32) + bias
        pv_dims = _NT
    else:
        k2 = kv[:, pr * LANES:(pr + 1) * LANES]
        v2 = kv[:, KV_DIM + pr * LANES: KV_DIM + (pr + 1) * LANES]
        s = lax.dot_general(q, k2, _NT, preferred_element_type=F32) + bias
        pv_dims = (((1,), (0,)), ((), ()))
    m_old = m_ref[pr]
    m_new = jnp.maximum(m_old, jnp.max(s, axis=-1, keepdims=True))
    e = jnp.exp2(s - m_new[:, 0:1])
    alpha = jnp.exp2(m_old - m_new)
    l_ref[pr] = alpha * l_ref[pr] + jnp.sum(e, axis=-1, keepdims=True)
    acc_ref[pr] = alpha * acc_ref[pr] + lax.dot_general(e.astype(BF16), v2, pv_dims, preferred_element_type=F32)
    m_ref[pr] = m_new


def _pair_finish(o_ref, gate, branch, tq, l_ref, acc_ref):
    pieces, halves = [None] * H_ATT, [0] * H_ATT
    for pr in range(N_PAIR):
        l = l_ref[pr]
        o = acc_ref[pr] * jnp.where(l > 0.0, 1.0 / l, 0.0)
        for side in range(2):
            for g in range(GQA):
                hq = (2 * pr + side) * GQA + g
                r0 = (side * GQA + g) * tq
                pieces[hq] = o[r0:r0 + tq] * _gate_col(gate, hq, branch)
                halves[hq] = side
    _emit_heads(o_ref, pieces, halves)


def _causal_bias(tq):
    qi = lax.broadcasted_iota(jnp.int32, (tq, tq), 0)
    ki = lax.broadcasted_iota(jnp.int32, (tq, tq), 1)
    return jnp.where(ki <= qi, 0.0, MASKV)


def _pair_scratch(tq):
    rows = 2 * GQA * tq
    return [pltpu.VMEM((N_PAIR, rows, LANES), F32)] * 3


def _sel_paged_kernel(pt_ref, *refs, tq, n_steps, page, group):
    page_refs = refs[:group]
    qz_ref, new_ref, gate_ref, selb_ref, o_ref, m_ref, l_ref, acc_ref = refs[group:]
    p = pl.program_id(1)

    @pl.when(p == 0)
    def _():
        _pair_init(m_ref, l_ref, acc_ref)

    @pl.when(p < n_steps)
    def _():
        kv = jnp.concatenate([r[0, 0] for r in page_refs], axis=1).astype(BF16)
        expand = _block_expand(p * (group * page), group * page)
        bias = [jnp.dot(selb_ref[0, h], expand, preferred_element_type=F32) for h in range(N_KV)]
        for pr in range(N_PAIR):
            _pair_update(pr, _pair_queries(qz_ref, pr, tq), kv, bias[2 * pr], bias[2 * pr + 1], m_ref, l_ref, acc_ref,
                         feature_major=True)

    @pl.when(p == n_steps)
    def _():
        kv = new_ref[...].astype(BF16)
        bias = _causal_bias(tq)
        for pr in range(N_PAIR):
            _pair_update(pr, _pair_queries(qz_ref, pr, tq), kv, bias, bias, m_ref, l_ref, acc_ref)
        _pair_finish(o_ref, _sigmoid(gate_ref[...]), 1, tq, l_ref, acc_ref)


def _sel_paged(qz, pool, layer, page_table, new_rows, us, selb, tq):
    bsz, n_pages = page_table.shape
    blk, page = _page_shape(pool, True)
    group = PAGE_GROUP
    n_steps = n_pages // group
    page_spec = lambda k: pl.BlockSpec(
        blk, lambda b, p, pt: (layer, pt[b, jnp.minimum(p, n_steps - 1) * group + k], 0, 0))
    grid_spec = pltpu.PrefetchScalarGridSpec(
        num_scalar_prefetch=1,
        grid=(bsz, n_steps + 1),
        in_specs=[page_spec(k) for k in range(group)] + [
            pl.BlockSpec((N_KV, GQA, tq, LANES), lambda b, p, pt: (0, 0, b, 0)),
            pl.BlockSpec((tq, 2 * KV_DIM), lambda b, p, pt: (b, 0)),
            pl.BlockSpec((tq, LANES), lambda b, p, pt: (b, 0)),
            pl.BlockSpec((1, N_KV, tq, LANES), lambda b, p, pt: (b, 0, 0, 0))],
        out_specs=pl.BlockSpec((tq, D_MODEL), lambda b, p, pt: (b, 0)),
        scratch_shapes=_pair_scratch(tq),
    )
    return pl.pallas_call(
        functools.partial(_sel_paged_kernel, tq=tq, n_steps=n_steps, page=page, group=group),
        grid_spec=grid_spec,
        out_shape=jax.ShapeDtypeStruct((bsz * tq, D_MODEL), F32),
        compiler_params=_cparams(("parallel", "arbitrary")),
        name="sel_paged",
    )(page_table, *([pool] * group), qz, new_rows, us, selb)


def _win_cached_kernel(qz_ref, wc_ref, new_ref, gate_ref, o_ref, m_ref, l_ref, acc_ref, *, tq, wlen):
    _pair_init(m_ref, l_ref, acc_ref)
    qi = lax.broadcasted_iota(jnp.int32, (tq, wlen), 0)
    kj = lax.broadcasted_iota(jnp.int32, (tq, wlen), 1)
    bias_c = jnp.where((wlen - kj + qi) < WINDOW, 0.0, MASKV)
    bias_n = _causal_bias(tq)
    kv_c = wc_ref[0, 0].astype(BF16)
    kv_n = new_ref[...].astype(BF16)
    for pr in range(N_PAIR):
        q = _pair_queries(qz_ref, pr, tq)
        _pair_update(pr, q, kv_c, bias_c, bias_c, m_ref, l_ref, acc_ref, feature_major=True)
        _pair_update(pr, q, kv_n, bias_n, bias_n, m_ref, l_ref, acc_ref)
    _pair_finish(o_ref, _sigmoid(gate_ref[...]), 2, tq, l_ref, acc_ref)


def _win_cached(qz, wcache, layer, new_rows, us, tq):
    bsz, wlen = wcache.shape[1], wcache.shape[3]
    return pl.pallas_call(
        functools.partial(_win_cached_kernel, tq=tq, wlen=wlen),
        grid=(bsz,),
        in_specs=[pl.BlockSpec((N_KV, GQA, tq, LANES), lambda b: (0, 0, b, 0)),
                  pl.BlockSpec((1, 1, 2 * KV_DIM, wlen), lambda b: (layer, b, 0, 0)),
                  pl.BlockSpec((tq, 2 * KV_DIM), lambda b: (b, 0)),
                  pl.BlockSpec((tq, LANES), lambda b: (b, 0))],
        out_specs=pl.BlockSpec((tq, D_MODEL), lambda b: (b, 0)),
        out_shape=jax.ShapeDtypeStruct((bsz * tq, D_MODEL), F32),
        scratch_shapes=_pair_scratch(tq),
        compiler_params=_cparams(("parallel",)),
        name="win_cached",
    )(qz, wcache, new_rows, us)


def _merge_kernel(x_ref, yn_ref, ysc_ref, oc_ref, os_ref, ow_ref, g1_ref, g2_ref, g3_ref,
                  wssd_ref, wsc_ref, wnsa_ref, wout_ref, o_ref):
    y_ssd = jnp.dot(yn_ref[...].astype(BF16), wssd_ref[...], preferred_element_type=F32)
    y_sc = jnp.dot(ysc_ref[...].astype(BF16), wsc_ref[...], preferred_element_type=F32)
    o = (oc_ref[...].astype(F32) + os_ref[...].astype(F32) + ow_ref[...].astype(F32)).astype(BF16)
    y_nsa = jnp.dot(o, wnsa_ref[...], preferred_element_type=F32)
    mix = _sigmoid(g1_ref[...]) * y_ssd + _sigmoid(g2_ref[...]) * y_sc + _sigmoid(g3_ref[...]) * y_nsa
    o_ref[...] = x_ref[...] + jnp.dot(mix.astype(BF16), wout_ref[...], preferred_element_type=F32)


def _merge(x, yn, ysc, oc, osel, ow, u, w, tm):
    m = x.shape[0]
    row = lambda c: pl.BlockSpec((tm, D_MODEL), lambda i: (i, c))
    wspec = pl.BlockSpec((D_MODEL, D_MODEL), lambda i: (0, 0))
    gcol = U_G // D_MODEL
    return pl.pallas_call(
        _merge_kernel,
        grid=(m // tm,),
        in_specs=[row(0)] * 6 + [row(gcol), row(gcol + 1), row(gcol + 2)] + [wspec] * 4,
        out_specs=row(0),
        out_shape=jax.ShapeDtypeStruct((m, D_MODEL), F32),
        compiler_params=_cparams(("parallel",)),
        name="merge",
    )(x, yn, ysc, oc, osel, ow, u, u, u, w["w_ssd_out"], w["w_sconv_out"], w["w_nsa_out"], w["w_out"])


def _mlp_kernel(x_ref, g_ref, wup_ref, wdn_ref, o_ref, h_ref, acc_ref, *, nf):
    j = pl.program_id(1)

    @pl.when(j == 0)
    def _():
        x = x_ref[...]
        ms = jnp.mean(x * x, axis=-1, keepdims=True)
        h_ref[...] = (x * lax.rsqrt(ms + EPS) * g_ref[...]).astype(h_ref.dtype)
        acc_ref[...] = jnp.zeros(acc_ref.shape, F32)

    up = jnp.dot(h_ref[...], wup_ref[...], preferred_element_type=F32)
    a = jnp.square(jnp.maximum(up, 0.0)).astype(BF16)
    acc_ref[...] += jnp.dot(a, wdn_ref[...], preferred_element_type=F32)

    @pl.when(j == nf - 1)
    def _():
        o_ref[...] = x_ref[...] + acc_ref[...]


def _mlp(x, g, wup, wdn, tm, tf):
    m = x.shape[0]
    nf = D_FF // tf
    return pl.pallas_call(
        functools.partial(_mlp_kernel, nf=nf),
        grid=(m // tm, nf),
        in_specs=[pl.BlockSpec((tm, D_MODEL), lambda i, j: (i, 0)),
                  pl.BlockSpec((1, D_MODEL), lambda i, j: (0, 0)),
                  pl.BlockSpec((D_MODEL, tf), lambda i, j: (0, j)),
                  pl.BlockSpec((tf, D_MODEL), lambda i, j: (j, 0))],
        out_specs=pl.BlockSpec((tm, D_MODEL), lambda i, j: (i, 0)),
        out_shape=jax.ShapeDtypeStruct((m, D_MODEL), F32),
        scratch_shapes=[pltpu.VMEM((tm, D_MODEL), BF16), pltpu.VMEM((tm, D_MODEL), F32)],
        compiler_params=_cparams(("parallel", "arbitrary")),
        name="mlp",
    )(x, g, wup, wdn)


def _pad_lanes(v, width=LANES):
    v = v.reshape(1, -1).astype(F32)
    return jnp.pad(v, ((0, 0), (0, width - v.shape[1])))


def _pad_rows(v, rows=SUBLANES, front=False):
    pad = rows - v.shape[-2]
    cfg = [(0, 0)] * (v.ndim - 2) + [((pad, 0) if front else (0, pad)), (0, 0)]
    return jnp.pad(v, cfg)


def _layer_weights(l, w_in, norm_mix, norm_mlp, ssd_conv_w, ssd_conv_b, ssd_dt_bias, ssd_a_log, ssd_d, ssd_norm,
                   w_ssd_out, sconv_w, w_sconv_out, cmp_pe, cmp_w1, cmp_w2, w_nsa_out, w_out, w_mlp_up, w_mlp_down):
    b = np.cumsum((0,) + IN_SIZES)
    wi = w_in[l]
    seg = lambda k: wi[:, b[k]:b[k + 1]]
    w_main = jnp.concatenate([seg(1), seg(7), seg(0), seg(3), seg(4), seg(5), seg(6), seg(9)], axis=1).astype(BF16)
    w_small = jnp.concatenate([seg(2), seg(8)], axis=1)
    w_small = jnp.pad(w_small, ((0, 0), (0, LANES - w_small.shape[1]))).astype(BF16)
    return dict(
        w_main=w_main, w_small=w_small,
        norm_mix=norm_mix[l].reshape(1, -1), norm_mlp=norm_mlp[l].reshape(1, -1),
        convw=_pad_rows(ssd_conv_w[l]), convb=ssd_conv_b[l].reshape(1, -1),
        dtb=_pad_lanes(ssd_dt_bias[l]), alog=_pad_lanes(ssd_a_log[l]), dvec=_pad_lanes(ssd_d[l]),
        ssdn=ssd_norm[l].reshape(1, -1), scw=_pad_rows(sconv_w[l]),
        w_ssd_out=w_ssd_out[l].astype(BF16), w_sconv_out=w_sconv_out[l].astype(BF16),
        w_nsa_out=w_nsa_out[l].astype(BF16), w_out=w_out[l].astype(BF16),
        cmp_pe=cmp_pe[l].reshape(2, 2, CMP_STRIDE, HEAD_DIM).transpose(0, 2, 1, 3).reshape(2, CMP_STRIDE, LANES),
        cmp_w1=cmp_w1[l].reshape(2, 2, CMP_STRIDE, HEAD_DIM, CMP_HID).transpose(0, 2, 1, 3, 4).reshape(
            2, CMP_STRIDE, LANES, CMP_HID).astype(BF16),
        cmp_w2=cmp_w2[l].astype(BF16),
        w_mlp_up=w_mlp_up[l].astype(BF16), w_mlp_down=w_mlp_down[l].astype(BF16),
    )


def _rope_tables(pos):
    half = HEAD_DIM // 2
    inv = ROPE_THETA ** (-jnp.arange(half, dtype=F32) / half)
    ang = pos.astype(F32)[:, None] * inv[None, :]
    cos, sin = jnp.cos(ang), jnp.sin(ang)
    cos_t = jnp.concatenate([cos, cos, cos, cos], axis=1)
    sin_t = jnp.concatenate([-sin, sin, -sin, sin], axis=1)
    return cos_t, sin_t


def _overlap(ncmp_pad, ncmp):
    c0 = np.arange(ncmp_pad)[:, None] * CMP_STRIDE
    s0 = np.arange(LANES)[None, :] * SEL_BLK
    ov = np.maximum(np.minimum(c0 + CMP_LEN, s0 + SEL_BLK) - np.maximum(c0, s0), 0).astype(np.float32) / CMP_LEN
    ov[ncmp:] = 0.0
    return jnp.asarray(ov)


def _tile(m, pref):
    t = pref
    while m % t:
        t //= 2
    return t


def _front(x2d, w, tm):
    u = _norm_matmul(x2d, w["norm_mix"], w["w_main"], tm, N_MAIN // 8)
    us = _norm_matmul(x2d, w["norm_mix"], w["w_small"], tm, LANES)
    return u, us


def _prompt_layer(x2d, w, bsz, t, tabs):
    m = bsz * t
    u, us = _front(x2d, w, _tile(m, 1024))
    cos_t, sin_t, ov = tabs
    qz, cmp_b, sel_b, win_b, cmp_t, sel_t, win_t = _rope(u, cos_t, sin_t, _tile(t, 512), bsz, t, True)
    zeros = lambda *s: jnp.zeros(s, F32)
    yn, ysc, h_new, ch_last = _seqmix(u, us, zeros(bsz, SUBLANES, CONV_DIM), zeros(bsz, SUBLANES, D_MODEL),
                                      zeros(bsz, D_MODEL, SSD_STATE), w, bsz, t)
    n_pg = t // CMP_PAGE
    kc, vc = _compress(cmp_b.reshape(1, bsz * n_pg, CMP_PAGE, 2 * KV_DIM), 0,
                       jnp.arange(bsz * n_pg, dtype=jnp.int32).reshape(bsz, n_pg),
                       w["cmp_pe"], w["cmp_w1"], w["cmp_w2"], False)
    tq = _tile(t, 256)
    oc, selb = _cmp_attn(qz, kc, vc, us, ov, bsz, t, tq, 0, N_TOP, None)
    ta = _tile(t, 512)
    osel = _band_attn(qz, sel_b, us, selb, bsz, t, ta, ta, True)
    ow = _band_attn(qz, win_b, us, None, bsz, t, ta, ta, False)
    x1 = _merge(x2d, yn, ysc, oc, osel, ow, u, w, _tile(m, 256))
    x2 = _mlp(x1, w["norm_mlp"], w["w_mlp_up"], w["w_mlp_down"], _tile(m, 512), 1024)
    wl = min(WINDOW, t)
    u3 = u.reshape(bsz, t, N_MAIN)
    rows_major = lambda a: a.reshape(bsz, 2, N_KV, HEAD_DIM, a.shape[-1]).transpose(0, 4, 1, 2, 3)
    state = (rows_major(cmp_t), rows_major(sel_t), rows_major(win_t[:, :, t - wl:]),
             h_new.reshape(bsz, H_SSD, SSD_HEAD_DIM, SSD_STATE),
             u3[:, t - (SSD_CONV - 1):, U_XBC:U_XBC + CONV_DIM],
             ch_last[:, SUBLANES - (SC_WIDTH - 1):])
    return x2, state


def _feature_major(cache):
    nd = cache.ndim
    perm = tuple(range(nd - 4)) + (nd - 3, nd - 2, nd - 1, nd - 4)
    return jnp.transpose(cache, perm).reshape(cache.shape[:-4] + (2 * KV_DIM, cache.shape[-4]))


def _sample_layer(x2d, w, bsz, t, tabs, layer, past):
    ssm0, conv0, sc0, cmp_pool, sel_pool, win_cache, win_rows_old, page_table = past
    m = bsz * t
    u, us = _front(x2d, w, m)
    cos_t, sin_t, ov = tabs
    qz, cmp_rows, sel_rows, win_rows = _rope(u, cos_t, sin_t, m, bsz, t, False)
    yn, ysc, h_new, ch_last = _seqmix(u, us, _pad_rows(conv0, front=True), _pad_rows(sc0, front=True),
                                      ssm0.reshape(bsz, D_MODEL, SSD_STATE), w, bsz, t)
    kc, vc = _compress(cmp_pool, layer, page_table, w["cmp_pe"], w["cmp_w1"], w["cmp_w2"], True)
    past_len = page_table.shape[1] * cmp_pool.shape[3]
    oc, selb = _cmp_attn(qz, kc, vc, us, ov, bsz, t, t, past_len, N_TOP - 1, past_len // SEL_BLK)
    osel = _sel_paged(qz, sel_pool, layer, page_table, sel_rows, us, selb, t)
    ow = _win_cached(qz, win_cache, layer, win_rows, us, t)
    x1 = _merge(x2d, yn, ysc, oc, osel, ow, u, w, m)
    x2 = _mlp(x1, w["norm_mlp"], w["w_mlp_up"], w["w_mlp_down"], m, 1024)
    wl = win_rows_old.shape[1]
    win_all = jnp.concatenate([win_rows_old, win_rows.reshape(bsz, t, 2, N_KV, HEAD_DIM)], axis=1)
    keep = min(WINDOW, wl + t)
    u3 = u.reshape(bsz, t, N_MAIN)
    state = (cmp_rows.reshape(bsz, t, 2, N_KV, HEAD_DIM), sel_rows.reshape(bsz, t, 2, N_KV, HEAD_DIM),
             win_all[:, wl + t - keep:],
             h_new.reshape(bsz, H_SSD, SSD_HEAD_DIM, SSD_STATE),
             u3[:, t - (SSD_CONV - 1):, U_XBC:U_XBC + CONV_DIM],
             ch_last[:, SUBLANES - (SC_WIDTH - 1):])
    return x2, state


def kernel(x_prompt, x_sample, cache_cmp_kv, cache_sel_kv, cache_win_kv, state_ssm, state_ssd_conv, state_sconv,
           page_table, norm_mix, norm_mlp, norm_final, w_in, ssd_conv_w, ssd_conv_b, ssd_dt_bias, ssd_a_log, ssd_d,
           ssd_norm, w_ssd_out, sconv_w, w_sconv_out, cmp_pe, cmp_w1, cmp_w2, w_nsa_out, w_out, w_mlp_up,
           w_mlp_down):
    depth = w_in.shape[0]
    bp, tp, _ = x_prompt.shape
    bs, ts, _ = x_sample.shape
    n_pool, page = cache_cmp_kv.shape[1:3]
    past_len = page_table.shape[1] * page
    wl = cache_win_kv.shape[2]
    assert tp % (CMP_TILE * CMP_STRIDE) == 0 and tp // SEL_BLK <= LANES
    assert ts == SUBLANES and past_len % (CMP_TILE * CMP_STRIDE) == 0 and past_len // SEL_BLK <= LANES
    assert wl == WINDOW and past_len >= WINDOW
    assert page_table.shape[1] % PAGE_GROUP == 0 and (tp // CMP_PAGE) % PAGE_GROUP == 0
    assert (past_len + ts - CMP_LEN) // CMP_STRIDE + 1 == (past_len - CMP_LEN) // CMP_STRIDE + 1

    tabs_p = _rope_tables(jnp.tile(jnp.arange(tp), bp)) + (_overlap(tp // CMP_STRIDE, (tp - CMP_LEN) // CMP_STRIDE + 1),)
    tabs_s = _rope_tables(jnp.tile(past_len + jnp.arange(ts), bs)) + (
        _overlap(past_len // CMP_STRIDE, (past_len - CMP_LEN) // CMP_STRIDE + 1),)

    cmp_pool, sel_pool, win_cache = _feature_major(cache_cmp_kv), _feature_major(cache_sel_kv), _feature_major(cache_win_kv)
    xp = x_prompt.reshape(bp * tp, D_MODEL)
    xs = x_sample.reshape(bs * ts, D_MODEL)
    p_new = [[] for _ in range(6)]
    s_new = [[] for _ in range(6)]
    for l in range(depth):
        w = _layer_weights(l, w_in, norm_mix, norm_mlp, ssd_conv_w, ssd_conv_b, ssd_dt_bias, ssd_a_log, ssd_d,
                           ssd_norm, w_ssd_out, sconv_w, w_sconv_out, cmp_pe, cmp_w1, cmp_w2, w_nsa_out, w_out,
                           w_mlp_up, w_mlp_down)
        xp, st_p = _prompt_layer(xp, w, bp, tp, tabs_p)
        past = (state_ssm[l], state_ssd_conv[l], state_sconv[l], cmp_pool, sel_pool, win_cache, cache_win_kv[l],
                page_table)
        xs, st_s = _sample_layer(xs, w, bs, ts, tabs_s, l, past)
        for i in range(6):
            p_new[i].append(st_p[i])
            s_new[i].append(st_s[i])
    gf = norm_final.reshape(1, -1)
    y_prompt = _final_norm(xp, gf, _tile(bp * tp, 1024)).reshape(bp, tp, D_MODEL)
    y_sample = _final_norm(xs, gf, bs * ts).reshape(bs, ts, D_MODEL)
    return (y_prompt, y_sample) + tuple(jnp.stack(a) for a in p_new) + tuple(jnp.stack(a) for a in s_new)
```

```python
import functools
import math

import numpy as np
import jax
import jax.numpy as jnp
from jax import lax
from jax.experimental import pallas as pl
from jax.experimental.pallas import tpu as pltpu

F32 = jnp.float32
BF16 = jnp.bfloat16

D_MODEL = 1024
SSD_HEAD_DIM = 64
H_SSD = D_MODEL // SSD_HEAD_DIM
SSD_STATE = 128
SSD_GROUPS = 2
SSD_CONV = 4
SSD_CHUNK = 128
CONV_DIM = D_MODEL + 2 * SSD_GROUPS * SSD_STATE
SC_WIDTH = 3
HEAD_DIM = 64
H_ATT = D_MODEL // HEAD_DIM
N_KV = 4
GQA = H_ATT // N_KV
KV_DIM = N_KV * HEAD_DIM
CMP_LEN = 32
CMP_STRIDE = 16
CMP_HID = 4 * HEAD_DIM
SEL_BLK = 64
N_TOP = 16
WINDOW = 512
D_FF = 4 * D_MODEL
ROPE_THETA = 10000.0
EPS = 1e-6
NEG_INF = -1e30
BIG = 1e9
TAKEN = -3e38
LOG2E = 1.4426950408889634
MASKV = -1e30
M_INIT = -1e29
IN_SIZES = (D_MODEL, CONV_DIM, H_SSD, D_MODEL, D_MODEL, D_MODEL, D_MODEL, 6 * KV_DIM, 3 * H_ATT, 3 * D_MODEL)

LANES = 128
SUBLANES = 8
VMEM_LIMIT = 56 * 1024 * 1024

U_XBC, U_KV, U_Z, U_SCB, U_SCC, U_SCH, U_Q, U_G = 0, 1536, 3072, 4096, 5120, 6144, 7168, 8192
N_MAIN = 11264
GATE_COL0 = H_SSD

_NT = (((1,), (1,)), ((), ()))
_TN = (((0,), (0,)), ((), ()))


def _cparams(sem):
    return pltpu.CompilerParams(dimension_semantics=sem, vmem_limit_bytes=VMEM_LIMIT)


def _silu(x):
    return x * (1.0 / (1.0 + jnp.exp(-x)))


def _sigmoid(x):
    return 1.0 / (1.0 + jnp.exp(-x))


def _norm_matmul_kernel(x_ref, g_ref, w_ref, o_ref, h_ref):
    @pl.when(pl.program_id(1) == 0)
    def _():
        x = x_ref[...]
        ms = jnp.mean(x * x, axis=-1, keepdims=True)
        h_ref[...] = (x * lax.rsqrt(ms + EPS) * g_ref[...]).astype(h_ref.dtype)

    o_ref[...] = jnp.dot(h_ref[...], w_ref[...], preferred_element_type=F32).astype(o_ref.dtype)


def _norm_matmul(x, g, w, tm, tn):
    m, d = x.shape
    n = w.shape[1]
    return pl.pallas_call(
        _norm_matmul_kernel,
        grid=(m // tm, n // tn),
        in_specs=[pl.BlockSpec((tm, d), lambda i, j: (i, 0)),
                  pl.BlockSpec((1, d), lambda i, j: (0, 0)),
                  pl.BlockSpec((d, tn), lambda i, j: (0, j))],
        out_specs=pl.BlockSpec((tm, tn), lambda i, j: (i, j)),
        out_shape=jax.ShapeDtypeStruct((m, n), F32),
        scratch_shapes=[pltpu.VMEM((tm, d), BF16)],
        compiler_params=_cparams(("parallel", "arbitrary")),
        name="norm_matmul",
    )(x, g, w)


def _final_norm_kernel(x_ref, g_ref, o_ref):
    x = x_ref[...]
    ms = jnp.mean(x * x, axis=-1, keepdims=True)
    o_ref[...] = x * lax.rsqrt(ms + EPS) * g_ref[...]


def _final_norm(x, g, tm):
    m, d = x.shape
    return pl.pallas_call(
        _final_norm_kernel,
        grid=(m // tm,),
        in_specs=[pl.BlockSpec((tm, d), lambda i: (i, 0)), pl.BlockSpec((1, d), lambda i: (0, 0))],
        out_specs=pl.BlockSpec((tm, d), lambda i: (i, 0)),
        out_shape=jax.ShapeDtypeStruct((m, d), F32),
        compiler_params=_cparams(("parallel",)),
        name="final_norm",
    )(x, g)


def _rot_half(x, first_half):
    return jnp.where(first_half, pltpu.roll(x, LANES - HEAD_DIM // 2, axis=1), pltpu.roll(x, HEAD_DIM // 2, axis=1))


def _rope_kernel(q_ref, kv_ref, cos_ref, sin_ref, qz_ref, *out_refs, scale, feature_major):
    row_refs, t_refs = out_refs[:3], out_refs[3:]
    cos = cos_ref[...]
    sin = sin_ref[...]
    lane = lax.broadcasted_iota(jnp.int32, cos.shape, 1)
    first_half = (lane % HEAD_DIM) < (HEAD_DIM // 2)
    left = lane < HEAD_DIM

    def rope(x):
        return x * cos + _rot_half(x, first_half) * sin

    for c in range(D_MODEL // LANES):
        r = rope(q_ref[:, c * LANES:(c + 1) * LANES]) * scale
        sw = pltpu.roll(r, HEAD_DIM, axis=1)
        h = (2 * c) // GQA
        g = (2 * c) % GQA
        if h % 2 == 0:
            qz_ref[h, g] = jnp.where(left, r, 0.0).astype(qz_ref.dtype)
            qz_ref[h, g + 1] = jnp.where(left, sw, 0.0).astype(qz_ref.dtype)
        else:
            qz_ref[h, g] = jnp.where(left, 0.0, sw).astype(qz_ref.dtype)
            qz_ref[h, g + 1] = jnp.where(left, 0.0, r).astype(qz_ref.dtype)
    for br in range(3):
        base = br * 2 * KV_DIM
        for c in range(2 * KV_DIM // LANES):
            x = kv_ref[:, base + c * LANES: base + (c + 1) * LANES]
            if c < KV_DIM // LANES:
                x = rope(x)
            row_refs[br][:, c * LANES:(c + 1) * LANES] = x.astype(row_refs[br].dtype)
            if feature_major:
                t_refs[br][0, c * LANES:(c + 1) * LANES, :] = x.T


def _rope(u, cos, sin, tm, bsz, t, prompt):
    m = u.shape[0]
    nt = t // tm if prompt else 1
    act = BF16 if prompt else F32
    kv_spec = pl.BlockSpec((tm, 2 * KV_DIM), lambda i: (i, 0))
    out_specs = [pl.BlockSpec((N_KV, GQA, tm, LANES), lambda i: (0, 0, i, 0)), kv_spec, kv_spec, kv_spec]
    out_shape = [jax.ShapeDtypeStruct((N_KV, GQA, m, LANES), act)] + [jax.ShapeDtypeStruct((m, 2 * KV_DIM), act)] * 3
    if prompt:
        out_specs += [pl.BlockSpec((1, 2 * KV_DIM, tm), lambda i: (i // nt, 0, i % nt))] * 3
        out_shape += [jax.ShapeDtypeStruct((bsz, 2 * KV_DIM, t), F32)] * 3
    return pl.pallas_call(
        functools.partial(_rope_kernel, scale=HEAD_DIM ** -0.5 * LOG2E, feature_major=prompt),
        grid=(m // tm,),
        in_specs=[pl.BlockSpec((tm, D_MODEL), lambda i: (i, U_Q // D_MODEL)),
                  pl.BlockSpec((tm, 6 * KV_DIM), lambda i: (i, U_KV // (6 * KV_DIM))),
                  pl.BlockSpec((tm, LANES), lambda i: (i, 0)),
                  pl.BlockSpec((tm, LANES), lambda i: (i, 0))],
        out_specs=out_specs,
        out_shape=out_shape,
        compiler_params=_cparams(("parallel",)),
        name="rope",
    )(u, u, cos, sin)


def _seqmix_kernel(xbc_ref, z_ref, scb_ref, scc_ref, sch_ref, dtr_ref, conv0_ref, sc0_ref, h0_ref,
                   convw_ref, convb_ref, dtb_ref, alog_ref, dvec_ref, ssdn_ref, scw_ref,
                   yn_ref, ysc_ref, hout_ref, chlast_ref,
                   ext_ref, chext_ref, h_ref, y_ref, *, chunk, valid):
    c = pl.program_id(1)
    L = chunk
    P = SSD_HEAD_DIM
    N = SSD_STATE

    @pl.when(c == 0)
    def _():
        ext_ref[0:SUBLANES, :] = conv0_ref[0]
        chext_ref[0:SUBLANES, :] = sc0_ref[0]
        h_ref[...] = h0_ref[0]

    def load(ref):
        x = ref[...]
        if valid < L:
            x = jnp.concatenate([x, jnp.zeros((L - valid, x.shape[1]), x.dtype)], axis=0)
        return x

    ext_ref[SUBLANES:SUBLANES + L, :] = load(xbc_ref)
    conv = jnp.broadcast_to(convb_ref[...], (L, CONV_DIM))
    for k in range(SSD_CONV):
        off = SUBLANES - (SSD_CONV - 1) + k
        conv = conv + convw_ref[k:k + 1, :] * ext_ref[off:off + L, :]
    ext_ref[0:SUBLANES, :] = ext_ref[L:L + SUBLANES, :]
    xbc = _silu(conv)

    ch = load(scc_ref) * load(sch_ref)
    chext_ref[SUBLANES:SUBLANES + L, :] = ch
    cv = jnp.zeros((L, D_MODEL), F32)
    for k in range(SC_WIDTH):
        off = SUBLANES - (SC_WIDTH - 1) + k
        cv = cv + scw_ref[k:k + 1, :] * chext_ref[off:off + L, :]
    ysc_ref[...] = (load(scb_ref) * cv)[:valid].astype(ysc_ref.dtype)
    chlast_ref[0] = chext_ref[valid:valid + SUBLANES, :]
    chext_ref[0:SUBLANES, :] = chext_ref[L:L + SUBLANES, :]

    x = load(dtr_ref) + dtb_ref[...]
    dt = jnp.maximum(x, 0.0) + jnp.log1p(jnp.exp(-jnp.abs(x)))
    row = lax.broadcasted_iota(jnp.int32, (L, LANES), 0)
    if valid < L:
        dt = jnp.where(row < valid, dt, 0.0)
    a = -jnp.exp(alog_ref[...])
    da = dt * a
    ri = lax.broadcasted_iota(jnp.int32, (L, L), 0)
    ci = lax.broadcasted_iota(jnp.int32, (L, L), 1)
    tril = ri >= ci
    acum = jnp.dot(tril.astype(F32), da, precision=lax.Precision.HIGHEST, preferred_element_type=F32)
    acum_t = acum.T
    a_last = acum[L - 1:L, :]

    for g in range(SSD_GROUPS):
        bg = xbc[:, D_MODEL + g * N: D_MODEL + (g + 1) * N]
        cg = xbc[:, D_MODEL + SSD_GROUPS * N + g * N: D_MODEL + SSD_GROUPS * N + (g + 1) * N]
        cb = lax.dot_general(cg, bg, _NT, preferred_element_type=F32)
        for r in range(H_SSD // SSD_GROUPS):
            h = g * (H_SSD // SSD_GROUPS) + r
            a_col = acum[:, h:h + 1]
            a_row = acum_t[h:h + 1, :]
            seg = jnp.where(tril, a_col - a_row, NEG_INF)
            wts = jnp.exp(seg) * cb
            xs_h = xbc[:, h * P:(h + 1) * P]
            xdt = xs_h * dt[:, h:h + 1]
            hprev = h_ref[h * P:(h + 1) * P, :]
            y_h = jnp.dot(wts, xdt, preferred_element_type=F32)
            y_in = lax.dot_general(cg, hprev, _NT, preferred_element_type=F32)
            y_h = y_h + y_in * jnp.exp(a_col)
            y_h = y_h + dvec_ref[:, h:h + 1] * xs_h
            y_ref[:, h * P:(h + 1) * P] = y_h
            al = a_last[:, h:h + 1]
            xw = xdt * jnp.exp(al - a_col)
            s_h = lax.dot_general(xw, bg, _TN, preferred_element_type=F32)
            h_ref[h * P:(h + 1) * P, :] = jnp.exp(al) * hprev + s_h

    y = y_ref[0:valid, :] * _silu(z_ref[...])
    ms = jnp.mean(y * y, axis=-1, keepdims=True)
    yn_ref[...] = (y * lax.rsqrt(ms + EPS) * ssdn_ref[...]).astype(yn_ref.dtype)
    hout_ref[0] = h_ref[...]


def _seqmix(u, us, conv0, sc0, h0, prm, bsz, t):
    L = SSD_CHUNK
    valid = min(t, L)
    nc = t // valid
    m = bsz * t
    act = _act_dtype(valid)
    row = lambda b, c: b * nc + c
    spec_u = lambda w, col: pl.BlockSpec((valid, w), lambda b, c: (row(b, c), col // w))
    full = lambda shape: pl.BlockSpec(shape, lambda b, c: (0,) * len(shape))
    perb = lambda shape: pl.BlockSpec((1,) + shape, lambda b, c: (b,) + (0,) * len(shape))
    return pl.pallas_call(
        functools.partial(_seqmix_kernel, chunk=L, valid=valid),
        grid=(bsz, nc),
        in_specs=[spec_u(CONV_DIM, U_XBC), spec_u(D_MODEL, U_Z), spec_u(D_MODEL, U_SCB), spec_u(D_MODEL, U_SCC),
                  spec_u(D_MODEL, U_SCH),
                  pl.BlockSpec((valid, LANES), lambda b, c: (row(b, c), 0)),
                  perb((SUBLANES, CONV_DIM)), perb((SUBLANES, D_MODEL)), perb((D_MODEL, SSD_STATE)),
                  full((SUBLANES, CONV_DIM)), full((1, CONV_DIM)), full((1, LANES)), full((1, LANES)),
                  full((1, LANES)), full((1, D_MODEL)), full((SUBLANES, D_MODEL))],
        out_specs=[pl.BlockSpec((valid, D_MODEL), lambda b, c: (row(b, c), 0)),
                   pl.BlockSpec((valid, D_MODEL), lambda b, c: (row(b, c), 0)),
                   perb((D_MODEL, SSD_STATE)), perb((SUBLANES, D_MODEL))],
        out_shape=[jax.ShapeDtypeStruct((m, D_MODEL), act), jax.ShapeDtypeStruct((m, D_MODEL), act),
                   jax.ShapeDtypeStruct((bsz, D_MODEL, SSD_STATE), F32),
                   jax.ShapeDtypeStruct((bsz, SUBLANES, D_MODEL), F32)],
        scratch_shapes=[pltpu.VMEM((L + SUBLANES, CONV_DIM), F32), pltpu.VMEM((L + SUBLANES, D_MODEL), F32),
                        pltpu.VMEM((D_MODEL, SSD_STATE), F32), pltpu.VMEM((L, D_MODEL), F32)],
        compiler_params=_cparams(("parallel", "arbitrary")),
        name="seqmix",
    )(u, u, u, u, u, us, conv0, sc0, h0, prm["convw"], prm["convb"], prm["dtb"], prm["alog"], prm["dvec"],
      prm["ssdn"], prm["scw"])


CMP_TILE = 512
CMP_PAGE = 128
PAGE_GROUP = 8


N_COMBO = 2 * N_KV


def _compress_fill(z_ref, page, g0, feature_major):
    R = page.shape[1] if feature_major else page.shape[0]
    lo = pl.ds(g0 + CMP_STRIDE, R)
    hi = pl.ds(g0, R)
    for pr in range(N_COMBO // 2):
        if feature_major:
            x2 = page[pr * LANES:(pr + 1) * LANES, :].T
        else:
            x2 = page[:, pr * LANES:(pr + 1) * LANES].astype(F32)
        sw = pltpu.roll(x2, HEAD_DIM, axis=1)
        z_ref[2 * pr, lo, 0:HEAD_DIM] = x2[:, 0:HEAD_DIM]
        z_ref[2 * pr, hi, HEAD_DIM:LANES] = sw[:, HEAD_DIM:LANES]
        z_ref[2 * pr + 1, lo, 0:HEAD_DIM] = sw[:, 0:HEAD_DIM]
        z_ref[2 * pr + 1, hi, HEAD_DIM:LANES] = x2[:, HEAD_DIM:LANES]


def _compress_tile(z_ref, row0, pe_ref, w1_ref, w2_ref, out_refs, out_row0, tile):
    for c in range(N_COMBO):
        kv, h = divmod(c, N_KV)
        acc = jnp.zeros((tile, CMP_HID), F32)
        for r in range(CMP_STRIDE):
            zr = z_ref[c, pl.ds(row0 + CMP_STRIDE + r, tile, stride=CMP_STRIDE), :] + pe_ref[kv, r:r + 1, :]
            acc = acc + jnp.dot(zr.astype(BF16), w1_ref[kv, r], preferred_element_type=F32)
        o = jnp.dot(_silu(acc).astype(BF16), w2_ref[kv], preferred_element_type=F32)
        out_refs[kv][0, pl.ds(out_row0, tile), pl.ds(h * HEAD_DIM, HEAD_DIM)] = o.astype(out_refs[kv].dtype)


def _compress_kernel(pt_ref, *refs, n_pages, page, group, feature_major):
    page_refs = refs[:group]
    pe_ref, w1_ref, w2_ref, kc_ref, vc_ref, z_ref = refs[group:]
    p = pl.program_id(1)
    for k in range(group):
        _compress_fill(z_ref, page_refs[k][0, 0], pl.multiple_of((p * group + k) * page, page), feature_major)

    @pl.when(p == n_pages // group - 1)
    def _():
        rows = n_pages * page
        z_ref[:, rows:rows + CMP_STRIDE, HEAD_DIM:LANES] = jnp.zeros((N_COMBO, CMP_STRIDE, LANES - HEAD_DIM), F32)

        tile = min(CMP_TILE, rows // CMP_STRIDE)

        def body(j, carry):
            _compress_tile(z_ref, pl.multiple_of(j * tile * CMP_STRIDE, tile * CMP_STRIDE),
                           pe_ref, w1_ref, w2_ref, (kc_ref, vc_ref), pl.multiple_of(j * tile, tile), tile)
            return carry

        lax.fori_loop(0, rows // (tile * CMP_STRIDE), body, 0)


def _page_shape(pool, feature_major):
    return (1, 1) + pool.shape[2:], pool.shape[3] if feature_major else pool.shape[2]


def _compress(pool, layer, page_table, pe2, w1p, w2, feature_major):
    bsz, n_pages = page_table.shape
    blk, page = _page_shape(pool, feature_major)
    rows = n_pages * page
    group = PAGE_GROUP
    out = jax.ShapeDtypeStruct((bsz, rows // CMP_STRIDE, KV_DIM), BF16)
    full = lambda shape: pl.BlockSpec(shape, lambda b, p, pt: (0,) * len(shape))
    page_spec = lambda k: pl.BlockSpec(blk, lambda b, p, pt: (layer, pt[b, p * group + k], 0, 0))
    grid_spec = pltpu.PrefetchScalarGridSpec(
        num_scalar_prefetch=1,
        grid=(bsz, n_pages // group),
        in_specs=[page_spec(k) for k in range(group)] + [full(pe2.shape), full(w1p.shape), full(w2.shape)],
        out_specs=[pl.BlockSpec((1, rows // CMP_STRIDE, KV_DIM), lambda b, p, pt: (b, 0, 0)),
                   pl.BlockSpec((1, rows // CMP_STRIDE, KV_DIM), lambda b, p, pt: (b, 0, 0))],
        scratch_shapes=[pltpu.VMEM((N_COMBO, rows + CMP_STRIDE, LANES), F32)],
    )
    return pl.pallas_call(
        functools.partial(_compress_kernel, n_pages=n_pages, page=page, group=group, feature_major=feature_major),
        grid_spec=grid_spec,
        out_shape=[out, out],
        compiler_params=_cparams(("parallel", "arbitrary")),
        name="compress",
    )(page_table, *([pool] * group), pe2, w1p, w2)


BF16_ROWS = 16


def _act_dtype(block_rows):
    return BF16 if block_rows % BF16_ROWS == 0 else F32


def _left_lanes(shape):
    return lax.broadcasted_iota(jnp.int32, shape, len(shape) - 1) < HEAD_DIM


def _own_half(left, h):
    return left if h % 2 == 0 else jnp.logical_not(left)


def _value_with_ones(v2, own):
    return jnp.where(own, v2, jnp.ones_like(v2))


def _normalise(acc, own):
    l = jnp.where(own, pltpu.roll(acc, HEAD_DIM, axis=1), acc)
    rinv = jnp.where(l > 0.0, 1.0 / l, 0.0)
    return acc * rinv, rinv


def _emit_heads(o_ref, pieces, halves):
    left = _left_lanes(pieces[0].shape)
    for c in range(H_ATT // 2):
        a, b = pieces[2 * c], pieces[2 * c + 1]
        a = a if halves[2 * c] == 0 else pltpu.roll(a, HEAD_DIM, axis=1)
        b = b if halves[2 * c + 1] == 1 else pltpu.roll(b, HEAD_DIM, axis=1)
        o_ref[:, c * LANES:(c + 1) * LANES] = jnp.where(left, a, b).astype(o_ref.dtype)


def _gate_col(gate, hq, branch):
    col = GATE_COL0 + hq * 3 + branch
    return gate[:, col:col + 1]


def _topk_lanes(vals, n_iter):
    lane_f = lax.broadcasted_iota(jnp.int32, vals.shape, 1).astype(F32)

    def body(_, carry):
        vals, sel = carry
        mx = jnp.max(vals, axis=-1, keepdims=True)
        first = jnp.min(jnp.where(vals == mx, lane_f, float(LANES)), axis=-1, keepdims=True)
        hit = lane_f == first
        return jnp.where(hit, TAKEN, vals), jnp.where(hit, 1.0, sel)

    return lax.fori_loop(0, n_iter, body, (vals, jnp.zeros(vals.shape, F32)))[1]


def _topk_rows(vals, n_iter):
    row_f = lax.broadcasted_iota(jnp.int32, vals.shape, 0).astype(F32)

    def body(_, vals):
        mx = jnp.max(vals, axis=0, keepdims=True)
        first = jnp.min(jnp.where(vals == mx, row_f, float(LANES)), axis=0, keepdims=True)
        return jnp.where(row_f == first, TAKEN, vals)

    return jnp.where(lax.fori_loop(0, n_iter, body, vals) == TAKEN, 1.0, 0.0)


def _cmp_attn_kernel(qz_ref, kc_ref, vc_ref, gate_ref, ov_ref, o_ref, selb_ref, *, tq, pos0, n_iter, n_past_blk):
    i = pl.program_id(1)
    ncmp = kc_ref.shape[1]
    rows = GQA * tq
    qpos = pos0 + i * tq + lax.broadcasted_iota(jnp.int32, (tq, 1), 0)
    cend = lax.broadcasted_iota(jnp.int32, (1, ncmp), 1) * CMP_STRIDE + (CMP_LEN - 1)
    bias = jnp.where(cend <= qpos, 0.0, MASKV)
    gate = _sigmoid(gate_ref[...])
    left = _left_lanes((rows, LANES))
    transposed = tq % LANES == 0
    shape = (LANES, tq) if transposed else (tq, LANES)
    blk = lax.broadcasted_iota(jnp.int32, shape, 0 if transposed else 1)
    if n_past_blk is None:
        cur = (pos0 + i * tq + lax.broadcasted_iota(jnp.int32, shape, 1 if transposed else 0)) // SEL_BLK
        forced = (blk == 0) | (blk == cur) | (blk == cur - 1)
        allowed = blk <= cur
    else:
        forced = (blk == 0) | (blk == n_past_blk - 1)
        allowed = blk < n_past_blk
    pieces, halves = [None] * H_ATT, [0] * H_ATT

    def scores(h):
        pr = h // 2
        q = qz_ref[h].reshape(rows, LANES).astype(BF16)
        s = lax.dot_general(q, kc_ref[0, :, pr * LANES:(pr + 1) * LANES], _NT, preferred_element_type=F32)
        return (s.reshape(GQA, tq, ncmp) + bias[None]).reshape(rows, ncmp)

    vals = []
    s = scores(0)
    for h in range(N_KV):
        s_next = scores(h + 1) if h + 1 < N_KV else None
        pr = h // 2
        own = _own_half(left, h)
        v2 = vc_ref[0, :, pr * LANES:(pr + 1) * LANES]
        m = jnp.maximum(jnp.max(s, axis=-1, keepdims=True), M_INIT)
        e = jnp.exp2(s - m)
        acc = jnp.dot(e.astype(BF16), _value_with_ones(v2, _own_half(_left_lanes(v2.shape), h)),
                      preferred_element_type=F32)
        o, rinv = _normalise(acc, own)
        for g in range(GQA):
            hq = h * GQA + g
            pieces[hq] = o[g * tq:(g + 1) * tq] * _gate_col(gate, hq, 0)
            halves[hq] = h % 2
        p = e * jnp.concatenate([rinv] * (ncmp // LANES), axis=1)
        psum = jnp.sum(p.reshape(GQA, tq, ncmp), axis=0)
        imp = jnp.dot(psum, ov_ref[...], precision=lax.Precision.HIGHEST, preferred_element_type=F32)
        vals.append(jnp.where(forced, BIG, jnp.where(allowed, imp.T if transposed else imp, NEG_INF)))
        s = s_next
    _emit_heads(o_ref, pieces, halves)
    for h in range(N_KV):
        sel_h = _topk_rows(vals[h], n_iter).T if transposed else _topk_lanes(vals[h], n_iter)
        selb_ref[0, h] = jnp.where(sel_h > 0.5, 0.0, MASKV).astype(selb_ref.dtype)


def _cmp_attn(qz, kc, vc, us, ov, bsz, t, tq, pos0, n_iter, n_past_blk):
    nq = t // tq
    ncmp = kc.shape[1]
    return pl.pallas_call(
        functools.partial(_cmp_attn_kernel, tq=tq, pos0=pos0, n_iter=n_iter, n_past_blk=n_past_blk),
        grid=(bsz, nq),
        in_specs=[pl.BlockSpec((N_KV, GQA, tq, LANES), lambda b, i: (0, 0, b * nq + i, 0)),
                  pl.BlockSpec((1, ncmp, KV_DIM), lambda b, i: (b, 0, 0)),
                  pl.BlockSpec((1, ncmp, KV_DIM), lambda b, i: (b, 0, 0)),
                  pl.BlockSpec((tq, LANES), lambda b, i: (b * nq + i, 0)),
                  pl.BlockSpec(ov.shape, lambda b, i: (0, 0))],
        out_specs=[pl.BlockSpec((tq, D_MODEL), lambda b, i: (b * nq + i, 0)),
                   pl.BlockSpec((1, N_KV, tq, LANES), lambda b, i: (b, 0, i, 0))],
        out_shape=[jax.ShapeDtypeStruct((bsz * t, D_MODEL), _act_dtype(tq)),
                   jax.ShapeDtypeStruct((bsz, N_KV, t, LANES), BF16)],
        compiler_params=_cparams(("parallel", "parallel")),
        name="cmp_attn",
    )(qz, kc, vc, us, ov)


def _block_expand(k0, tk):
    blk = lax.broadcasted_iota(jnp.int32, (LANES, tk), 0)
    col = lax.broadcasted_iota(jnp.int32, (LANES, tk), 1)
    return (blk == (k0 + col) // SEL_BLK).astype(BF16)


def _band_attn_kernel(qz_ref, kv_ref, gate_ref, *rest, tq, tk, nk, nback, branch, use_sel):
    if use_sel:
        selb_ref, o_ref, m_ref, acc_ref = rest
    else:
        o_ref, m_ref, acc_ref = rest
    i = pl.program_id(1)
    kk = pl.program_id(2)
    rows = GQA * tq

    @pl.when(kk == 0)
    def _():
        m_ref[...] = jnp.full(m_ref.shape, M_INIT, F32)
        acc_ref[...] = jnp.zeros(acc_ref.shape, F32)

    if use_sel:
        kt = kk
        live = kt * tk <= i * tq + tq - 1
    else:
        kt = i * (tq // tk) - nback + kk
        live = kt >= 0

    @pl.when(live)
    def _():
        k0 = kt * tk
        dlt = (i * tq + lax.broadcasted_iota(jnp.int32, (tq, tk), 0)) - (k0 + lax.broadcasted_iota(jnp.int32, (tq, tk), 1))
        if use_sel:
            pos_bias = jnp.where(dlt >= 0, 0.0, MASKV)
            expand = _block_expand(k0, tk)
        else:
            pos_bias = jnp.where((dlt >= 0) & (dlt < WINDOW), 0.0, MASKV)
        left_v = _left_lanes((tk, LANES))

        def scores(h):
            pr = (h // 2) * LANES
            bias = pos_bias
            if use_sel:
                bias = bias + jnp.dot(selb_ref[0, h], expand, preferred_element_type=F32)
            q = qz_ref[h].reshape(rows, LANES)
            s = lax.dot_general(q, kv_ref[:, pr:pr + LANES], _NT, preferred_element_type=F32)
            return (s.reshape(GQA, tq, tk) + bias[None]).reshape(rows, tk)

        s = scores(0)
        for h in range(N_KV):
            s_next = scores(h + 1) if h + 1 < N_KV else None
            pr = (h // 2) * LANES
            vaug = _value_with_ones(kv_ref[:, KV_DIM + pr: KV_DIM + pr + LANES], _own_half(left_v, h))
            m_old = m_ref[h]
            m_new = jnp.maximum(m_old, jnp.max(s, axis=-1, keepdims=True))
            e = jnp.exp2(s - m_new[:, 0:1]).astype(BF16)
            alpha = jnp.exp2(m_old - m_new)
            acc_ref[h] = alpha * acc_ref[h] + jnp.dot(e, vaug, preferred_element_type=F32)
            m_ref[h] = m_new
            s = s_next

    @pl.when(kk == nk - 1)
    def _():
        gate = _sigmoid(gate_ref[...])
        left = _left_lanes((rows, LANES))
        pieces, halves = [None] * H_ATT, [0] * H_ATT
        for h in range(N_KV):
            o, _ = _normalise(acc_ref[h], _own_half(left, h))
            for g in range(GQA):
                hq = h * GQA + g
                pieces[hq] = o[g * tq:(g + 1) * tq] * _gate_col(gate, hq, branch)
                halves[hq] = h % 2
        _emit_heads(o_ref, pieces, halves)


def _band_attn(qz, kvb, us, selb, bsz, t, tq, tk, use_sel):
    nq = t // tq
    nkt = t // tk
    if use_sel:
        nk, nback, branch = nkt, 0, 1
        kv_idx = lambda b, i, kk: (b * nkt + jnp.minimum(kk, (i * tq + tq - 1) // tk), 0)
    else:
        nback = -(-(WINDOW - 1) // tk)
        nk, branch = nback + tq // tk, 2
        kv_idx = lambda b, i, kk: (b * nkt + jnp.maximum(i * (tq // tk) - nback + kk, 0), 0)
    in_specs = [pl.BlockSpec((N_KV, GQA, tq, LANES), lambda b, i, kk: (0, 0, b * nq + i, 0)),
                pl.BlockSpec((tk, 2 * KV_DIM), kv_idx),
                pl.BlockSpec((tq, LANES), lambda b, i, kk: (b * nq + i, 0))]
    args = [qz, kvb, us]
    if use_sel:
        in_specs.append(pl.BlockSpec((1, N_KV, tq, LANES), lambda b, i, kk: (b, 0, i, 0)))
        args.append(selb)
    return pl.pallas_call(
        functools.partial(_band_attn_kernel, tq=tq, tk=tk, nk=nk, nback=nback, branch=branch, use_sel=use_sel),
        grid=(bsz, nq, nk),
        in_specs=in_specs,
        out_specs=pl.BlockSpec((tq, D_MODEL), lambda b, i, kk: (b * nq + i, 0)),
        out_shape=jax.ShapeDtypeStruct((bsz * t, D_MODEL), BF16),
        scratch_shapes=[pltpu.VMEM((N_KV, GQA * tq, LANES), F32), pltpu.VMEM((N_KV, GQA * tq, LANES), F32)],
        compiler_params=_cparams(("parallel", "parallel", "arbitrary")),
        name="sel_attn" if use_sel else "win_attn",
    )(*args)


N_PAIR = N_KV // 2


def _pair_init(m_ref, l_ref, acc_ref):
    m_ref[...] = jnp.full(m_ref.shape, M_INIT, F32)
    l_ref[...] = jnp.zeros(l_ref.shape, F32)
    acc_ref[...] = jnp.zeros(acc_ref.shape, F32)


def _pair_queries(qz_ref, pr, tq):
    return jnp.concatenate([qz_ref[2 * pr].reshape(GQA * tq, LANES), qz_ref[2 * pr + 1].reshape(GQA * tq, LANES)],
                           axis=0).astype(BF16)


def _pair_update(pr, q, kv, bias_a, bias_b, m_ref, l_ref, acc_ref, feature_major=False):
    bias = jnp.concatenate([bias_a] * GQA + [bias_b] * GQA, axis=0)
    if feature_major:
        k2 = kv[pr * LANES:(pr + 1) * LANES, :]
        v2 = kv[KV_DIM + pr * LANES: KV_DIM + (pr + 1) * LANES, :]
        s = jnp.dot(q, k2, preferred_element_type=F32) + bias
        pv_dims = _NT
    else:
        k2 = kv[:, pr * LANES:(pr + 1) * LANES]
        v2 = kv[:, KV_DIM + pr * LANES: KV_DIM + (pr + 1) * LANES]
        s = lax.dot_general(q, k2, _NT, preferred_element_type=F32) + bias
        pv_dims = (((1,), (0,)), ((), ()))
    m_old = m_ref[pr]
    m_new = jnp.maximum(m_old, jnp.max(s, axis=-1, keepdims=True))
    e = jnp.exp2(s - m_new[:, 0:1])
    alpha = jnp.exp2(m_old - m_new)
    l_ref[pr] = alpha * l_ref[pr] + jnp.sum(e, axis=-1, keepdims=True)
    acc_ref[pr] = alpha * acc_ref[pr] + lax.dot_general(e.astype(BF16), v2, pv_dims, preferred_element_type=F32)
    m_ref[pr] = m_new


def _pair_finish(o_ref, gate, branch, tq, l_ref, acc_ref):
    pieces, halves = [None] * H_ATT, [0] * H_ATT
    for pr in range(N_PAIR):
        l = l_ref[pr]
        o = acc_ref[pr] * jnp.where(l > 0.0, 1.0 / l, 0.0)
        for side in range(2):
            for g in range(GQA):
                hq = (2 * pr + side) * GQA + g
                r0 = (side * GQA + g) * tq
                pieces[hq] = o[r0:r0 + tq] * _gate_col(gate, hq, branch)
                halves[hq] = side
    _emit_heads(o_ref, pieces, halves)


def _causal_bias(tq):
    qi = lax.broadcasted_iota(jnp.int32, (tq, tq), 0)
    ki = lax.broadcasted_iota(jnp.int32, (tq, tq), 1)
    return jnp.where(ki <= qi, 0.0, MASKV)


def _pair_scratch(tq):
    rows = 2 * GQA * tq
    return [pltpu.VMEM((N_PAIR, rows, LANES), F32)] * 3


def _sel_paged_kernel(pt_ref, *refs, tq, n_steps, page, group):
    page_refs = refs[:group]
    qz_ref, new_ref, gate_ref, selb_ref, o_ref, m_ref, l_ref, acc_ref = refs[group:]
    p = pl.program_id(1)

    @pl.when(p == 0)
    def _():
        _pair_init(m_ref, l_ref, acc_ref)

    @pl.when(p < n_steps)
    def _():
        kv = jnp.concatenate([r[0, 0] for r in page_refs], axis=1).astype(BF16)
        expand = _block_expand(p * (group * page), group * page)
        bias = [jnp.dot(selb_ref[0, h], expand, preferred_element_type=F32) for h in range(N_KV)]
        for pr in range(N_PAIR):
            _pair_update(pr, _pair_queries(qz_ref, pr, tq), kv, bias[2 * pr], bias[2 * pr + 1], m_ref, l_ref, acc_ref,
                         feature_major=True)

    @pl.when(p == n_steps)
    def _():
        kv = new_ref[...].astype(BF16)
        bias = _causal_bias(tq)
        for pr in range(N_PAIR):
            _pair_update(pr, _pair_queries(qz_ref, pr, tq), kv, bias, bias, m_ref, l_ref, acc_ref)
        _pair_finish(o_ref, _sigmoid(gate_ref[...]), 1, tq, l_ref, acc_ref)


def _sel_paged(qz, pool, layer, page_table, new_rows, us, selb, tq):
    bsz, n_pages = page_table.shape
    blk, page = _page_shape(pool, True)
    group = PAGE_GROUP
    n_steps = n_pages // group
    page_spec = lambda k: pl.BlockSpec(
        blk, lambda b, p, pt: (layer, pt[b, jnp.minimum(p, n_steps - 1) * group + k], 0, 0))
    grid_spec = pltpu.PrefetchScalarGridSpec(
        num_scalar_prefetch=1,
        grid=(bsz, n_steps + 1),
        in_specs=[page_spec(k) for k in range(group)] + [
            pl.BlockSpec((N_KV, GQA, tq, LANES), lambda b, p, pt: (0, 0, b, 0)),
            pl.BlockSpec((tq, 2 * KV_DIM), lambda b, p, pt: (b, 0)),
            pl.BlockSpec((tq, LANES), lambda b, p, pt: (b, 0)),
            pl.BlockSpec((1, N_KV, tq, LANES), lambda b, p, pt: (b, 0, 0, 0))],
        out_specs=pl.BlockSpec((tq, D_MODEL), lambda b, p, pt: (b, 0)),
        scratch_shapes=_pair_scratch(tq),
    )
    return pl.pallas_call(
        functools.partial(_sel_paged_kernel, tq=tq, n_steps=n_steps, page=page, group=group),
        grid_spec=grid_spec,
        out_shape=jax.ShapeDtypeStruct((bsz * tq, D_MODEL), F32),
        compiler_params=_cparams(("parallel", "arbitrary")),
        name="sel_paged",
    )(page_table, *([pool] * group), qz, new_rows, us, selb)


def _win_cached_kernel(qz_ref, wc_ref, new_ref, gate_ref, o_ref, m_ref, l_ref, acc_ref, *, tq, wlen):
    _pair_init(m_ref, l_ref, acc_ref)
    qi = lax.broadcasted_iota(jnp.int32, (tq, wlen), 0)
    kj = lax.broadcasted_iota(jnp.int32, (tq, wlen), 1)
    bias_c = jnp.where((wlen - kj + qi) < WINDOW, 0.0, MASKV)
    bias_n = _causal_bias(tq)
    kv_c = wc_ref[0, 0].astype(BF16)
    kv_n = new_ref[...].astype(BF16)
    for pr in range(N_PAIR):
        q = _pair_queries(qz_ref, pr, tq)
        _pair_update(pr, q, kv_c, bias_c, bias_c, m_ref, l_ref, acc_ref, feature_major=True)
        _pair_update(pr, q, kv_n, bias_n, bias_n, m_ref, l_ref, acc_ref)
    _pair_finish(o_ref, _sigmoid(gate_ref[...]), 2, tq, l_ref, acc_ref)


def _win_cached(qz, wcache, layer, new_rows, us, tq):
    bsz, wlen = wcache.shape[1], wcache.shape[3]
    return pl.pallas_call(
        functools.partial(_win_cached_kernel, tq=tq, wlen=wlen),
        grid=(bsz,),
        in_specs=[pl.BlockSpec((N_KV, GQA, tq, LANES), lambda b: (0, 0, b, 0)),
                  pl.BlockSpec((1, 1, 2 * KV_DIM, wlen), lambda b: (layer, b, 0, 0)),
                  pl.BlockSpec((tq, 2 * KV_DIM), lambda b: (b, 0)),
                  pl.BlockSpec((tq, LANES), lambda b: (b, 0))],
        out_specs=pl.BlockSpec((tq, D_MODEL), lambda b: (b, 0)),
        out_shape=jax.ShapeDtypeStruct((bsz * tq, D_MODEL), F32),
        scratch_shapes=_pair_scratch(tq),
        compiler_params=_cparams(("parallel",)),
        name="win_cached",
    )(qz, wcache, new_rows, us)


def _merge_kernel(x_ref, yn_ref, ysc_ref, oc_ref, os_ref, ow_ref, g1_ref, g2_ref, g3_ref,
                  wssd_ref, wsc_ref, wnsa_ref, wout_ref, o_ref):
    y_ssd = jnp.dot(yn_ref[...].astype(BF16), wssd_ref[...], preferred_element_type=F32)
    y_sc = jnp.dot(ysc_ref[...].astype(BF16), wsc_ref[...], preferred_element_type=F32)
    o = (oc_ref[...].astype(F32) + os_ref[...].astype(F32) + ow_ref[...].astype(F32)).astype(BF16)
    y_nsa = jnp.dot(o, wnsa_ref[...], preferred_element_type=F32)
    mix = _sigmoid(g1_ref[...]) * y_ssd + _sigmoid(g2_ref[...]) * y_sc + _sigmoid(g3_ref[...]) * y_nsa
    o_ref[...] = x_ref[...] + jnp.dot(mix.astype(BF16), wout_ref[...], preferred_element_type=F32)


def _merge(x, yn, ysc, oc, osel, ow, u, w, tm):
    m = x.shape[0]
    row = lambda c: pl.BlockSpec((tm, D_MODEL), lambda i: (i, c))
    wspec = pl.BlockSpec((D_MODEL, D_MODEL), lambda i: (0, 0))
    gcol = U_G // D_MODEL
    return pl.pallas_call(
        _merge_kernel,
        grid=(m // tm,),
        in_specs=[row(0)] * 6 + [row(gcol), row(gcol + 1), row(gcol + 2)] + [wspec] * 4,
        out_specs=row(0),
        out_shape=jax.ShapeDtypeStruct((m, D_MODEL), F32),
        compiler_params=_cparams(("parallel",)),
        name="merge",
    )(x, yn, ysc, oc, osel, ow, u, u, u, w["w_ssd_out"], w["w_sconv_out"], w["w_nsa_out"], w["w_out"])


def _mlp_kernel(x_ref, g_ref, wup_ref, wdn_ref, o_ref, h_ref, acc_ref, *, nf):
    j = pl.program_id(1)

    @pl.when(j == 0)
    def _():
        x = x_ref[...]
        ms = jnp.mean(x * x, axis=-1, keepdims=True)
        h_ref[...] = (x * lax.rsqrt(ms + EPS) * g_ref[...]).astype(h_ref.dtype)
        acc_ref[...] = jnp.zeros(acc_ref.shape, F32)

    up = jnp.dot(h_ref[...], wup_ref[...], preferred_element_type=F32)
    a = jnp.square(jnp.maximum(up, 0.0)).astype(BF16)
    acc_ref[...] += jnp.dot(a, wdn_ref[...], preferred_element_type=F32)

    @pl.when(j == nf - 1)
    def _():
        o_ref[...] = x_ref[...] + acc_ref[...]


def _mlp(x, g, wup, wdn, tm, tf):
    m = x.shape[0]
    nf = D_FF // tf
    return pl.pallas_call(
        functools.partial(_mlp_kernel, nf=nf),
        grid=(m // tm, nf),
        in_specs=[pl.BlockSpec((tm, D_MODEL), lambda i, j: (i, 0)),
                  pl.BlockSpec((1, D_MODEL), lambda i, j: (0, 0)),
                  pl.BlockSpec((D_MODEL, tf), lambda i, j: (0, j)),
                  pl.BlockSpec((tf, D_MODEL), lambda i, j: (j, 0))],
        out_specs=pl.BlockSpec((tm, D_MODEL), lambda i, j: (i, 0)),
        out_shape=jax.ShapeDtypeStruct((m, D_MODEL), F32),
        scratch_shapes=[pltpu.VMEM((tm, D_MODEL), BF16), pltpu.VMEM((tm, D_MODEL), F32)],
        compiler_params=_cparams(("parallel", "arbitrary")),
        name="mlp",
    )(x, g, wup, wdn)


def _pad_lanes(v, width=LANES):
    v = v.reshape(1, -1).astype(F32)
    return jnp.pad(v, ((0, 0), (0, width - v.shape[1])))


def _pad_rows(v, rows=SUBLANES, front=False):
    pad = rows - v.shape[-2]
    cfg = [(0, 0)] * (v.ndim - 2) + [((pad, 0) if front else (0, pad)), (0, 0)]
    return jnp.pad(v, cfg)


def _layer_weights(l, w_in, norm_mix, norm_mlp, ssd_conv_w, ssd_conv_b, ssd_dt_bias, ssd_a_log, ssd_d, ssd_norm,
                   w_ssd_out, sconv_w, w_sconv_out, cmp_pe, cmp_w1, cmp_w2, w_nsa_out, w_out, w_mlp_up, w_mlp_down):
    b = np.cumsum((0,) + IN_SIZES)
    wi = w_in[l]
    seg = lambda k: wi[:, b[k]:b[k + 1]]
    w_main = jnp.concatenate([seg(1), seg(7), seg(0), seg(3), seg(4), seg(5), seg(6), seg(9)], axis=1).astype(BF16)
    w_small = jnp.concatenate([seg(2), seg(8)], axis=1)
    w_small = jnp.pad(w_small, ((0, 0), (0, LANES - w_small.shape[1]))).astype(BF16)
    return dict(
        w_main=w_main, w_small=w_small,
        norm_mix=norm_mix[l].reshape(1, -1), norm_mlp=norm_mlp[l].reshape(1, -1),
        convw=_pad_rows(ssd_conv_w[l]), convb=ssd_conv_b[l].reshape(1, -1),
        dtb=_pad_lanes(ssd_dt_bias[l]), alog=_pad_lanes(ssd_a_log[l]), dvec=_pad_lanes(ssd_d[l]),
        ssdn=ssd_norm[l].reshape(1, -1), scw=_pad_rows(sconv_w[l]),
        w_ssd_out=w_ssd_out[l].astype(BF16), w_sconv_out=w_sconv_out[l].astype(BF16),
        w_nsa_out=w_nsa_out[l].astype(BF16), w_out=w_out[l].astype(BF16),
        cmp_pe=cmp_pe[l].reshape(2, 2, CMP_STRIDE, HEAD_DIM).transpose(0, 2, 1, 3).reshape(2, CMP_STRIDE, LANES),
        cmp_w1=cmp_w1[l].reshape(2, 2, CMP_STRIDE, HEAD_DIM, CMP_HID).transpose(0, 2, 1, 3, 4).reshape(
            2, CMP_STRIDE, LANES, CMP_HID).astype(BF16),
        cmp_w2=cmp_w2[l].astype(BF16),
        w_mlp_up=w_mlp_up[l].astype(BF16), w_mlp_down=w_mlp_down[l].astype(BF16),
    )


def _rope_tables(pos):
    half = HEAD_DIM // 2
    inv = ROPE_THETA ** (-jnp.arange(half, dtype=F32) / half)
    ang = pos.astype(F32)[:, None] * inv[None, :]
    cos, sin = jnp.cos(ang), jnp.sin(ang)
    cos_t = jnp.concatenate([cos, cos, cos, cos], axis=1)
    sin_t = jnp.concatenate([-sin, sin, -sin, sin], axis=1)
    return cos_t, sin_t


def _overlap(ncmp_pad, ncmp):
    c0 = np.arange(ncmp_pad)[:, None] * CMP_STRIDE
    s0 = np.arange(LANES)[None, :] * SEL_BLK
    ov = np.maximum(np.minimum(c0 + CMP_LEN, s0 + SEL_BLK) - np.maximum(c0, s0), 0).astype(np.float32) / CMP_LEN
    ov[ncmp:] = 0.0
    return jnp.asarray(ov)


def _tile(m, pref):
    t = pref
    while m % t:
        t //= 2
    return t


def _front(x2d, w, tm):
    u = _norm_matmul(x2d, w["norm_mix"], w["w_main"], tm, N_MAIN // 8)
    us = _norm_matmul(x2d, w["norm_mix"], w["w_small"], tm, LANES)
    return u, us


def _prompt_layer(x2d, w, bsz, t, tabs):
    m = bsz * t
    u, us = _front(x2d, w, _tile(m, 1024))
    cos_t, sin_t, ov = tabs
    qz, cmp_b, sel_b, win_b, cmp_t, sel_t, win_t = _rope(u, cos_t, sin_t, _tile(t, 512), bsz, t, True)
    zeros = lambda *s: jnp.zeros(s, F32)
    yn, ysc, h_new, ch_last = _seqmix(u, us, zeros(bsz, SUBLANES, CONV_DIM), zeros(bsz, SUBLANES, D_MODEL),
                                      zeros(bsz, D_MODEL, SSD_STATE), w, bsz, t)
    n_pg = t // CMP_PAGE
    kc, vc = _compress(cmp_b.reshape(1, bsz * n_pg, CMP_PAGE, 2 * KV_DIM), 0,
                       jnp.arange(bsz * n_pg, dtype=jnp.int32).reshape(bsz, n_pg),
                       w["cmp_pe"], w["cmp_w1"], w["cmp_w2"], False)
    tq = _tile(t, 256)
    oc, selb = _cmp_attn(qz, kc, vc, us, ov, bsz, t, tq, 0, N_TOP, None)
    ta = _tile(t, 512)
    osel = _band_attn(qz, sel_b, us, selb, bsz, t, ta, ta, True)
    ow = _band_attn(qz, win_b, us, None, bsz, t, ta, ta, False)
    x1 = _merge(x2d, yn, ysc, oc, osel, ow, u, w, _tile(m, 256))
    x2 = _mlp(x1, w["norm_mlp"], w["w_mlp_up"], w["w_mlp_down"], _tile(m, 512), 1024)
    wl = min(WINDOW, t)
    u3 = u.reshape(bsz, t, N_MAIN)
    rows_major = lambda a: a.reshape(bsz, 2, N_KV, HEAD_DIM, a.shape[-1]).transpose(0, 4, 1, 2, 3)
    state = (rows_major(cmp_t), rows_major(sel_t), rows_major(win_t[:, :, t - wl:]),
             h_new.reshape(bsz, H_SSD, SSD_HEAD_DIM, SSD_STATE),
             u3[:, t - (SSD_CONV - 1):, U_XBC:U_XBC + CONV_DIM],
             ch_last[:, SUBLANES - (SC_WIDTH - 1):])
    return x2, state


def _feature_major(cache):
    nd = cache.ndim
    perm = tuple(range(nd - 4)) + (nd - 3, nd - 2, nd - 1, nd - 4)
    return jnp.transpose(cache, perm).reshape(cache.shape[:-4] + (2 * KV_DIM, cache.shape[-4]))


def _sample_layer(x2d, w, bsz, t, tabs, layer, past):
    ssm0, conv0, sc0, cmp_pool, sel_pool, win_cache, win_rows_old, page_table = past
    m = bsz * t
    u, us = _front(x2d, w, m)
    cos_t, sin_t, ov = tabs
    qz, cmp_rows, sel_rows, win_rows = _rope(u, cos_t, sin_t, m, bsz, t, False)
    yn, ysc, h_new, ch_last = _seqmix(u, us, _pad_rows(conv0, front=True), _pad_rows(sc0, front=True),
                                      ssm0.reshape(bsz, D_MODEL, SSD_STATE), w, bsz, t)
    kc, vc = _compress(cmp_pool, layer, page_table, w["cmp_pe"], w["cmp_w1"], w["cmp_w2"], True)
    past_len = page_table.shape[1] * cmp_pool.shape[3]
    oc, selb = _cmp_attn(qz, kc, vc, us, ov, bsz, t, t, past_len, N_TOP - 1, past_len // SEL_BLK)
    osel = _sel_paged(qz, sel_pool, layer, page_table, sel_rows, us, selb, t)
    ow = _win_cached(qz, win_cache, layer, win_rows, us, t)
    x1 = _merge(x2d, yn, ysc, oc, osel, ow, u, w, m)
    x2 = _mlp(x1, w["norm_mlp"], w["w_mlp_up"], w["w_mlp_down"], m, 1024)
    wl = win_rows_old.shape[1]
    win_all = jnp.concatenate([win_rows_old, win_rows.reshape(bsz, t, 2, N_KV, HEAD_DIM)], axis=1)
    keep = min(WINDOW, wl + t)
    u3 = u.reshape(bsz, t, N_MAIN)
    state = (cmp_rows.reshape(bsz, t, 2, N_KV, HEAD_DIM), sel_rows.reshape(bsz, t, 2, N_KV, HEAD_DIM),
             win_all[:, wl + t - keep:],
             h_new.reshape(bsz, H_SSD, SSD_HEAD_DIM, SSD_STATE),
             u3[:, t - (SSD_CONV - 1):, U_XBC:U_XBC + CONV_DIM],
             ch_last[:, SUBLANES - (SC_WIDTH - 1):])
    return x2, state


def kernel(x_prompt, x_sample, cache_cmp_kv, cache_sel_kv, cache_win_kv, state_ssm, state_ssd_conv, state_sconv,
           page_table, norm_mix, norm_mlp, norm_final, w_in, ssd_conv_w, ssd_conv_b, ssd_dt_bias, ssd_a_log, ssd_d,
           ssd_norm, w_ssd_out, sconv_w, w_sconv_out, cmp_pe, cmp_w1, cmp_w2, w_nsa_out, w_out, w_mlp_up,
           w_mlp_down):
    depth = w_in.shape[0]
    bp, tp, _ = x_prompt.shape
    bs, ts, _ = x_sample.shape
    n_pool, page = cache_cmp_kv.shape[1:3]
    past_len = page_table.shape[1] * page
    wl = cache_win_kv.shape[2]
    for rows in (tp, past_len):
        assert rows % (LANES * CMP_STRIDE) == 0 and (rows // CMP_STRIDE) % min(CMP_TILE, rows // CMP_STRIDE) == 0
        assert rows // SEL_BLK <= LANES
    assert ts == SUBLANES
    assert wl == WINDOW and past_len >= WINDOW
    assert page_table.shape[1] % PAGE_GROUP == 0 and (tp // CMP_PAGE) % PAGE_GROUP == 0
    assert (past_len + ts - CMP_LEN) // CMP_STRIDE + 1 == (past_len - CMP_LEN) // CMP_STRIDE + 1

    tabs_p = _rope_tables(jnp.tile(jnp.arange(tp), bp)) + (_overlap(tp // CMP_STRIDE, (tp - CMP_LEN) // CMP_STRIDE + 1),)
    tabs_s = _rope_tables(jnp.tile(past_len + jnp.arange(ts), bs)) + (
        _overlap(past_len // CMP_STRIDE, (past_len - CMP_LEN) // CMP_STRIDE + 1),)

    cmp_pool, sel_pool, win_cache = _feature_major(cache_cmp_kv), _feature_major(cache_sel_kv), _feature_major(cache_win_kv)
    xp = x_prompt.reshape(bp * tp, D_MODEL)
    xs = x_sample.reshape(bs * ts, D_MODEL)
    p_new = [[] for _ in range(6)]
    s_new = [[] for _ in range(6)]
    for l in range(depth):
        w = _layer_weights(l, w_in, norm_mix, norm_mlp, ssd_conv_w, ssd_conv_b, ssd_dt_bias, ssd_a_log, ssd_d,
                           ssd_norm, w_ssd_out, sconv_w, w_sconv_out, cmp_pe, cmp_w1, cmp_w2, w_nsa_out, w_out,
                           w_mlp_up, w_mlp_down)
        xp, st_p = _prompt_layer(xp, w, bp, tp, tabs_p)
        past = (state_ssm[l], state_ssd_conv[l], state_sconv[l], cmp_pool, sel_pool, win_cache, cache_win_kv[l],
                page_table)
        xs, st_s = _sample_layer(xs, w, bs, ts, tabs_s, l, past)
        for i in range(6):
            p_new[i].append(st_p[i])
            s_new[i].append(st_s[i])
    gf = norm_final.reshape(1, -1)
    y_prompt = _final_norm(xp, gf, _tile(bp * tp, 1024)).reshape(bp, tp, D_MODEL)
    y_sample = _final_norm(xs, gf, bs * ts).reshape(bs, ts, D_MODEL)
    return (y_prompt, y_sample) + tuple(jnp.stack(a) for a in p_new) + tuple(jnp.stack(a) for a in s_new)
```

```python
import functools
import math

import numpy as np
import jax
import jax.numpy as jnp
from jax import lax
from jax.experimental import pallas as pl
from jax.experimental.pallas import tpu as pltpu

F32 = jnp.float32
BF16 = jnp.bfloat16

D_MODEL = 1024
SSD_HEAD_DIM = 64
H_SSD = D_MODEL // SSD_HEAD_DIM
SSD_STATE = 128
SSD_GROUPS = 2
SSD_CONV = 4
SSD_CHUNK = 128
CONV_DIM = D_MODEL + 2 * SSD_GROUPS * SSD_STATE
SC_WIDTH = 3
HEAD_DIM = 64
H_ATT = D_MODEL // HEAD_DIM
N_KV = 4
GQA = H_ATT // N_KV
KV_DIM = N_KV * HEAD_DIM
CMP_LEN = 32
CMP_STRIDE = 16
CMP_HID = 4 * HEAD_DIM
SEL_BLK = 64
N_TOP = 16
WINDOW = 512
D_FF = 4 * D_MODEL
ROPE_THETA = 10000.0
EPS = 1e-6
NEG_INF = -1e30
BIG = 1e9
TAKEN = -3e38
N_FORCED = 3
LOG2E = 1.4426950408889634
MASKV = -1e30
M_INIT = -1e29
IN_SIZES = (D_MODEL, CONV_DIM, H_SSD, D_MODEL, D_MODEL, D_MODEL, D_MODEL, 6 * KV_DIM, 3 * H_ATT, 3 * D_MODEL)

LANES = 128
SUBLANES = 8
VMEM_LIMIT = 56 * 1024 * 1024

U_XBC, U_KV, U_Z, U_SCB, U_SCC, U_SCH, U_Q, U_G = 0, 1536, 3072, 4096, 5120, 6144, 7168, 8192
N_MAIN = 11264
GATE_COL0 = H_SSD

_NT = (((1,), (1,)), ((), ()))
_TN = (((0,), (0,)), ((), ()))


def _cparams(sem):
    return pltpu.CompilerParams(dimension_semantics=sem, vmem_limit_bytes=VMEM_LIMIT)


def _silu(x):
    return x * (1.0 / (1.0 + jnp.exp(-x)))


def _sigmoid(x):
    return 1.0 / (1.0 + jnp.exp(-x))


def _norm_matmul_kernel(x_ref, g_ref, w_ref, o_ref, h_ref):
    @pl.when(pl.program_id(1) == 0)
    def _():
        x = x_ref[...]
        ms = jnp.mean(x * x, axis=-1, keepdims=True)
        h_ref[...] = (x * lax.rsqrt(ms + EPS) * g_ref[...]).astype(h_ref.dtype)

    o_ref[...] = jnp.dot(h_ref[...], w_ref[...], preferred_element_type=F32).astype(o_ref.dtype)


def _norm_matmul(x, g, w, tm, tn):
    m, d = x.shape
    n = w.shape[1]
    return pl.pallas_call(
        _norm_matmul_kernel,
        grid=(m // tm, n // tn),
        in_specs=[pl.BlockSpec((tm, d), lambda i, j: (i, 0)),
                  pl.BlockSpec((1, d), lambda i, j: (0, 0)),
                  pl.BlockSpec((d, tn), lambda i, j: (0, j))],
        out_specs=pl.BlockSpec((tm, tn), lambda i, j: (i, j)),
        out_shape=jax.ShapeDtypeStruct((m, n), F32),
        scratch_shapes=[pltpu.VMEM((tm, d), BF16)],
        compiler_params=_cparams(("parallel", "arbitrary")),
        name="norm_matmul",
    )(x, g, w)


def _final_norm_kernel(x_ref, g_ref, o_ref):
    x = x_ref[...]
    ms = jnp.mean(x * x, axis=-1, keepdims=True)
    o_ref[...] = x * lax.rsqrt(ms + EPS) * g_ref[...]


def _final_norm(x, g, tm):
    m, d = x.shape
    return pl.pallas_call(
        _final_norm_kernel,
        grid=(m // tm,),
        in_specs=[pl.BlockSpec((tm, d), lambda i: (i, 0)), pl.BlockSpec((1, d), lambda i: (0, 0))],
        out_specs=pl.BlockSpec((tm, d), lambda i: (i, 0)),
        out_shape=jax.ShapeDtypeStruct((m, d), F32),
        compiler_params=_cparams(("parallel",)),
        name="final_norm",
    )(x, g)


def _rot_half(x, first_half):
    return jnp.where(first_half, pltpu.roll(x, LANES - HEAD_DIM // 2, axis=1), pltpu.roll(x, HEAD_DIM // 2, axis=1))


def _rope_kernel(q_ref, kv_ref, cos_ref, sin_ref, qz_ref, *out_refs, scale, feature_major):
    row_refs, t_refs = out_refs[:3], out_refs[3:]
    cos = cos_ref[...]
    sin = sin_ref[...]
    lane = lax.broadcasted_iota(jnp.int32, cos.shape, 1)
    first_half = (lane % HEAD_DIM) < (HEAD_DIM // 2)
    left = lane < HEAD_DIM

    def rope(x):
        return x * cos + _rot_half(x, first_half) * sin

    for c in range(D_MODEL // LANES):
        r = rope(q_ref[:, c * LANES:(c + 1) * LANES]) * scale
        sw = pltpu.roll(r, HEAD_DIM, axis=1)
        h = (2 * c) // GQA
        g = (2 * c) % GQA
        if h % 2 == 0:
            qz_ref[h, g] = jnp.where(left, r, 0.0).astype(qz_ref.dtype)
            qz_ref[h, g + 1] = jnp.where(left, sw, 0.0).astype(qz_ref.dtype)
        else:
            qz_ref[h, g] = jnp.where(left, 0.0, sw).astype(qz_ref.dtype)
            qz_ref[h, g + 1] = jnp.where(left, 0.0, r).astype(qz_ref.dtype)
    for br in range(3):
        base = br * 2 * KV_DIM
        for c in range(2 * KV_DIM // LANES):
            x = kv_ref[:, base + c * LANES: base + (c + 1) * LANES]
            if c < KV_DIM // LANES:
                x = rope(x)
            row_refs[br][:, c * LANES:(c + 1) * LANES] = x.astype(row_refs[br].dtype)
            if feature_major:
                t_refs[br][0, c * LANES:(c + 1) * LANES, :] = x.T


def _rope(u, cos, sin, tm, bsz, t, prompt):
    m = u.shape[0]
    nt = t // tm if prompt else 1
    act = BF16 if prompt else F32
    kv_spec = pl.BlockSpec((tm, 2 * KV_DIM), lambda i: (i, 0))
    out_specs = [pl.BlockSpec((N_KV, GQA, tm, LANES), lambda i: (0, 0, i, 0)), kv_spec, kv_spec, kv_spec]
    out_shape = [jax.ShapeDtypeStruct((N_KV, GQA, m, LANES), act)] + [jax.ShapeDtypeStruct((m, 2 * KV_DIM), act)] * 3
    if prompt:
        out_specs += [pl.BlockSpec((1, 2 * KV_DIM, tm), lambda i: (i // nt, 0, i % nt))] * 3
        out_shape += [jax.ShapeDtypeStruct((bsz, 2 * KV_DIM, t), F32)] * 3
    return pl.pallas_call(
        functools.partial(_rope_kernel, scale=HEAD_DIM ** -0.5 * LOG2E, feature_major=prompt),
        grid=(m // tm,),
        in_specs=[pl.BlockSpec((tm, D_MODEL), lambda i: (i, U_Q // D_MODEL)),
                  pl.BlockSpec((tm, 6 * KV_DIM), lambda i: (i, U_KV // (6 * KV_DIM))),
                  pl.BlockSpec((tm, LANES), lambda i: (i, 0)),
                  pl.BlockSpec((tm, LANES), lambda i: (i, 0))],
        out_specs=out_specs,
        out_shape=out_shape,
        compiler_params=_cparams(("parallel",)),
        name="rope",
    )(u, u, cos, sin)


def _seqmix_kernel(xbc_ref, z_ref, scb_ref, scc_ref, sch_ref, dtr_ref, conv0_ref, sc0_ref, h0_ref,
                   convw_ref, convb_ref, dtb_ref, alog_ref, dvec_ref, ssdn_ref, scw_ref,
                   yn_ref, ysc_ref, hout_ref, chlast_ref,
                   ext_ref, chext_ref, h_ref, y_ref, *, chunk, valid):
    c = pl.program_id(1)
    L = chunk
    P = SSD_HEAD_DIM
    N = SSD_STATE

    @pl.when(c == 0)
    def _():
        ext_ref[0:SUBLANES, :] = conv0_ref[0]
        chext_ref[0:SUBLANES, :] = sc0_ref[0]
        h_ref[...] = h0_ref[0]

    def load(ref):
        x = ref[...]
        if valid < L:
            x = jnp.concatenate([x, jnp.zeros((L - valid, x.shape[1]), x.dtype)], axis=0)
        return x

    ext_ref[SUBLANES:SUBLANES + L, :] = load(xbc_ref)
    conv = jnp.broadcast_to(convb_ref[...], (L, CONV_DIM))
    for k in range(SSD_CONV):
        off = SUBLANES - (SSD_CONV - 1) + k
        conv = conv + convw_ref[k:k + 1, :] * ext_ref[off:off + L, :]
    ext_ref[0:SUBLANES, :] = ext_ref[L:L + SUBLANES, :]
    xbc = _silu(conv)

    ch = load(scc_ref) * load(sch_ref)
    chext_ref[SUBLANES:SUBLANES + L, :] = ch
    cv = jnp.zeros((L, D_MODEL), F32)
    for k in range(SC_WIDTH):
        off = SUBLANES - (SC_WIDTH - 1) + k
        cv = cv + scw_ref[k:k + 1, :] * chext_ref[off:off + L, :]
    ysc_ref[...] = (load(scb_ref) * cv)[:valid].astype(ysc_ref.dtype)
    chlast_ref[0] = chext_ref[valid:valid + SUBLANES, :]
    chext_ref[0:SUBLANES, :] = chext_ref[L:L + SUBLANES, :]

    x = load(dtr_ref) + dtb_ref[...]
    dt = jnp.maximum(x, 0.0) + jnp.log1p(jnp.exp(-jnp.abs(x)))
    row = lax.broadcasted_iota(jnp.int32, (L, LANES), 0)
    if valid < L:
        dt = jnp.where(row < valid, dt, 0.0)
    a = -jnp.exp(alog_ref[...])
    da = dt * a
    ri = lax.broadcasted_iota(jnp.int32, (L, L), 0)
    ci = lax.broadcasted_iota(jnp.int32, (L, L), 1)
    tril = ri >= ci
    acum = jnp.dot(tril.astype(F32), da, precision=lax.Precision.HIGHEST, preferred_element_type=F32)
    acum_t = acum.T
    a_last = acum[L - 1:L, :]

    for g in range(SSD_GROUPS):
        bg = xbc[:, D_MODEL + g * N: D_MODEL + (g + 1) * N]
        cg = xbc[:, D_MODEL + SSD_GROUPS * N + g * N: D_MODEL + SSD_GROUPS * N + (g + 1) * N]
        cb = lax.dot_general(cg, bg, _NT, preferred_element_type=F32)
        for r in range(H_SSD // SSD_GROUPS):
            h = g * (H_SSD // SSD_GROUPS) + r
            a_col = acum[:, h:h + 1]
            a_row = acum_t[h:h + 1, :]
            seg = jnp.where(tril, a_col - a_row, NEG_INF)
            wts = jnp.exp(seg) * cb
            xs_h = xbc[:, h * P:(h + 1) * P]
            xdt = xs_h * dt[:, h:h + 1]
            hprev = h_ref[h * P:(h + 1) * P, :]
            y_h = jnp.dot(wts, xdt, preferred_element_type=F32)
            y_in = lax.dot_general(cg, hprev, _NT, preferred_element_type=F32)
            y_h = y_h + y_in * jnp.exp(a_col)
            y_h = y_h + dvec_ref[:, h:h + 1] * xs_h
            y_ref[:, h * P:(h + 1) * P] = y_h
            al = a_last[:, h:h + 1]
            xw = xdt * jnp.exp(al - a_col)
            s_h = lax.dot_general(xw, bg, _TN, preferred_element_type=F32)
            h_ref[h * P:(h + 1) * P, :] = jnp.exp(al) * hprev + s_h

    y = y_ref[0:valid, :] * _silu(z_ref[...])
    ms = jnp.mean(y * y, axis=-1, keepdims=True)
    yn_ref[...] = (y * lax.rsqrt(ms + EPS) * ssdn_ref[...]).astype(yn_ref.dtype)
    hout_ref[0] = h_ref[...]


def _seqmix(u, us, conv0, sc0, h0, prm, bsz, t):
    L = SSD_CHUNK
    valid = min(t, L)
    nc = t // valid
    m = bsz * t
    act = _act_dtype(valid)
    row = lambda b, c: b * nc + c
    spec_u = lambda w, col: pl.BlockSpec((valid, w), lambda b, c: (row(b, c), col // w))
    full = lambda shape: pl.BlockSpec(shape, lambda b, c: (0,) * len(shape))
    perb = lambda shape: pl.BlockSpec((1,) + shape, lambda b, c: (b,) + (0,) * len(shape))
    return pl.pallas_call(
        functools.partial(_seqmix_kernel, chunk=L, valid=valid),
        grid=(bsz, nc),
        in_specs=[spec_u(CONV_DIM, U_XBC), spec_u(D_MODEL, U_Z), spec_u(D_MODEL, U_SCB), spec_u(D_MODEL, U_SCC),
                  spec_u(D_MODEL, U_SCH),
                  pl.BlockSpec((valid, LANES), lambda b, c: (row(b, c), 0)),
                  perb((SUBLANES, CONV_DIM)), perb((SUBLANES, D_MODEL)), perb((D_MODEL, SSD_STATE)),
                  full((SUBLANES, CONV_DIM)), full((1, CONV_DIM)), full((1, LANES)), full((1, LANES)),
                  full((1, LANES)), full((1, D_MODEL)), full((SUBLANES, D_MODEL))],
        out_specs=[pl.BlockSpec((valid, D_MODEL), lambda b, c: (row(b, c), 0)),
                   pl.BlockSpec((valid, D_MODEL), lambda b, c: (row(b, c), 0)),
                   perb((D_MODEL, SSD_STATE)), perb((SUBLANES, D_MODEL))],
        out_shape=[jax.ShapeDtypeStruct((m, D_MODEL), act), jax.ShapeDtypeStruct((m, D_MODEL), act),
                   jax.ShapeDtypeStruct((bsz, D_MODEL, SSD_STATE), F32),
                   jax.ShapeDtypeStruct((bsz, SUBLANES, D_MODEL), F32)],
        scratch_shapes=[pltpu.VMEM((L + SUBLANES, CONV_DIM), F32), pltpu.VMEM((L + SUBLANES, D_MODEL), F32),
                        pltpu.VMEM((D_MODEL, SSD_STATE), F32), pltpu.VMEM((L, D_MODEL), F32)],
        compiler_params=_cparams(("parallel", "arbitrary")),
        name="seqmix",
    )(u, u, u, u, u, us, conv0, sc0, h0, prm["convw"], prm["convb"], prm["dtb"], prm["alog"], prm["dvec"],
      prm["ssdn"], prm["scw"])


CMP_TILE = 512
CMP_PAGE = 128
PAGE_GROUP = 8
SEL_PAGE_GROUP = 16


N_COMBO = 2 * N_KV


def _compress_fill(z_ref, page, g0, feature_major):
    R = page.shape[1] if feature_major else page.shape[0]
    lo = pl.ds(g0 + CMP_STRIDE, R)
    hi = pl.ds(g0, R)
    for pr in range(N_COMBO // 2):
        if feature_major:
            x2 = page[pr * LANES:(pr + 1) * LANES, :].T
        else:
            x2 = page[:, pr * LANES:(pr + 1) * LANES].astype(F32)
        sw = pltpu.roll(x2, HEAD_DIM, axis=1)
        z_ref[2 * pr, lo, 0:HEAD_DIM] = x2[:, 0:HEAD_DIM]
        z_ref[2 * pr, hi, HEAD_DIM:LANES] = sw[:, HEAD_DIM:LANES]
        z_ref[2 * pr + 1, lo, 0:HEAD_DIM] = sw[:, 0:HEAD_DIM]
        z_ref[2 * pr + 1, hi, HEAD_DIM:LANES] = x2[:, HEAD_DIM:LANES]


def _compress_tile(z_ref, row0, pe_ref, w1_ref, w2_ref, out_refs, out_row0, tile):
    for c in range(N_COMBO):
        kv, h = divmod(c, N_KV)
        acc = jnp.zeros((tile, CMP_HID), F32)
        for r in range(CMP_STRIDE):
            zr = z_ref[c, pl.ds(row0 + CMP_STRIDE + r, tile, stride=CMP_STRIDE), :] + pe_ref[kv, r:r + 1, :]
            acc = acc + jnp.dot(zr.astype(BF16), w1_ref[kv, r], preferred_element_type=F32)
        o = jnp.dot(_silu(acc).astype(BF16), w2_ref[kv], preferred_element_type=F32)
        out_refs[kv][0, pl.ds(out_row0, tile), pl.ds(h * HEAD_DIM, HEAD_DIM)] = o.astype(out_refs[kv].dtype)


def _compress_kernel(pt_ref, *refs, n_pages, page, group, feature_major):
    page_refs = refs[:group]
    pe_ref, w1_ref, w2_ref, kc_ref, vc_ref, z_ref = refs[group:]
    p = pl.program_id(1)
    for k in range(group):
        _compress_fill(z_ref, page_refs[k][0, 0], pl.multiple_of((p * group + k) * page, page), feature_major)

    @pl.when(p == n_pages // group - 1)
    def _():
        rows = n_pages * page
        z_ref[:, rows:rows + CMP_STRIDE, HEAD_DIM:LANES] = jnp.zeros((N_COMBO, CMP_STRIDE, LANES - HEAD_DIM), F32)

        tile = min(CMP_TILE, rows // CMP_STRIDE)

        def body(j, carry):
            _compress_tile(z_ref, pl.multiple_of(j * tile * CMP_STRIDE, tile * CMP_STRIDE),
                           pe_ref, w1_ref, w2_ref, (kc_ref, vc_ref), pl.multiple_of(j * tile, tile), tile)
            return carry

        lax.fori_loop(0, rows // (tile * CMP_STRIDE), body, 0)


def _page_shape(pool, feature_major):
    return (1, 1) + pool.shape[2:], pool.shape[3] if feature_major else pool.shape[2]


def _compress(pool, layer, page_table, pe2, w1p, w2, feature_major):
    bsz, n_pages = page_table.shape
    blk, page = _page_shape(pool, feature_major)
    rows = n_pages * page
    group = PAGE_GROUP
    out = jax.ShapeDtypeStruct((bsz, rows // CMP_STRIDE, KV_DIM), BF16)
    full = lambda shape: pl.BlockSpec(shape, lambda b, p, pt: (0,) * len(shape))
    page_spec = lambda k: pl.BlockSpec(blk, lambda b, p, pt: (layer, pt[b, p * group + k], 0, 0))
    grid_spec = pltpu.PrefetchScalarGridSpec(
        num_scalar_prefetch=1,
        grid=(bsz, n_pages // group),
        in_specs=[page_spec(k) for k in range(group)] + [full(pe2.shape), full(w1p.shape), full(w2.shape)],
        out_specs=[pl.BlockSpec((1, rows // CMP_STRIDE, KV_DIM), lambda b, p, pt: (b, 0, 0)),
                   pl.BlockSpec((1, rows // CMP_STRIDE, KV_DIM), lambda b, p, pt: (b, 0, 0))],
        scratch_shapes=[pltpu.VMEM((N_COMBO, rows + CMP_STRIDE, LANES), F32)],
    )
    return pl.pallas_call(
        functools.partial(_compress_kernel, n_pages=n_pages, page=page, group=group, feature_major=feature_major),
        grid_spec=grid_spec,
        out_shape=[out, out],
        compiler_params=_cparams(("parallel", "arbitrary")),
        name="compress",
    )(page_table, *([pool] * group), pe2, w1p, w2)


BF16_ROWS = 16


def _act_dtype(block_rows):
    return BF16 if block_rows % BF16_ROWS == 0 else F32


def _left_lanes(shape):
    return lax.broadcasted_iota(jnp.int32, shape, len(shape) - 1) < HEAD_DIM


def _own_half(left, h):
    return left if h % 2 == 0 else jnp.logical_not(left)


def _value_with_ones(v2, own):
    return jnp.where(own, v2, jnp.ones_like(v2))


def _normalise(acc, own):
    l = jnp.where(own, pltpu.roll(acc, HEAD_DIM, axis=1), acc)
    rinv = jnp.where(l > 0.0, 1.0 / l, 0.0)
    return acc * rinv, rinv


def _emit_heads(o_ref, pieces, halves):
    left = _left_lanes(pieces[0].shape)
    for c in range(H_ATT // 2):
        a, b = pieces[2 * c], pieces[2 * c + 1]
        a = a if halves[2 * c] == 0 else pltpu.roll(a, HEAD_DIM, axis=1)
        b = b if halves[2 * c + 1] == 1 else pltpu.roll(b, HEAD_DIM, axis=1)
        o_ref[:, c * LANES:(c + 1) * LANES] = jnp.where(left, a, b).astype(o_ref.dtype)


def _gate_col(gate, hq, branch):
    col = GATE_COL0 + hq * 3 + branch
    return gate[:, col:col + 1]


def _topk_lanes(vals, n_iter):
    lane_f = lax.broadcasted_iota(jnp.int32, vals.shape, 1).astype(F32)

    def body(_, vals):
        mx = jnp.max(vals, axis=-1, keepdims=True)
        first = jnp.min(jnp.where(vals == mx, lane_f, float(LANES)), axis=-1, keepdims=True)
        return jnp.where(lane_f == first, TAKEN, vals)

    return jnp.where(lax.fori_loop(0, n_iter, body, vals) == TAKEN, 1.0, 0.0)


def _topk_rows(vals, n_iter):
    row_f = lax.broadcasted_iota(jnp.int32, vals.shape, 0).astype(F32)

    def body(_, vals):
        mx = jnp.max(vals, axis=0, keepdims=True)
        first = jnp.min(jnp.where(vals == mx, row_f, float(LANES)), axis=0, keepdims=True)
        return jnp.where(row_f == first, TAKEN, vals)

    return jnp.where(lax.fori_loop(0, n_iter, body, vals) == TAKEN, 1.0, 0.0)


def _split_bf16x3(x):
    hi = x.astype(BF16)
    r1 = x - hi.astype(F32)
    mid = r1.astype(BF16)
    lo = (r1 - mid.astype(F32)).astype(BF16)
    return hi, mid, lo


def _cmp_attn_kernel(qz_ref, kc_ref, vc_ref, gate_ref, ov_ref, o_ref, selb_ref, *, tq, pos0, n_iter, n_past_blk):
    i = pl.program_id(1)
    n_all = kc_ref.shape[1]
    q_last = pos0 + (i + 1) * tq - 1
    n_vis = jnp.maximum(q_last - (CMP_LEN - 1), 0) // CMP_STRIDE + 1
    steps = jnp.minimum((n_vis + LANES - 1) // LANES, n_all // LANES)
    for w in range(1, n_all // LANES + 1):
        @pl.when(steps == w)
        def _(w=w):
            _cmp_attn_body(qz_ref, kc_ref, vc_ref, gate_ref, ov_ref, o_ref, selb_ref, tq=tq, pos0=pos0, n_iter=n_iter,
                           n_past_blk=n_past_blk, ncmp=w * LANES)


def _cmp_attn_body(qz_ref, kc_ref, vc_ref, gate_ref, ov_ref, o_ref, selb_ref, *, tq, pos0, n_iter, n_past_blk, ncmp):
    i = pl.program_id(1)
    rows = GQA * tq
    ov_b = ov_ref[0:ncmp, :].astype(BF16)
    qpos = pos0 + i * tq + lax.broadcasted_iota(jnp.int32, (tq, 1), 0)
    cend = lax.broadcasted_iota(jnp.int32, (1, ncmp), 1) * CMP_STRIDE + (CMP_LEN - 1)
    bias = jnp.where(cend <= qpos, 0.0, MASKV)
    gate = _sigmoid(gate_ref[...])
    left = _left_lanes((rows, LANES))
    transposed = tq % LANES == 0
    shape = (LANES, tq) if transposed else (tq, LANES)
    blk = lax.broadcasted_iota(jnp.int32, shape, 0 if transposed else 1)
    if n_past_blk is None:
        cur = (pos0 + i * tq + lax.broadcasted_iota(jnp.int32, shape, 1 if transposed else 0)) // SEL_BLK
        forced = (blk == 0) | (blk == cur) | (blk == cur - 1)
        allowed = blk <= cur
    else:
        forced = (blk == 0) | (blk == n_past_blk - 1)
        allowed = blk < n_past_blk
    pieces, halves = [None] * H_ATT, [0] * H_ATT

    def scores(h):
        pr = h // 2
        q = qz_ref[h].reshape(rows, LANES).astype(BF16)
        s = lax.dot_general(q, kc_ref[0, 0:ncmp, pr * LANES:(pr + 1) * LANES], _NT, preferred_element_type=F32)
        return (s.reshape(GQA, tq, ncmp) + bias[None]).reshape(rows, ncmp)

    vals = []
    s = scores(0)
    for h in range(N_KV):
        s_next = scores(h + 1) if h + 1 < N_KV else None
        pr = h // 2
        own = _own_half(left, h)
        v2 = vc_ref[0, 0:ncmp, pr * LANES:(pr + 1) * LANES]
        m = jnp.maximum(jnp.max(s, axis=-1, keepdims=True), M_INIT)
        e = jnp.exp2(s - m)
        acc = jnp.dot(e.astype(BF16), _value_with_ones(v2, _own_half(_left_lanes(v2.shape), h)),
                      preferred_element_type=F32)
        o, rinv = _normalise(acc, own)
        for g in range(GQA):
            hq = h * GQA + g
            pieces[hq] = o[g * tq:(g + 1) * tq] * _gate_col(gate, hq, 0)
            halves[hq] = h % 2
        p = e * jnp.concatenate([rinv] * (ncmp // LANES), axis=1)
        psum = jnp.sum(p.reshape(GQA, tq, ncmp), axis=0)
        imp = sum(jnp.dot(part, ov_b, preferred_element_type=F32) for part in _split_bf16x3(psum))
        vals.append(jnp.where(forced, TAKEN, jnp.where(allowed, imp.T if transposed else imp, NEG_INF)))
        s = s_next
    _emit_heads(o_ref, pieces, halves)
    for h in range(N_KV):
        sel_h = _topk_rows(vals[h], n_iter).T if transposed else _topk_lanes(vals[h], n_iter)
        selb_ref[0, h] = jnp.where(sel_h > 0.5, 0.0, MASKV).astype(selb_ref.dtype)


def _cmp_attn(qz, kc, vc, us, ov, bsz, t, tq, pos0, n_iter, n_past_blk):
    nq = t // tq
    ncmp = kc.shape[1]
    return pl.pallas_call(
        functools.partial(_cmp_attn_kernel, tq=tq, pos0=pos0, n_iter=n_iter, n_past_blk=n_past_blk),
        grid=(bsz, nq),
        in_specs=[pl.BlockSpec((N_KV, GQA, tq, LANES), lambda b, i: (0, 0, b * nq + i, 0)),
                  pl.BlockSpec((1, ncmp, KV_DIM), lambda b, i: (b, 0, 0)),
                  pl.BlockSpec((1, ncmp, KV_DIM), lambda b, i: (b, 0, 0)),
                  pl.BlockSpec((tq, LANES), lambda b, i: (b * nq + i, 0)),
                  pl.BlockSpec(ov.shape, lambda b, i: (0, 0))],
        out_specs=[pl.BlockSpec((tq, D_MODEL), lambda b, i: (b * nq + i, 0)),
                   pl.BlockSpec((1, N_KV, tq, LANES), lambda b, i: (b, 0, i, 0))],
        out_shape=[jax.ShapeDtypeStruct((bsz * t, D_MODEL), _act_dtype(tq)),
                   jax.ShapeDtypeStruct((bsz, N_KV, t, LANES), BF16)],
        compiler_params=_cparams(("parallel", "parallel")),
        name="cmp_attn",
    )(qz, kc, vc, us, ov)


def _block_expand(k0, tk):
    blk = lax.broadcasted_iota(jnp.int32, (LANES, tk), 0)
    col = lax.broadcasted_iota(jnp.int32, (LANES, tk), 1)
    return (blk == (k0 + col) // SEL_BLK).astype(BF16)


def _band_attn_kernel(qz_ref, kv_ref, gate_ref, *rest, tq, tk, nk, nback, branch, use_sel):
    if use_sel:
        selb_ref, o_ref, m_ref, acc_ref = rest
    else:
        o_ref, m_ref, acc_ref = rest
    i = pl.program_id(1)
    kk = pl.program_id(2)
    rows = GQA * tq

    @pl.when(kk == 0)
    def _():
        m_ref[...] = jnp.full(m_ref.shape, M_INIT, F32)
        acc_ref[...] = jnp.zeros(acc_ref.shape, F32)

    if use_sel:
        kt = kk
        live = kt * tk <= i * tq + tq - 1
    else:
        kt = i * (tq // tk) - nback + kk
        live = kt >= 0

    @pl.when(live)
    def _():
        k0 = kt * tk
        dlt = (i * tq + lax.broadcasted_iota(jnp.int32, (tq, tk), 0)) - (k0 + lax.broadcasted_iota(jnp.int32, (tq, tk), 1))
        if use_sel:
            pos_bias = jnp.where(dlt >= 0, 0.0, MASKV)
            expand = _block_expand(k0, tk)
        else:
            pos_bias = jnp.where((dlt >= 0) & (dlt < WINDOW), 0.0, MASKV)
        left_v = _left_lanes((tk, LANES))

        def scores(h):
            pr = (h // 2) * LANES
            bias = pos_bias
            if use_sel:
                bias = bias + jnp.dot(selb_ref[0, h], expand, preferred_element_type=F32)
            q = qz_ref[h].reshape(rows, LANES)
            s = lax.dot_general(q, kv_ref[:, pr:pr + LANES], _NT, preferred_element_type=F32)
            return (s.reshape(GQA, tq, tk) + bias[None]).reshape(rows, tk)

        s = scores(0)
        for h in range(N_KV):
            s_next = scores(h + 1) if h + 1 < N_KV else None
            pr = (h // 2) * LANES
            vaug = _value_with_ones(kv_ref[:, KV_DIM + pr: KV_DIM + pr + LANES], _own_half(left_v, h))
            m_old = m_ref[h]
            m_new = jnp.maximum(m_old, jnp.max(s, axis=-1, keepdims=True))
            e = jnp.exp2(s - m_new[:, 0:1]).astype(BF16)
            alpha = jnp.exp2(m_old - m_new)
            acc_ref[h] = alpha * acc_ref[h] + jnp.dot(e, vaug, preferred_element_type=F32)
            m_ref[h] = m_new
            s = s_next

    @pl.when(kk == nk - 1)
    def _():
        gate = _sigmoid(gate_ref[...])
        left = _left_lanes((rows, LANES))
        pieces, halves = [None] * H_ATT, [0] * H_ATT
        for h in range(N_KV):
            o, _ = _normalise(acc_ref[h], _own_half(left, h))
            for g in range(GQA):
                hq = h * GQA + g
                pieces[hq] = o[g * tq:(g + 1) * tq] * _gate_col(gate, hq, branch)
                halves[hq] = h % 2
        _emit_heads(o_ref, pieces, halves)


def _band_attn(qz, kvb, us, selb, bsz, t, tq, tk, use_sel):
    nq = t // tq
    nkt = t // tk
    if use_sel:
        nk, nback, branch = nkt, 0, 1
        kv_idx = lambda b, i, kk: (b * nkt + jnp.minimum(kk, (i * tq + tq - 1) // tk), 0)
    else:
        nback = -(-(WINDOW - 1) // tk)
        nk, branch = nback + tq // tk, 2
        kv_idx = lambda b, i, kk: (b * nkt + jnp.maximum(i * (tq // tk) - nback + kk, 0), 0)
    in_specs = [pl.BlockSpec((N_KV, GQA, tq, LANES), lambda b, i, kk: (0, 0, b * nq + i, 0)),
                pl.BlockSpec((tk, 2 * KV_DIM), kv_idx),
                pl.BlockSpec((tq, LANES), lambda b, i, kk: (b * nq + i, 0))]
    args = [qz, kvb, us]
    if use_sel:
        in_specs.append(pl.BlockSpec((1, N_KV, tq, LANES), lambda b, i, kk: (b, 0, i, 0)))
        args.append(selb)
    return pl.pallas_call(
        functools.partial(_band_attn_kernel, tq=tq, tk=tk, nk=nk, nback=nback, branch=branch, use_sel=use_sel),
        grid=(bsz, nq, nk),
        in_specs=in_specs,
        out_specs=pl.BlockSpec((tq, D_MODEL), lambda b, i, kk: (b * nq + i, 0)),
        out_shape=jax.ShapeDtypeStruct((bsz * t, D_MODEL), BF16),
        scratch_shapes=[pltpu.VMEM((N_KV, GQA * tq, LANES), F32), pltpu.VMEM((N_KV, GQA * tq, LANES), F32)],
        compiler_params=_cparams(("parallel", "parallel", "arbitrary")),
        name="sel_attn" if use_sel else "win_attn",
    )(*args)


N_PAIR = N_KV // 2


def _pair_init(m_ref, l_ref, acc_ref):
    m_ref[...] = jnp.full(m_ref.shape, M_INIT, F32)
    l_ref[...] = jnp.zeros(l_ref.shape, F32)
    acc_ref[...] = jnp.zeros(acc_ref.shape, F32)


def _pair_queries(qz_ref, pr, tq):
    return jnp.concatenate([qz_ref[2 * pr].reshape(GQA * tq, LANES), qz_ref[2 * pr + 1].reshape(GQA * tq, LANES)],
                           axis=0).astype(BF16)


def _pair_update(pr, q, kv, bias_a, bias_b, m_ref, l_ref, acc_ref, feature_major=False):
    bias = jnp.concatenate([bias_a] * GQA + [bias_b] * GQA, axis=0)
    if feature_major:
        k2 = kv[pr * LANES:(pr + 1) * LANES, :]
        v2 = kv[KV_DIM + pr * LANES: KV_DIM + (pr + 1) * LANES, :]
        s = jnp.dot(q, k2, preferred_element_type=F32) + bias
        pv_dims = _NT
    else:
        k2 = kv[:, pr * LANES:(pr + 1) * LANES]
        v2 = kv[:, KV_DIM + pr * LANES: KV_DIM + (pr + 1) * LANES]
        s = lax.dot_general(q, k2, _NT, preferred_element_type=F32) + bias
        pv_dims = (((1,), (0,)), ((), ()))
    m_old = m_ref[pr]
    m_new = jnp.maximum(m_old, jnp.max(s, axis=-1, keepdims=True))
    e = jnp.exp2(s - m_new[:, 0:1])
    alpha = jnp.exp2(m_old - m_new)
    l_ref[pr] = alpha * l_ref[pr] + jnp.sum(e, axis=-1, keepdims=True)
    acc_ref[pr] = alpha * acc_ref[pr] + lax.dot_general(e.astype(BF16), v2, pv_dims, preferred_element_type=F32)
    m_ref[pr] = m_new


def _pair_finish(o_ref, gate, branch, tq, l_ref, acc_ref):
    pieces, halves = [None] * H_ATT, [0] * H_ATT
    for pr in range(N_PAIR):
        l = l_ref[pr]
        o = acc_ref[pr] * jnp.where(l > 0.0, 1.0 / l, 0.0)
        for side in range(2):
            for g in range(GQA):
                hq = (2 * pr + side) * GQA + g
                r0 = (side * GQA + g) * tq
                pieces[hq] = o[r0:r0 + tq] * _gate_col(gate, hq, branch)
                halves[hq] = side
    _emit_heads(o_ref, pieces, halves)


def _causal_bias(tq):
    qi = lax.broadcasted_iota(jnp.int32, (tq, tq), 0)
    ki = lax.broadcasted_iota(jnp.int32, (tq, tq), 1)
    return jnp.where(ki <= qi, 0.0, MASKV)


def _pair_scratch(tq):
    rows = 2 * GQA * tq
    return [pltpu.VMEM((N_PAIR, rows, LANES), F32)] * 3


def _sel_paged_kernel(pt_ref, *refs, tq, n_steps, page, group):
    page_refs = refs[:group]
    qz_ref, new_ref, gate_ref, selb_ref, o_ref, m_ref, l_ref, acc_ref = refs[group:]
    p = pl.program_id(1)

    @pl.when(p == 0)
    def _():
        _pair_init(m_ref, l_ref, acc_ref)

    @pl.when(p < n_steps)
    def _():
        kv = jnp.concatenate([r[0, 0] for r in page_refs], axis=1).astype(BF16)
        expand = _block_expand(p * (group * page), group * page)
        bias = [jnp.dot(selb_ref[0, h], expand, preferred_element_type=F32) for h in range(N_KV)]
        for pr in range(N_PAIR):
            _pair_update(pr, _pair_queries(qz_ref, pr, tq), kv, bias[2 * pr], bias[2 * pr + 1], m_ref, l_ref, acc_ref,
                         feature_major=True)

    @pl.when(p == n_steps)
    def _():
        kv = new_ref[...].astype(BF16)
        bias = _causal_bias(tq)
        for pr in range(N_PAIR):
            _pair_update(pr, _pair_queries(qz_ref, pr, tq), kv, bias, bias, m_ref, l_ref, acc_ref)
        _pair_finish(o_ref, _sigmoid(gate_ref[...]), 1, tq, l_ref, acc_ref)


def _sel_paged(qz, pool, layer, page_table, new_rows, us, selb, tq):
    bsz, n_pages = page_table.shape
    blk, page = _page_shape(pool, True)
    group = SEL_PAGE_GROUP
    n_steps = n_pages // group
    page_spec = lambda k: pl.BlockSpec(
        blk, lambda b, p, pt: (layer, pt[b, jnp.minimum(p, n_steps - 1) * group + k], 0, 0))
    grid_spec = pltpu.PrefetchScalarGridSpec(
        num_scalar_prefetch=1,
        grid=(bsz, n_steps + 1),
        in_specs=[page_spec(k) for k in range(group)] + [
            pl.BlockSpec((N_KV, GQA, tq, LANES), lambda b, p, pt: (0, 0, b, 0)),
            pl.BlockSpec((tq, 2 * KV_DIM), lambda b, p, pt: (b, 0)),
            pl.BlockSpec((tq, LANES), lambda b, p, pt: (b, 0)),
            pl.BlockSpec((1, N_KV, tq, LANES), lambda b, p, pt: (b, 0, 0, 0))],
        out_specs=pl.BlockSpec((tq, D_MODEL), lambda b, p, pt: (b, 0)),
        scratch_shapes=_pair_scratch(tq),
    )
    return pl.pallas_call(
        functools.partial(_sel_paged_kernel, tq=tq, n_steps=n_steps, page=page, group=group),
        grid_spec=grid_spec,
        out_shape=jax.ShapeDtypeStruct((bsz * tq, D_MODEL), F32),
        compiler_params=_cparams(("parallel", "arbitrary")),
        name="sel_paged",
    )(page_table, *([pool] * group), qz, new_rows, us, selb)


def _win_cached_kernel(qz_ref, wc_ref, new_ref, gate_ref, o_ref, m_ref, l_ref, acc_ref, *, tq, wlen):
    _pair_init(m_ref, l_ref, acc_ref)
    qi = lax.broadcasted_iota(jnp.int32, (tq, wlen), 0)
    kj = lax.broadcasted_iota(jnp.int32, (tq, wlen), 1)
    bias_c = jnp.where((wlen - kj + qi) < WINDOW, 0.0, MASKV)
    bias_n = _causal_bias(tq)
    kv_c = wc_ref[0, 0].astype(BF16)
    kv_n = new_ref[...].astype(BF16)
    for pr in range(N_PAIR):
        q = _pair_queries(qz_ref, pr, tq)
        _pair_update(pr, q, kv_c, bias_c, bias_c, m_ref, l_ref, acc_ref, feature_major=True)
        _pair_update(pr, q, kv_n, bias_n, bias_n, m_ref, l_ref, acc_ref)
    _pair_finish(o_ref, _sigmoid(gate_ref[...]), 2, tq, l_ref, acc_ref)


def _win_cached(qz, wcache, layer, new_rows, us, tq):
    bsz, wlen = wcache.shape[1], wcache.shape[3]
    return pl.pallas_call(
        functools.partial(_win_cached_kernel, tq=tq, wlen=wlen),
        grid=(bsz,),
        in_specs=[pl.BlockSpec((N_KV, GQA, tq, LANES), lambda b: (0, 0, b, 0)),
                  pl.BlockSpec((1, 1, 2 * KV_DIM, wlen), lambda b: (layer, b, 0, 0)),
                  pl.BlockSpec((tq, 2 * KV_DIM), lambda b: (b, 0)),
                  pl.BlockSpec((tq, LANES), lambda b: (b, 0))],
        out_specs=pl.BlockSpec((tq, D_MODEL), lambda b: (b, 0)),
        out_shape=jax.ShapeDtypeStruct((bsz * tq, D_MODEL), F32),
        scratch_shapes=_pair_scratch(tq),
        compiler_params=_cparams(("parallel",)),
        name="win_cached",
    )(qz, wcache, new_rows, us)


def _merge_kernel(x_ref, yn_ref, ysc_ref, oc_ref, os_ref, ow_ref, g1_ref, g2_ref, g3_ref,
                  wssd_ref, wsc_ref, wnsa_ref, wout_ref, o_ref):
    y_ssd = jnp.dot(yn_ref[...].astype(BF16), wssd_ref[...], preferred_element_type=F32)
    y_sc = jnp.dot(ysc_ref[...].astype(BF16), wsc_ref[...], preferred_element_type=F32)
    o = (oc_ref[...].astype(F32) + os_ref[...].astype(F32) + ow_ref[...].astype(F32)).astype(BF16)
    y_nsa = jnp.dot(o, wnsa_ref[...], preferred_element_type=F32)
    mix = _sigmoid(g1_ref[...]) * y_ssd + _sigmoid(g2_ref[...]) * y_sc + _sigmoid(g3_ref[...]) * y_nsa
    o_ref[...] = x_ref[...] + jnp.dot(mix.astype(BF16), wout_ref[...], preferred_element_type=F32)


def _merge(x, yn, ysc, oc, osel, ow, u, w, tm):
    m = x.shape[0]
    row = lambda c: pl.BlockSpec((tm, D_MODEL), lambda i: (i, c))
    wspec = pl.BlockSpec((D_MODEL, D_MODEL), lambda i: (0, 0))
    gcol = U_G // D_MODEL
    return pl.pallas_call(
        _merge_kernel,
        grid=(m // tm,),
        in_specs=[row(0)] * 6 + [row(gcol), row(gcol + 1), row(gcol + 2)] + [wspec] * 4,
        out_specs=row(0),
        out_shape=jax.ShapeDtypeStruct((m, D_MODEL), F32),
        compiler_params=_cparams(("parallel",)),
        name="merge",
    )(x, yn, ysc, oc, osel, ow, u, u, u, w["w_ssd_out"], w["w_sconv_out"], w["w_nsa_out"], w["w_out"])


def _mlp_kernel(x_ref, g_ref, wup_ref, wdn_ref, o_ref, h_ref, acc_ref, *, nf):
    j = pl.program_id(1)

    @pl.when(j == 0)
    def _():
        x = x_ref[...]
        ms = jnp.mean(x * x, axis=-1, keepdims=True)
        h_ref[...] = (x * lax.rsqrt(ms + EPS) * g_ref[...]).astype(h_ref.dtype)
        acc_ref[...] = jnp.zeros(acc_ref.shape, F32)

    up = jnp.dot(h_ref[...], wup_ref[...], preferred_element_type=F32)
    a = jnp.square(jnp.maximum(up, 0.0)).astype(BF16)
    acc_ref[...] += jnp.dot(a, wdn_ref[...], preferred_element_type=F32)

    @pl.when(j == nf - 1)
    def _():
        o_ref[...] = x_ref[...] + acc_ref[...]


def _mlp(x, g, wup, wdn, tm, tf):
    m = x.shape[0]
    nf = D_FF // tf
    return pl.pallas_call(
        functools.partial(_mlp_kernel, nf=nf),
        grid=(m // tm, nf),
        in_specs=[pl.BlockSpec((tm, D_MODEL), lambda i, j: (i, 0)),
                  pl.BlockSpec((1, D_MODEL), lambda i, j: (0, 0)),
                  pl.BlockSpec((D_MODEL, tf), lambda i, j: (0, j)),
                  pl.BlockSpec((tf, D_MODEL), lambda i, j: (j, 0))],
        out_specs=pl.BlockSpec((tm, D_MODEL), lambda i, j: (i, 0)),
        out_shape=jax.ShapeDtypeStruct((m, D_MODEL), F32),
        scratch_shapes=[pltpu.VMEM((tm, D_MODEL), BF16), pltpu.VMEM((tm, D_MODEL), F32)],
        compiler_params=_cparams(("parallel", "arbitrary")),
        name="mlp",
    )(x, g, wup, wdn)


def _pad_lanes(v, width=LANES):
    v = v.reshape(1, -1).astype(F32)
    return jnp.pad(v, ((0, 0), (0, width - v.shape[1])))


def _pad_rows(v, rows=SUBLANES, front=False):
    pad = rows - v.shape[-2]
    cfg = [(0, 0)] * (v.ndim - 2) + [((pad, 0) if front else (0, pad)), (0, 0)]
    return jnp.pad(v, cfg)


def _layer_weights(l, w_in, norm_mix, norm_mlp, ssd_conv_w, ssd_conv_b, ssd_dt_bias, ssd_a_log, ssd_d, ssd_norm,
                   w_ssd_out, sconv_w, w_sconv_out, cmp_pe, cmp_w1, cmp_w2, w_nsa_out, w_out, w_mlp_up, w_mlp_down):
    b = np.cumsum((0,) + IN_SIZES)
    wi = w_in[l]
    seg = lambda k: wi[:, b[k]:b[k + 1]]
    w_main = jnp.concatenate([seg(1), seg(7), seg(0), seg(3), seg(4), seg(5), seg(6), seg(9)], axis=1).astype(BF16)
    w_small = jnp.concatenate([seg(2), seg(8)], axis=1)
    w_small = jnp.pad(w_small, ((0, 0), (0, LANES - w_small.shape[1]))).astype(BF16)
    return dict(
        w_main=w_main, w_small=w_small,
        norm_mix=norm_mix[l].reshape(1, -1), norm_mlp=norm_mlp[l].reshape(1, -1),
        convw=_pad_rows(ssd_conv_w[l]), convb=ssd_conv_b[l].reshape(1, -1),
        dtb=_pad_lanes(ssd_dt_bias[l]), alog=_pad_lanes(ssd_a_log[l]), dvec=_pad_lanes(ssd_d[l]),
        ssdn=ssd_norm[l].reshape(1, -1), scw=_pad_rows(sconv_w[l]),
        w_ssd_out=w_ssd_out[l].astype(BF16), w_sconv_out=w_sconv_out[l].astype(BF16),
        w_nsa_out=w_nsa_out[l].astype(BF16), w_out=w_out[l].astype(BF16),
        cmp_pe=cmp_pe[l].reshape(2, 2, CMP_STRIDE, HEAD_DIM).transpose(0, 2, 1, 3).reshape(2, CMP_STRIDE, LANES),
        cmp_w1=cmp_w1[l].reshape(2, 2, CMP_STRIDE, HEAD_DIM, CMP_HID).transpose(0, 2, 1, 3, 4).reshape(
            2, CMP_STRIDE, LANES, CMP_HID).astype(BF16),
        cmp_w2=cmp_w2[l].astype(BF16),
        w_mlp_up=w_mlp_up[l].astype(BF16), w_mlp_down=w_mlp_down[l].astype(BF16),
    )


def _rope_tables(pos):
    half = HEAD_DIM // 2
    inv = ROPE_THETA ** (-jnp.arange(half, dtype=F32) / half)
    ang = pos.astype(F32)[:, None] * inv[None, :]
    cos, sin = jnp.cos(ang), jnp.sin(ang)
    cos_t = jnp.concatenate([cos, cos, cos, cos], axis=1)
    sin_t = jnp.concatenate([-sin, sin, -sin, sin], axis=1)
    return cos_t, sin_t


def _overlap(ncmp_pad, ncmp):
    c0 = np.arange(ncmp_pad)[:, None] * CMP_STRIDE
    s0 = np.arange(LANES)[None, :] * SEL_BLK
    ov = np.maximum(np.minimum(c0 + CMP_LEN, s0 + SEL_BLK) - np.maximum(c0, s0), 0).astype(np.float32) / CMP_LEN
    ov[ncmp:] = 0.0
    return jnp.asarray(ov)


def _tile(m, pref):
    t = pref
    while m % t:
        t //= 2
    return t


def _front(x2d, w, tm):
    u = _norm_matmul(x2d, w["norm_mix"], w["w_main"], tm, N_MAIN // 8)
    us = _norm_matmul(x2d, w["norm_mix"], w["w_small"], tm, LANES)
    return u, us


def _prompt_layer(x2d, w, bsz, t, tabs):
    m = bsz * t
    u, us = _front(x2d, w, _tile(m, 1024))
    cos_t, sin_t, ov = tabs
    qz, cmp_b, sel_b, win_b, cmp_t, sel_t, win_t = _rope(u, cos_t, sin_t, _tile(t, 512), bsz, t, True)
    zeros = lambda *s: jnp.zeros(s, F32)
    yn, ysc, h_new, ch_last = _seqmix(u, us, zeros(bsz, SUBLANES, CONV_DIM), zeros(bsz, SUBLANES, D_MODEL),
                                      zeros(bsz, D_MODEL, SSD_STATE), w, bsz, t)
    n_pg = t // CMP_PAGE
    kc, vc = _compress(cmp_b.reshape(1, bsz * n_pg, CMP_PAGE, 2 * KV_DIM), 0,
                       jnp.arange(bsz * n_pg, dtype=jnp.int32).reshape(bsz, n_pg),
                       w["cmp_pe"], w["cmp_w1"], w["cmp_w2"], False)
    tq = _tile(t, 256)
    oc, selb = _cmp_attn(qz, kc, vc, us, ov, bsz, t, tq, 0, N_TOP - N_FORCED, None)
    ta = _tile(t, 512)
    osel = _band_attn(qz, sel_b, us, selb, bsz, t, ta, ta, True)
    ow = _band_attn(qz, win_b, us, None, bsz, t, ta, ta, False)
    x1 = _merge(x2d, yn, ysc, oc, osel, ow, u, w, _tile(m, 256))
    x2 = _mlp(x1, w["norm_mlp"], w["w_mlp_up"], w["w_mlp_down"], _tile(m, 512), 1024)
    wl = min(WINDOW, t)
    u3 = u.reshape(bsz, t, N_MAIN)
    rows_major = lambda a: a.reshape(bsz, 2, N_KV, HEAD_DIM, a.shape[-1]).transpose(0, 4, 1, 2, 3)
    state = (rows_major(cmp_t), rows_major(sel_t), rows_major(win_t[:, :, t - wl:]),
             h_new.reshape(bsz, H_SSD, SSD_HEAD_DIM, SSD_STATE),
             u3[:, t - (SSD_CONV - 1):, U_XBC:U_XBC + CONV_DIM],
             ch_last[:, SUBLANES - (SC_WIDTH - 1):])
    return x2, state


def _feature_major(cache):
    nd = cache.ndim
    perm = tuple(range(nd - 4)) + (nd - 3, nd - 2, nd - 1, nd - 4)
    return jnp.transpose(cache, perm).reshape(cache.shape[:-4] + (2 * KV_DIM, cache.shape[-4]))


def _sample_layer(x2d, w, bsz, t, tabs, layer, past):
    ssm0, conv0, sc0, cmp_pool, sel_pool, win_cache, win_rows_old, page_table = past
    m = bsz * t
    u, us = _front(x2d, w, m)
    cos_t, sin_t, ov = tabs
    qz, cmp_rows, sel_rows, win_rows = _rope(u, cos_t, sin_t, m, bsz, t, False)
    yn, ysc, h_new, ch_last = _seqmix(u, us, _pad_rows(conv0, front=True), _pad_rows(sc0, front=True),
                                      ssm0.reshape(bsz, D_MODEL, SSD_STATE), w, bsz, t)
    kc, vc = _compress(cmp_pool, layer, page_table, w["cmp_pe"], w["cmp_w1"], w["cmp_w2"], True)
    past_len = page_table.shape[1] * cmp_pool.shape[3]
    oc, selb = _cmp_attn(qz, kc, vc, us, ov, bsz, t, t, past_len, N_TOP - N_FORCED, past_len // SEL_BLK)
    osel = _sel_paged(qz, sel_pool, layer, page_table, sel_rows, us, selb, t)
    ow = _win_cached(qz, win_cache, layer, win_rows, us, t)
    x1 = _merge(x2d, yn, ysc, oc, osel, ow, u, w, m)
    x2 = _mlp(x1, w["norm_mlp"], w["w_mlp_up"], w["w_mlp_down"], m, 1024)
    wl = win_rows_old.shape[1]
    win_all = jnp.concatenate([win_rows_old, win_rows.reshape(bsz, t, 2, N_KV, HEAD_DIM)], axis=1)
    keep = min(WINDOW, wl + t)
    u3 = u.reshape(bsz, t, N_MAIN)
    state = (cmp_rows.reshape(bsz, t, 2, N_KV, HEAD_DIM), sel_rows.reshape(bsz, t, 2, N_KV, HEAD_DIM),
             win_all[:, wl + t - keep:],
             h_new.reshape(bsz, H_SSD, SSD_HEAD_DIM, SSD_STATE),
             u3[:, t - (SSD_CONV - 1):, U_XBC:U_XBC + CONV_DIM],
             ch_last[:, SUBLANES - (SC_WIDTH - 1):])
    return x2, state


def kernel(x_prompt, x_sample, cache_cmp_kv, cache_sel_kv, cache_win_kv, state_ssm, state_ssd_conv, state_sconv,
           page_table, norm_mix, norm_mlp, norm_final, w_in, ssd_conv_w, ssd_conv_b, ssd_dt_bias, ssd_a_log, ssd_d,
           ssd_norm, w_ssd_out, sconv_w, w_sconv_out, cmp_pe, cmp_w1, cmp_w2, w_nsa_out, w_out, w_mlp_up,
           w_mlp_down):
    depth = w_in.shape[0]
    bp, tp, _ = x_prompt.shape
    bs, ts, _ = x_sample.shape
    n_pool, page = cache_cmp_kv.shape[1:3]
    past_len = page_table.shape[1] * page
    wl = cache_win_kv.shape[2]
    for rows in (tp, past_len):
        assert rows % (LANES * CMP_STRIDE) == 0 and (rows // CMP_STRIDE) % min(CMP_TILE, rows // CMP_STRIDE) == 0
        assert rows // SEL_BLK <= LANES
    assert ts == SUBLANES
    assert wl == WINDOW and past_len >= WINDOW
    assert page_table.shape[1] % SEL_PAGE_GROUP == 0 and (tp // CMP_PAGE) % PAGE_GROUP == 0
    assert (past_len + ts - CMP_LEN) // CMP_STRIDE + 1 == (past_len - CMP_LEN) // CMP_STRIDE + 1

    tabs_p = _rope_tables(jnp.tile(jnp.arange(tp), bp)) + (_overlap(tp // CMP_STRIDE, (tp - CMP_LEN) // CMP_STRIDE + 1),)
    tabs_s = _rope_tables(jnp.tile(past_len + jnp.arange(ts), bs)) + (
        _overlap(past_len // CMP_STRIDE, (past_len - CMP_LEN) // CMP_STRIDE + 1),)

    cmp_pool, sel_pool, win_cache = _feature_major(cache_cmp_kv), _feature_major(cache_sel_kv), _feature_major(cache_win_kv)
    xp = x_prompt.reshape(bp * tp, D_MODEL)
    xs = x_sample.reshape(bs * ts, D_MODEL)
    p_new = [[] for _ in range(6)]
    s_new = [[] for _ in range(6)]
    for l in range(depth):
        w = _layer_weights(l, w_in, norm_mix, norm_mlp, ssd_conv_w, ssd_conv_b, ssd_dt_bias, ssd_a_log, ssd_d,
                           ssd_norm, w_ssd_out, sconv_w, w_sconv_out, cmp_pe, cmp_w1, cmp_w2, w_nsa_out, w_out,
                           w_mlp_up, w_mlp_down)
        xp, st_p = _prompt_layer(xp, w, bp, tp, tabs_p)
        past = (state_ssm[l], state_ssd_conv[l], state_sconv[l], cmp_pool, sel_pool, win_cache, cache_win_kv[l],
                page_table)
        xs, st_s = _sample_layer(xs, w, bs, ts, tabs_s, l, past)
        for i in range(6):
            p_new[i].append(st_p[i])
            s_new[i].append(st_s[i])
    gf = norm_final.reshape(1, -1)
    y_prompt = _final_norm(xp, gf, _tile(bp * tp, 1024)).reshape(bp, tp, D_MODEL)
    y_sample = _final_norm(xs, gf, bs * ts).reshape(bs, ts, D_MODEL)
    return (y_prompt, y_sample) + tuple(jnp.stack(a) for a in p_new) + tuple(jnp.stack(a) for a in s_new)
```

```python
import functools
import math

import numpy as np
import jax
import jax.numpy as jnp
from jax import lax
from jax.experimental import pallas as pl
from jax.experimental.pallas import tpu as pltpu

F32 = jnp.float32
BF16 = jnp.bfloat16

D_MODEL = 1024
SSD_HEAD_DIM = 64
H_SSD = D_MODEL // SSD_HEAD_DIM
SSD_STATE = 128
SSD_GROUPS = 2
SSD_CONV = 4
SSD_CHUNK = 128
CONV_DIM = D_MODEL + 2 * SSD_GROUPS * SSD_STATE
SC_WIDTH = 3
HEAD_DIM = 64
H_ATT = D_MODEL // HEAD_DIM
N_KV = 4
GQA = H_ATT // N_KV
KV_DIM = N_KV * HEAD_DIM
CMP_LEN = 32
CMP_STRIDE = 16
CMP_HID = 4 * HEAD_DIM
SEL_BLK = 64
N_TOP = 16
WINDOW = 512
D_FF = 4 * D_MODEL
ROPE_THETA = 10000.0
EPS = 1e-6
NEG_INF = -1e30
BIG = 1e9
TAKEN = -3e38
N_FORCED = 3
LOG2E = 1.4426950408889634
MASKV = -1e30
M_INIT = -1e29
IN_SIZES = (D_MODEL, CONV_DIM, H_SSD, D_MODEL, D_MODEL, D_MODEL, D_MODEL, 6 * KV_DIM, 3 * H_ATT, 3 * D_MODEL)

LANES = 128
SUBLANES = 8
VMEM_LIMIT = 56 * 1024 * 1024

U_XBC, U_KV, U_Z, U_SCB, U_SCC, U_SCH, U_Q, U_G = 0, 1536, 3072, 4096, 5120, 6144, 7168, 8192
N_MAIN = 11264
GATE_COL0 = H_SSD

_NT = (((1,), (1,)), ((), ()))
_TN = (((0,), (0,)), ((), ()))


def _cparams(sem):
    return pltpu.CompilerParams(dimension_semantics=sem, vmem_limit_bytes=VMEM_LIMIT)


def _silu(x):
    return x * (1.0 / (1.0 + jnp.exp(-x)))


def _sigmoid(x):
    return 1.0 / (1.0 + jnp.exp(-x))


def _norm_matmul_kernel(x_ref, g_ref, w_ref, o_ref, h_ref):
    @pl.when(pl.program_id(1) == 0)
    def _():
        x = x_ref[...]
        ms = jnp.mean(x * x, axis=-1, keepdims=True)
        h_ref[...] = (x * lax.rsqrt(ms + EPS) * g_ref[...]).astype(h_ref.dtype)

    o_ref[...] = jnp.dot(h_ref[...], w_ref[...], preferred_element_type=F32).astype(o_ref.dtype)


def _norm_matmul(x, g, w, tm, tn):
    m, d = x.shape
    n = w.shape[1]
    return pl.pallas_call(
        _norm_matmul_kernel,
        grid=(m // tm, n // tn),
        in_specs=[pl.BlockSpec((tm, d), lambda i, j: (i, 0)),
                  pl.BlockSpec((1, d), lambda i, j: (0, 0)),
                  pl.BlockSpec((d, tn), lambda i, j: (0, j))],
        out_specs=pl.BlockSpec((tm, tn), lambda i, j: (i, j)),
        out_shape=jax.ShapeDtypeStruct((m, n), F32),
        scratch_shapes=[pltpu.VMEM((tm, d), BF16)],
        compiler_params=_cparams(("parallel", "arbitrary")),
        name="norm_matmul",
    )(x, g, w)


def _final_norm_kernel(x_ref, g_ref, o_ref):
    x = x_ref[...]
    ms = jnp.mean(x * x, axis=-1, keepdims=True)
    o_ref[...] = x * lax.rsqrt(ms + EPS) * g_ref[...]


def _final_norm(x, g, tm):
    m, d = x.shape
    return pl.pallas_call(
        _final_norm_kernel,
        grid=(m // tm,),
        in_specs=[pl.BlockSpec((tm, d), lambda i: (i, 0)), pl.BlockSpec((1, d), lambda i: (0, 0))],
        out_specs=pl.BlockSpec((tm, d), lambda i: (i, 0)),
        out_shape=jax.ShapeDtypeStruct((m, d), F32),
        compiler_params=_cparams(("parallel",)),
        name="final_norm",
    )(x, g)


def _rot_half(x, first_half):
    return jnp.where(first_half, pltpu.roll(x, LANES - HEAD_DIM // 2, axis=1), pltpu.roll(x, HEAD_DIM // 2, axis=1))


def _rope_kernel(q_ref, kv_ref, cos_ref, sin_ref, qz_ref, *out_refs, scale, feature_major):
    row_refs, t_refs = out_refs[:3], out_refs[3:]
    cos = cos_ref[...]
    sin = sin_ref[...]
    lane = lax.broadcasted_iota(jnp.int32, cos.shape, 1)
    first_half = (lane % HEAD_DIM) < (HEAD_DIM // 2)
    left = lane < HEAD_DIM

    def rope(x):
        return x * cos + _rot_half(x, first_half) * sin

    for c in range(D_MODEL // LANES):
        r = rope(q_ref[:, c * LANES:(c + 1) * LANES]) * scale
        sw = pltpu.roll(r, HEAD_DIM, axis=1)
        h = (2 * c) // GQA
        g = (2 * c) % GQA
        if h % 2 == 0:
            qz_ref[h, g] = jnp.where(left, r, 0.0).astype(qz_ref.dtype)
            qz_ref[h, g + 1] = jnp.where(left, sw, 0.0).astype(qz_ref.dtype)
        else:
            qz_ref[h, g] = jnp.where(left, 0.0, sw).astype(qz_ref.dtype)
            qz_ref[h, g + 1] = jnp.where(left, 0.0, r).astype(qz_ref.dtype)
    for br in range(3):
        base = br * 2 * KV_DIM
        for c in range(2 * KV_DIM // LANES):
            x = kv_ref[:, base + c * LANES: base + (c + 1) * LANES]
            if c < KV_DIM // LANES:
                x = rope(x)
            row_refs[br][:, c * LANES:(c + 1) * LANES] = x.astype(row_refs[br].dtype)
            if feature_major:
                t_refs[br][0, c * LANES:(c + 1) * LANES, :] = x.T


def _rope(u, cos, sin, tm, bsz, t, prompt):
    m = u.shape[0]
    nt = t // tm if prompt else 1
    act = BF16 if prompt else F32
    kv_spec = pl.BlockSpec((tm, 2 * KV_DIM), lambda i: (i, 0))
    out_specs = [pl.BlockSpec((N_KV, GQA, tm, LANES), lambda i: (0, 0, i, 0)), kv_spec, kv_spec, kv_spec]
    out_shape = [jax.ShapeDtypeStruct((N_KV, GQA, m, LANES), act)] + [jax.ShapeDtypeStruct((m, 2 * KV_DIM), act)] * 3
    if prompt:
        out_specs += [pl.BlockSpec((1, 2 * KV_DIM, tm), lambda i: (i // nt, 0, i % nt))] * 3
        out_shape += [jax.ShapeDtypeStruct((bsz, 2 * KV_DIM, t), F32)] * 3
    return pl.pallas_call(
        functools.partial(_rope_kernel, scale=HEAD_DIM ** -0.5 * LOG2E, feature_major=prompt),
        grid=(m // tm,),
        in_specs=[pl.BlockSpec((tm, D_MODEL), lambda i: (i, U_Q // D_MODEL)),
                  pl.BlockSpec((tm, 6 * KV_DIM), lambda i: (i, U_KV // (6 * KV_DIM))),
                  pl.BlockSpec((tm, LANES), lambda i: (i, 0)),
                  pl.BlockSpec((tm, LANES), lambda i: (i, 0))],
        out_specs=out_specs,
        out_shape=out_shape,
        compiler_params=_cparams(("parallel",)),
        name="rope",
    )(u, u, cos, sin)


def _seqmix_kernel(xbc_ref, z_ref, scb_ref, scc_ref, sch_ref, dtr_ref, conv0_ref, sc0_ref, h0_ref,
                   convw_ref, convb_ref, dtb_ref, alog_ref, dvec_ref, ssdn_ref, scw_ref,
                   yn_ref, ysc_ref, hout_ref, chlast_ref,
                   ext_ref, chext_ref, h_ref, y_ref, *, chunk, valid):
    c = pl.program_id(1)
    L = chunk
    P = SSD_HEAD_DIM
    N = SSD_STATE

    @pl.when(c == 0)
    def _():
        ext_ref[0:SUBLANES, :] = conv0_ref[0]
        chext_ref[0:SUBLANES, :] = sc0_ref[0]
        h_ref[...] = h0_ref[0]

    def load(ref):
        x = ref[...]
        if valid < L:
            x = jnp.concatenate([x, jnp.zeros((L - valid, x.shape[1]), x.dtype)], axis=0)
        return x

    ext_ref[SUBLANES:SUBLANES + L, :] = load(xbc_ref)
    conv = jnp.broadcast_to(convb_ref[...], (L, CONV_DIM))
    for k in range(SSD_CONV):
        off = SUBLANES - (SSD_CONV - 1) + k
        conv = conv + convw_ref[k:k + 1, :] * ext_ref[off:off + L, :]
    ext_ref[0:SUBLANES, :] = ext_ref[L:L + SUBLANES, :]
    xbc = _silu(conv)

    ch = load(scc_ref) * load(sch_ref)
    chext_ref[SUBLANES:SUBLANES + L, :] = ch
    cv = jnp.zeros((L, D_MODEL), F32)
    for k in range(SC_WIDTH):
        off = SUBLANES - (SC_WIDTH - 1) + k
        cv = cv + scw_ref[k:k + 1, :] * chext_ref[off:off + L, :]
    ysc_ref[...] = (load(scb_ref) * cv)[:valid].astype(ysc_ref.dtype)
    chlast_ref[0] = chext_ref[valid:valid + SUBLANES, :]
    chext_ref[0:SUBLANES, :] = chext_ref[L:L + SUBLANES, :]

    x = load(dtr_ref) + dtb_ref[...]
    dt = jnp.maximum(x, 0.0) + jnp.log1p(jnp.exp(-jnp.abs(x)))
    row = lax.broadcasted_iota(jnp.int32, (L, LANES), 0)
    if valid < L:
        dt = jnp.where(row < valid, dt, 0.0)
    a = -jnp.exp(alog_ref[...])
    da = dt * a
    ri = lax.broadcasted_iota(jnp.int32, (L, L), 0)
    ci = lax.broadcasted_iota(jnp.int32, (L, L), 1)
    tril = ri >= ci
    acum = jnp.dot(tril.astype(F32), da, precision=lax.Precision.HIGHEST, preferred_element_type=F32)
    acum_t = acum.T
    a_last = acum[L - 1:L, :]

    for g in range(SSD_GROUPS):
        bg = xbc[:, D_MODEL + g * N: D_MODEL + (g + 1) * N]
        cg = xbc[:, D_MODEL + SSD_GROUPS * N + g * N: D_MODEL + SSD_GROUPS * N + (g + 1) * N]
        cb = lax.dot_general(cg, bg, _NT, preferred_element_type=F32)
        for r in range(H_SSD // SSD_GROUPS):
            h = g * (H_SSD // SSD_GROUPS) + r
            a_col = acum[:, h:h + 1]
            a_row = acum_t[h:h + 1, :]
            seg = jnp.where(tril, a_col - a_row, NEG_INF)
            wts = jnp.exp(seg) * cb
            xs_h = xbc[:, h * P:(h + 1) * P]
            xdt = xs_h * dt[:, h:h + 1]
            hprev = h_ref[h * P:(h + 1) * P, :]
            y_h = jnp.dot(wts, xdt, preferred_element_type=F32)
            y_in = lax.dot_general(cg, hprev, _NT, preferred_element_type=F32)
            y_h = y_h + y_in * jnp.exp(a_col)
            y_h = y_h + dvec_ref[:, h:h + 1] * xs_h
            y_ref[:, h * P:(h + 1) * P] = y_h
            al = a_last[:, h:h + 1]
            xw = xdt * jnp.exp(al - a_col)
            s_h = lax.dot_general(xw, bg, _TN, preferred_element_type=F32)
            h_ref[h * P:(h + 1) * P, :] = jnp.exp(al) * hprev + s_h

    y = y_ref[0:valid, :] * _silu(z_ref[...])
    ms = jnp.mean(y * y, axis=-1, keepdims=True)
    yn_ref[...] = (y * lax.rsqrt(ms + EPS) * ssdn_ref[...]).astype(yn_ref.dtype)
    hout_ref[0] = h_ref[...]


def _seqmix(u, us, conv0, sc0, h0, prm, bsz, t):
    L = SSD_CHUNK
    valid = min(t, L)
    nc = t // valid
    m = bsz * t
    act = _act_dtype(valid)
    row = lambda b, c: b * nc + c
    spec_u = lambda w, col: pl.BlockSpec((valid, w), lambda b, c: (row(b, c), col // w))
    full = lambda shape: pl.BlockSpec(shape, lambda b, c: (0,) * len(shape))
    perb = lambda shape: pl.BlockSpec((1,) + shape, lambda b, c: (b,) + (0,) * len(shape))
    return pl.pallas_call(
        functools.partial(_seqmix_kernel, chunk=L, valid=valid),
        grid=(bsz, nc),
        in_specs=[spec_u(CONV_DIM, U_XBC), spec_u(D_MODEL, U_Z), spec_u(D_MODEL, U_SCB), spec_u(D_MODEL, U_SCC),
                  spec_u(D_MODEL, U_SCH),
                  pl.BlockSpec((valid, LANES), lambda b, c: (row(b, c), 0)),
                  perb((SUBLANES, CONV_DIM)), perb((SUBLANES, D_MODEL)), perb((D_MODEL, SSD_STATE)),
                  full((SUBLANES, CONV_DIM)), full((1, CONV_DIM)), full((1, LANES)), full((1, LANES)),
                  full((1, LANES)), full((1, D_MODEL)), full((SUBLANES, D_MODEL))],
        out_specs=[pl.BlockSpec((valid, D_MODEL), lambda b, c: (row(b, c), 0)),
                   pl.BlockSpec((valid, D_MODEL), lambda b, c: (row(b, c), 0)),
                   perb((D_MODEL, SSD_STATE)), perb((SUBLANES, D_MODEL))],
        out_shape=[jax.ShapeDtypeStruct((m, D_MODEL), act), jax.ShapeDtypeStruct((m, D_MODEL), act),
                   jax.ShapeDtypeStruct((bsz, D_MODEL, SSD_STATE), F32),
                   jax.ShapeDtypeStruct((bsz, SUBLANES, D_MODEL), F32)],
        scratch_shapes=[pltpu.VMEM((L + SUBLANES, CONV_DIM), F32), pltpu.VMEM((L + SUBLANES, D_MODEL), F32),
                        pltpu.VMEM((D_MODEL, SSD_STATE), F32), pltpu.VMEM((L, D_MODEL), F32)],
        compiler_params=_cparams(("parallel", "arbitrary")),
        name="seqmix",
    )(u, u, u, u, u, us, conv0, sc0, h0, prm["convw"], prm["convb"], prm["dtb"], prm["alog"], prm["dvec"],
      prm["ssdn"], prm["scw"])


CMP_TILE = 512
CMP_PAGE = 128
PAGE_GROUP = 8
SEL_PAGE_GROUP = 16


N_COMBO = 2 * N_KV


def _compress_fill(z_ref, page, g0, feature_major):
    R = page.shape[1] if feature_major else page.shape[0]
    lo = pl.ds(g0 + CMP_STRIDE, R)
    hi = pl.ds(g0, R)
    for pr in range(N_COMBO // 2):
        if feature_major:
            x2 = page[pr * LANES:(pr + 1) * LANES, :].T
        else:
            x2 = page[:, pr * LANES:(pr + 1) * LANES].astype(F32)
        sw = pltpu.roll(x2, HEAD_DIM, axis=1)
        z_ref[2 * pr, lo, 0:HEAD_DIM] = x2[:, 0:HEAD_DIM]
        z_ref[2 * pr, hi, HEAD_DIM:LANES] = sw[:, HEAD_DIM:LANES]
        z_ref[2 * pr + 1, lo, 0:HEAD_DIM] = sw[:, 0:HEAD_DIM]
        z_ref[2 * pr + 1, hi, HEAD_DIM:LANES] = x2[:, HEAD_DIM:LANES]


def _compress_tile(z_ref, row0, pe_ref, w1_ref, w2_ref, out_refs, out_row0, tile):
    for c in range(N_COMBO):
        kv, h = divmod(c, N_KV)
        acc = jnp.zeros((tile, CMP_HID), F32)
        for r in range(CMP_STRIDE):
            zr = z_ref[c, pl.ds(row0 + CMP_STRIDE + r, tile, stride=CMP_STRIDE), :] + pe_ref[kv, r:r + 1, :]
            acc = acc + jnp.dot(zr.astype(BF16), w1_ref[kv, r], preferred_element_type=F32)
        o = jnp.dot(_silu(acc).astype(BF16), w2_ref[kv], preferred_element_type=F32)
        out_refs[kv][0, pl.ds(out_row0, tile), pl.ds(h * HEAD_DIM, HEAD_DIM)] = o.astype(out_refs[kv].dtype)


def _compress_kernel(pt_ref, *refs, n_pages, page, group, feature_major):
    page_refs = refs[:group]
    pe_ref, w1_ref, w2_ref, kc_ref, vc_ref, z_ref = refs[group:]
    p = pl.program_id(1)
    for k in range(group):
        _compress_fill(z_ref, page_refs[k][0, 0], pl.multiple_of((p * group + k) * page, page), feature_major)

    @pl.when(p == n_pages // group - 1)
    def _():
        rows = n_pages * page
        z_ref[:, rows:rows + CMP_STRIDE, HEAD_DIM:LANES] = jnp.zeros((N_COMBO, CMP_STRIDE, LANES - HEAD_DIM), F32)

        tile = min(CMP_TILE, rows // CMP_STRIDE)

        def body(j, carry):
            _compress_tile(z_ref, pl.multiple_of(j * tile * CMP_STRIDE, tile * CMP_STRIDE),
                           pe_ref, w1_ref, w2_ref, (kc_ref, vc_ref), pl.multiple_of(j * tile, tile), tile)
            return carry

        lax.fori_loop(0, rows // (tile * CMP_STRIDE), body, 0)


def _page_shape(pool, feature_major):
    return (1, 1) + pool.shape[2:], pool.shape[3] if feature_major else pool.shape[2]


def _compress(pool, layer, page_table, pe2, w1p, w2, feature_major):
    bsz, n_pages = page_table.shape
    blk, page = _page_shape(pool, feature_major)
    rows = n_pages * page
    group = PAGE_GROUP
    out = jax.ShapeDtypeStruct((bsz, rows // CMP_STRIDE, KV_DIM), BF16)
    full = lambda shape: pl.BlockSpec(shape, lambda b, p, pt: (0,) * len(shape))
    page_spec = lambda k: pl.BlockSpec(blk, lambda b, p, pt: (layer, pt[b, p * group + k], 0, 0))
    grid_spec = pltpu.PrefetchScalarGridSpec(
        num_scalar_prefetch=1,
        grid=(bsz, n_pages // group),
        in_specs=[page_spec(k) for k in range(group)] + [full(pe2.shape), full(w1p.shape), full(w2.shape)],
        out_specs=[pl.BlockSpec((1, rows // CMP_STRIDE, KV_DIM), lambda b, p, pt: (b, 0, 0)),
                   pl.BlockSpec((1, rows // CMP_STRIDE, KV_DIM), lambda b, p, pt: (b, 0, 0))],
        scratch_shapes=[pltpu.VMEM((N_COMBO, rows + CMP_STRIDE, LANES), F32)],
    )
    return pl.pallas_call(
        functools.partial(_compress_kernel, n_pages=n_pages, page=page, group=group, feature_major=feature_major),
        grid_spec=grid_spec,
        out_shape=[out, out],
        compiler_params=_cparams(("parallel", "arbitrary")),
        name="compress",
    )(page_table, *([pool] * group), pe2, w1p, w2)


BF16_ROWS = 16


def _act_dtype(block_rows):
    return BF16 if block_rows % BF16_ROWS == 0 else F32


def _left_lanes(shape):
    return lax.broadcasted_iota(jnp.int32, shape, len(shape) - 1) < HEAD_DIM


def _own_half(left, h):
    return left if h % 2 == 0 else jnp.logical_not(left)


def _value_with_ones(v2, own):
    return jnp.where(own, v2, jnp.ones_like(v2))


def _normalise(acc, own):
    l = jnp.where(own, pltpu.roll(acc, HEAD_DIM, axis=1), acc)
    rinv = jnp.where(l > 0.0, 1.0 / l, 0.0)
    return acc * rinv, rinv


def _emit_heads(o_ref, pieces, halves):
    left = _left_lanes(pieces[0].shape)
    for c in range(H_ATT // 2):
        a, b = pieces[2 * c], pieces[2 * c + 1]
        a = a if halves[2 * c] == 0 else pltpu.roll(a, HEAD_DIM, axis=1)
        b = b if halves[2 * c + 1] == 1 else pltpu.roll(b, HEAD_DIM, axis=1)
        o_ref[:, c * LANES:(c + 1) * LANES] = jnp.where(left, a, b).astype(o_ref.dtype)


def _gate_col(gate, hq, branch):
    col = GATE_COL0 + hq * 3 + branch
    return gate[:, col:col + 1]


def _topk_lanes(vals, n_iter):
    lane_f = lax.broadcasted_iota(jnp.int32, vals.shape, 1).astype(F32)

    def body(_, vals):
        mx = jnp.max(vals, axis=-1, keepdims=True)
        first = jnp.min(jnp.where(vals == mx, lane_f, float(LANES)), axis=-1, keepdims=True)
        return jnp.where(lane_f == first, TAKEN, vals)

    return jnp.where(lax.fori_loop(0, n_iter, body, vals) == TAKEN, 1.0, 0.0)


def _topk_rows(vals, n_iter):
    row_f = lax.broadcasted_iota(jnp.int32, vals.shape, 0).astype(F32)

    def body(_, vals):
        mx = jnp.max(vals, axis=0, keepdims=True)
        first = jnp.min(jnp.where(vals == mx, row_f, float(LANES)), axis=0, keepdims=True)
        return jnp.where(row_f == first, TAKEN, vals)

    return jnp.where(lax.fori_loop(0, n_iter, body, vals) == TAKEN, 1.0, 0.0)


def _split_bf16x3(x):
    hi = x.astype(BF16)
    r1 = x - hi.astype(F32)
    mid = r1.astype(BF16)
    lo = (r1 - mid.astype(F32)).astype(BF16)
    return hi, mid, lo


def _cmp_attn_kernel(qz_ref, kc_ref, vc_ref, gate_ref, ov_ref, o_ref, selb_ref, *, tq, pos0, n_iter, n_past_blk):
    i = pl.program_id(1)
    n_all = kc_ref.shape[1]
    q_last = pos0 + (i + 1) * tq - 1
    n_vis = jnp.maximum(q_last - (CMP_LEN - 1), 0) // CMP_STRIDE + 1
    steps = jnp.minimum((n_vis + LANES - 1) // LANES, n_all // LANES)
    for w in range(1, n_all // LANES + 1):
        @pl.when(steps == w)
        def _(w=w):
            _cmp_attn_body(qz_ref, kc_ref, vc_ref, gate_ref, ov_ref, o_ref, selb_ref, tq=tq, pos0=pos0, n_iter=n_iter,
                           n_past_blk=n_past_blk, ncmp=w * LANES)


def _cmp_attn_body(qz_ref, kc_ref, vc_ref, gate_ref, ov_ref, o_ref, selb_ref, *, tq, pos0, n_iter, n_past_blk, ncmp):
    i = pl.program_id(1)
    rows = GQA * tq
    ov_b = ov_ref[0:ncmp, :].astype(BF16)
    qpos = pos0 + i * tq + lax.broadcasted_iota(jnp.int32, (tq, 1), 0)
    cend = lax.broadcasted_iota(jnp.int32, (1, ncmp), 1) * CMP_STRIDE + (CMP_LEN - 1)
    bias = jnp.where(cend <= qpos, 0.0, MASKV)
    gate = _sigmoid(gate_ref[...])
    left = _left_lanes((rows, LANES))
    transposed = tq % LANES == 0
    shape = (LANES, tq) if transposed else (tq, LANES)
    blk = lax.broadcasted_iota(jnp.int32, shape, 0 if transposed else 1)
    if n_past_blk is None:
        cur = (pos0 + i * tq + lax.broadcasted_iota(jnp.int32, shape, 1 if transposed else 0)) // SEL_BLK
        forced = (blk == 0) | (blk == cur) | (blk == cur - 1)
        allowed = blk <= cur
    else:
        forced = (blk == 0) | (blk == n_past_blk - 1)
        allowed = blk < n_past_blk
    pieces, halves = [None] * H_ATT, [0] * H_ATT

    def scores(h):
        pr = h // 2
        q = qz_ref[h].reshape(rows, LANES).astype(BF16)
        s = lax.dot_general(q, kc_ref[0, 0:ncmp, pr * LANES:(pr + 1) * LANES], _NT, preferred_element_type=F32)
        return (s.reshape(GQA, tq, ncmp) + bias[None]).reshape(rows, ncmp)

    vals = []
    s = scores(0)
    for h in range(N_KV):
        s_next = scores(h + 1) if h + 1 < N_KV else None
        pr = h // 2
        own = _own_half(left, h)
        v2 = vc_ref[0, 0:ncmp, pr * LANES:(pr + 1) * LANES]
        m = jnp.maximum(jnp.max(s, axis=-1, keepdims=True), M_INIT)
        e = jnp.exp2(s - m)
        acc = jnp.dot(e.astype(BF16), _value_with_ones(v2, _own_half(_left_lanes(v2.shape), h)),
                      preferred_element_type=F32)
        o, rinv = _normalise(acc, own)
        for g in range(GQA):
            hq = h * GQA + g
            pieces[hq] = o[g * tq:(g + 1) * tq] * _gate_col(gate, hq, 0)
            halves[hq] = h % 2
        p = e * jnp.concatenate([rinv] * (ncmp // LANES), axis=1)
        psum = jnp.sum(p.reshape(GQA, tq, ncmp), axis=0)
        imp = sum(jnp.dot(part, ov_b, preferred_element_type=F32) for part in _split_bf16x3(psum))
        vals.append(jnp.where(forced, TAKEN, jnp.where(allowed, imp.T if transposed else imp, NEG_INF)))
        s = s_next
    _emit_heads(o_ref, pieces, halves)
    if transposed:
        sels = [_topk_rows(vals[h], n_iter).T for h in range(N_KV)]
    elif N_KV * tq <= LANES:
        stacked = jnp.concatenate(vals + [jnp.full((LANES - N_KV * tq, LANES), NEG_INF, F32)], axis=0)
        sel = _topk_rows(stacked.T, n_iter).T
        sels = [sel[h * tq:(h + 1) * tq] for h in range(N_KV)]
    else:
        sels = [_topk_lanes(vals[h], n_iter) for h in range(N_KV)]
    for h in range(N_KV):
        selb_ref[0, h] = jnp.where(sels[h] > 0.5, 0.0, MASKV).astype(selb_ref.dtype)


def _cmp_attn(qz, kc, vc, us, ov, bsz, t, tq, pos0, n_iter, n_past_blk):
    nq = t // tq
    ncmp = kc.shape[1]
    return pl.pallas_call(
        functools.partial(_cmp_attn_kernel, tq=tq, pos0=pos0, n_iter=n_iter, n_past_blk=n_past_blk),
        grid=(bsz, nq),
        in_specs=[pl.BlockSpec((N_KV, GQA, tq, LANES), lambda b, i: (0, 0, b * nq + i, 0)),
                  pl.BlockSpec((1, ncmp, KV_DIM), lambda b, i: (b, 0, 0)),
                  pl.BlockSpec((1, ncmp, KV_DIM), lambda b, i: (b, 0, 0)),
                  pl.BlockSpec((tq, LANES), lambda b, i: (b * nq + i, 0)),
                  pl.BlockSpec(ov.shape, lambda b, i: (0, 0))],
        out_specs=[pl.BlockSpec((tq, D_MODEL), lambda b, i: (b * nq + i, 0)),
                   pl.BlockSpec((1, N_KV, tq, LANES), lambda b, i: (b, 0, i, 0))],
        out_shape=[jax.ShapeDtypeStruct((bsz * t, D_MODEL), _act_dtype(tq)),
                   jax.ShapeDtypeStruct((bsz, N_KV, t, LANES), BF16)],
        compiler_params=_cparams(("parallel", "parallel")),
        name="cmp_attn",
    )(qz, kc, vc, us, ov)


def _block_expand(k0, tk):
    blk = lax.broadcasted_iota(jnp.int32, (LANES, tk), 0)
    col = lax.broadcasted_iota(jnp.int32, (LANES, tk), 1)
    return (blk == (k0 + col) // SEL_BLK).astype(BF16)


def _band_attn_kernel(qz_ref, kv_ref, gate_ref, *rest, tq, tk, nk, nback, branch, use_sel):
    if use_sel:
        selb_ref, o_ref, m_ref, acc_ref = rest
    else:
        o_ref, m_ref, acc_ref = rest
    i = pl.program_id(1)
    kk = pl.program_id(2)
    rows = GQA * tq

    @pl.when(kk == 0)
    def _():
        m_ref[...] = jnp.full(m_ref.shape, M_INIT, F32)
        acc_ref[...] = jnp.zeros(acc_ref.shape, F32)

    if use_sel:
        kt = kk
        live = kt * tk <= i * tq + tq - 1
    else:
        kt = i * (tq // tk) - nback + kk
        live = kt >= 0

    @pl.when(live)
    def _():
        k0 = kt * tk
        dlt = (i * tq + lax.broadcasted_iota(jnp.int32, (tq, tk), 0)) - (k0 + lax.broadcasted_iota(jnp.int32, (tq, tk), 1))
        if use_sel:
            pos_bias = jnp.where(dlt >= 0, 0.0, MASKV)
            expand = _block_expand(k0, tk)
        else:
            pos_bias = jnp.where((dlt >= 0) & (dlt < WINDOW), 0.0, MASKV)
        left_v = _left_lanes((tk, LANES))

        def scores(h):
            pr = (h // 2) * LANES
            bias = pos_bias
            if use_sel:
                bias = bias + jnp.dot(selb_ref[0, h], expand, preferred_element_type=F32)
            q = qz_ref[h].reshape(rows, LANES)
            s = lax.dot_general(q, kv_ref[:, pr:pr + LANES], _NT, preferred_element_type=F32)
            return (s.reshape(GQA, tq, tk) + bias[None]).reshape(rows, tk)

        s = scores(0)
        for h in range(N_KV):
            s_next = scores(h + 1) if h + 1 < N_KV else None
            pr = (h // 2) * LANES
            vaug = _value_with_ones(kv_ref[:, KV_DIM + pr: KV_DIM + pr + LANES], _own_half(left_v, h))
            m_old = m_ref[h]
            m_new = jnp.maximum(m_old, jnp.max(s, axis=-1, keepdims=True))
            e = jnp.exp2((s - m_new[:, 0:1]).astype(BF16))
            alpha = jnp.exp2(m_old - m_new)
            acc_ref[h] = alpha * acc_ref[h] + jnp.dot(e, vaug, preferred_element_type=F32)
            m_ref[h] = m_new
            s = s_next

    @pl.when(kk == nk - 1)
    def _():
        gate = _sigmoid(gate_ref[...])
        left = _left_lanes((rows, LANES))
        pieces, halves = [None] * H_ATT, [0] * H_ATT
        for h in range(N_KV):
            o, _ = _normalise(acc_ref[h], _own_half(left, h))
            for g in range(GQA):
                hq = h * GQA + g
                pieces[hq] = o[g * tq:(g + 1) * tq] * _gate_col(gate, hq, branch)
                halves[hq] = h % 2
        _emit_heads(o_ref, pieces, halves)


def _band_attn(qz, kvb, us, selb, bsz, t, tq, tk, use_sel):
    nq = t // tq
    nkt = t // tk
    if use_sel:
        nk, nback, branch = nkt, 0, 1
        kv_idx = lambda b, i, kk: (b * nkt + jnp.minimum(kk, (i * tq + tq - 1) // tk), 0)
    else:
        nback = -(-(WINDOW - 1) // tk)
        nk, branch = nback + tq // tk, 2
        kv_idx = lambda b, i, kk: (b * nkt + jnp.maximum(i * (tq // tk) - nback + kk, 0), 0)
    in_specs = [pl.BlockSpec((N_KV, GQA, tq, LANES), lambda b, i, kk: (0, 0, b * nq + i, 0)),
                pl.BlockSpec((tk, 2 * KV_DIM), kv_idx),
                pl.BlockSpec((tq, LANES), lambda b, i, kk: (b * nq + i, 0))]
    args = [qz, kvb, us]
    if use_sel:
        in_specs.append(pl.BlockSpec((1, N_KV, tq, LANES), lambda b, i, kk: (b, 0, i, 0)))
        args.append(selb)
    return pl.pallas_call(
        functools.partial(_band_attn_kernel, tq=tq, tk=tk, nk=nk, nback=nback, branch=branch, use_sel=use_sel),
        grid=(bsz, nq, nk),
        in_specs=in_specs,
        out_specs=pl.BlockSpec((tq, D_MODEL), lambda b, i, kk: (b * nq + i, 0)),
        out_shape=jax.ShapeDtypeStruct((bsz * t, D_MODEL), BF16),
        scratch_shapes=[pltpu.VMEM((N_KV, GQA * tq, LANES), F32), pltpu.VMEM((N_KV, GQA * tq, LANES), F32)],
        compiler_params=_cparams(("parallel", "parallel", "arbitrary")),
        name="sel_attn" if use_sel else "win_attn",
    )(*args)


N_PAIR = N_KV // 2


def _pair_init(m_ref, l_ref, acc_ref):
    m_ref[...] = jnp.full(m_ref.shape, M_INIT, F32)
    l_ref[...] = jnp.zeros(l_ref.shape, F32)
    acc_ref[...] = jnp.zeros(acc_ref.shape, F32)


def _pair_queries(qz_ref, pr, tq):
    return jnp.concatenate([qz_ref[2 * pr].reshape(GQA * tq, LANES), qz_ref[2 * pr + 1].reshape(GQA * tq, LANES)],
                           axis=0).astype(BF16)


def _pair_update(pr, q, kv, bias_a, bias_b, m_ref, l_ref, acc_ref, feature_major=False):
    bias = jnp.concatenate([bias_a] * GQA + [bias_b] * GQA, axis=0)
    if feature_major:
        k2 = kv[pr * LANES:(pr + 1) * LANES, :]
        v2 = kv[KV_DIM + pr * LANES: KV_DIM + (pr + 1) * LANES, :]
        s = jnp.dot(q, k2, preferred_element_type=F32) + bias
        pv_dims = _NT
    else:
        k2 = kv[:, pr * LANES:(pr + 1) * LANES]
        v2 = kv[:, KV_DIM + pr * LANES: KV_DIM + (pr + 1) * LANES]
        s = lax.dot_general(q, k2, _NT, preferred_element_type=F32) + bias
        pv_dims = (((1,), (0,)), ((), ()))
    m_old = m_ref[pr]
    m_new = jnp.maximum(m_old, jnp.max(s, axis=-1, keepdims=True))
    e = jnp.exp2(s - m_new[:, 0:1])
    alpha = jnp.exp2(m_old - m_new)
    l_ref[pr] = alpha * l_ref[pr] + jnp.sum(e, axis=-1, keepdims=True)
    acc_ref[pr] = alpha * acc_ref[pr] + lax.dot_general(e.astype(BF16), v2, pv_dims, preferred_element_type=F32)
    m_ref[pr] = m_new


def _pair_finish(o_ref, gate, branch, tq, l_ref, acc_ref):
    pieces, halves = [None] * H_ATT, [0] * H_ATT
    for pr in range(N_PAIR):
        l = l_ref[pr]
        o = acc_ref[pr] * jnp.where(l > 0.0, 1.0 / l, 0.0)
        for side in range(2):
            for g in range(GQA):
                hq = (2 * pr + side) * GQA + g
                r0 = (side * GQA + g) * tq
                pieces[hq] = o[r0:r0 + tq] * _gate_col(gate, hq, branch)
                halves[hq] = side
    _emit_heads(o_ref, pieces, halves)


def _causal_bias(tq):
    qi = lax.broadcasted_iota(jnp.int32, (tq, tq), 0)
    ki = lax.broadcasted_iota(jnp.int32, (tq, tq), 1)
    return jnp.where(ki <= qi, 0.0, MASKV)


def _pair_scratch(tq):
    rows = 2 * GQA * tq
    return [pltpu.VMEM((N_PAIR, rows, LANES), F32)] * 3


def _sel_paged_kernel(pt_ref, *refs, tq, n_steps, page, group):
    page_refs = refs[:group]
    qz_ref, new_ref, gate_ref, selb_ref, o_ref, m_ref, l_ref, acc_ref = refs[group:]
    p = pl.program_id(1)

    @pl.when(p == 0)
    def _():
        _pair_init(m_ref, l_ref, acc_ref)

    @pl.when(p < n_steps)
    def _():
        kv = jnp.concatenate([r[0, 0] for r in page_refs], axis=1).astype(BF16)
        expand = _block_expand(p * (group * page), group * page)
        bias = [jnp.dot(selb_ref[0, h], expand, preferred_element_type=F32) for h in range(N_KV)]
        for pr in range(N_PAIR):
            _pair_update(pr, _pair_queries(qz_ref, pr, tq), kv, bias[2 * pr], bias[2 * pr + 1], m_ref, l_ref, acc_ref,
                         feature_major=True)

    @pl.when(p == n_steps)
    def _():
        kv = new_ref[...].astype(BF16)
        bias = _causal_bias(tq)
        for pr in range(N_PAIR):
            _pair_update(pr, _pair_queries(qz_ref, pr, tq), kv, bias, bias, m_ref, l_ref, acc_ref)
        _pair_finish(o_ref, _sigmoid(gate_ref[...]), 1, tq, l_ref, acc_ref)


def _sel_paged(qz, pool, layer, page_table, new_rows, us, selb, tq):
    bsz, n_pages = page_table.shape
    blk, page = _page_shape(pool, True)
    group = SEL_PAGE_GROUP
    n_steps = n_pages // group
    page_spec = lambda k: pl.BlockSpec(
        blk, lambda b, p, pt: (layer, pt[b, jnp.minimum(p, n_steps - 1) * group + k], 0, 0))
    grid_spec = pltpu.PrefetchScalarGridSpec(
        num_scalar_prefetch=1,
        grid=(bsz, n_steps + 1),
        in_specs=[page_spec(k) for k in range(group)] + [
            pl.BlockSpec((N_KV, GQA, tq, LANES), lambda b, p, pt: (0, 0, b, 0)),
            pl.BlockSpec((tq, 2 * KV_DIM), lambda b, p, pt: (b, 0)),
            pl.BlockSpec((tq, LANES), lambda b, p, pt: (b, 0)),
            pl.BlockSpec((1, N_KV, tq, LANES), lambda b, p, pt: (b, 0, 0, 0))],
        out_specs=pl.BlockSpec((tq, D_MODEL), lambda b, p, pt: (b, 0)),
        scratch_shapes=_pair_scratch(tq),
    )
    return pl.pallas_call(
        functools.partial(_sel_paged_kernel, tq=tq, n_steps=n_steps, page=page, group=group),
        grid_spec=grid_spec,
        out_shape=jax.ShapeDtypeStruct((bsz * tq, D_MODEL), F32),
        compiler_params=_cparams(("parallel", "arbitrary")),
        name="sel_paged",
    )(page_table, *([pool] * group), qz, new_rows, us, selb)


def _win_cached_kernel(qz_ref, wc_ref, new_ref, gate_ref, o_ref, m_ref, l_ref, acc_ref, *, tq, wlen):
    _pair_init(m_ref, l_ref, acc_ref)
    qi = lax.broadcasted_iota(jnp.int32, (tq, wlen), 0)
    kj = lax.broadcasted_iota(jnp.int32, (tq, wlen), 1)
    bias_c = jnp.where((wlen - kj + qi) < WINDOW, 0.0, MASKV)
    bias_n = _causal_bias(tq)
    kv_c = wc_ref[0, 0].astype(BF16)
    kv_n = new_ref[...].astype(BF16)
    for pr in range(N_PAIR):
        q = _pair_queries(qz_ref, pr, tq)
        _pair_update(pr, q, kv_c, bias_c, bias_c, m_ref, l_ref, acc_ref, feature_major=True)
        _pair_update(pr, q, kv_n, bias_n, bias_n, m_ref, l_ref, acc_ref)
    _pair_finish(o_ref, _sigmoid(gate_ref[...]), 2, tq, l_ref, acc_ref)


def _win_cached(qz, wcache, layer, new_rows, us, tq):
    bsz, wlen = wcache.shape[1], wcache.shape[3]
    return pl.pallas_call(
        functools.partial(_win_cached_kernel, tq=tq, wlen=wlen),
        grid=(bsz,),
        in_specs=[pl.BlockSpec((N_KV, GQA, tq, LANES), lambda b: (0, 0, b, 0)),
                  pl.BlockSpec((1, 1, 2 * KV_DIM, wlen), lambda b: (layer, b, 0, 0)),
                  pl.BlockSpec((tq, 2 * KV_DIM), lambda b: (b, 0)),
                  pl.BlockSpec((tq, LANES), lambda b: (b, 0))],
        out_specs=pl.BlockSpec((tq, D_MODEL), lambda b: (b, 0)),
        out_shape=jax.ShapeDtypeStruct((bsz * tq, D_MODEL), F32),
        scratch_shapes=_pair_scratch(tq),
        compiler_params=_cparams(("parallel",)),
        name="win_cached",
    )(qz, wcache, new_rows, us)


def _merge_kernel(x_ref, yn_ref, ysc_ref, oc_ref, os_ref, ow_ref, g1_ref, g2_ref, g3_ref,
                  wssd_ref, wsc_ref, wnsa_ref, wout_ref, o_ref):
    y_ssd = jnp.dot(yn_ref[...].astype(BF16), wssd_ref[...], preferred_element_type=F32)
    y_sc = jnp.dot(ysc_ref[...].astype(BF16), wsc_ref[...], preferred_element_type=F32)
    o = (oc_ref[...].astype(F32) + os_ref[...].astype(F32) + ow_ref[...].astype(F32)).astype(BF16)
    y_nsa = jnp.dot(o, wnsa_ref[...], preferred_element_type=F32)
    mix = _sigmoid(g1_ref[...]) * y_ssd + _sigmoid(g2_ref[...]) * y_sc + _sigmoid(g3_ref[...]) * y_nsa
    o_ref[...] = x_ref[...] + jnp.dot(mix.astype(BF16), wout_ref[...], preferred_element_type=F32)


def _merge(x, yn, ysc, oc, osel, ow, u, w, tm):
    m = x.shape[0]
    row = lambda c: pl.BlockSpec((tm, D_MODEL), lambda i: (i, c))
    wspec = pl.BlockSpec((D_MODEL, D_MODEL), lambda i: (0, 0))
    gcol = U_G // D_MODEL
    return pl.pallas_call(
        _merge_kernel,
        grid=(m // tm,),
        in_specs=[row(0)] * 6 + [row(gcol), row(gcol + 1), row(gcol + 2)] + [wspec] * 4,
        out_specs=row(0),
        out_shape=jax.ShapeDtypeStruct((m, D_MODEL), F32),
        compiler_params=_cparams(("parallel",)),
        name="merge",
    )(x, yn, ysc, oc, osel, ow, u, u, u, w["w_ssd_out"], w["w_sconv_out"], w["w_nsa_out"], w["w_out"])


def _mlp_kernel(x_ref, g_ref, wup_ref, wdn_ref, o_ref, h_ref, acc_ref, *, nf):
    j = pl.program_id(1)

    @pl.when(j == 0)
    def _():
        x = x_ref[...]
        ms = jnp.mean(x * x, axis=-1, keepdims=True)
        h_ref[...] = (x * lax.rsqrt(ms + EPS) * g_ref[...]).astype(h_ref.dtype)
        acc_ref[...] = jnp.zeros(acc_ref.shape, F32)

    up = jnp.dot(h_ref[...], wup_ref[...], preferred_element_type=F32)
    a = jnp.square(jnp.maximum(up, 0.0)).astype(BF16)
    acc_ref[...] += jnp.dot(a, wdn_ref[...], preferred_element_type=F32)

    @pl.when(j == nf - 1)
    def _():
        o_ref[...] = x_ref[...] + acc_ref[...]


def _mlp(x, g, wup, wdn, tm, tf):
    m = x.shape[0]
    nf = D_FF // tf
    return pl.pallas_call(
        functools.partial(_mlp_kernel, nf=nf),
        grid=(m // tm, nf),
        in_specs=[pl.BlockSpec((tm, D_MODEL), lambda i, j: (i, 0)),
                  pl.BlockSpec((1, D_MODEL), lambda i, j: (0, 0)),
                  pl.BlockSpec((D_MODEL, tf), lambda i, j: (0, j)),
                  pl.BlockSpec((tf, D_MODEL), lambda i, j: (j, 0))],
        out_specs=pl.BlockSpec((tm, D_MODEL), lambda i, j: (i, 0)),
        out_shape=jax.ShapeDtypeStruct((m, D_MODEL), F32),
        scratch_shapes=[pltpu.VMEM((tm, D_MODEL), BF16), pltpu.VMEM((tm, D_MODEL), F32)],
        compiler_params=_cparams(("parallel", "arbitrary")),
        name="mlp",
    )(x, g, wup, wdn)


def _pad_lanes(v, width=LANES):
    v = v.reshape(1, -1).astype(F32)
    return jnp.pad(v, ((0, 0), (0, width - v.shape[1])))


def _pad_rows(v, rows=SUBLANES, front=False):
    pad = rows - v.shape[-2]
    cfg = [(0, 0)] * (v.ndim - 2) + [((pad, 0) if front else (0, pad)), (0, 0)]
    return jnp.pad(v, cfg)


def _layer_weights(l, w_in, norm_mix, norm_mlp, ssd_conv_w, ssd_conv_b, ssd_dt_bias, ssd_a_log, ssd_d, ssd_norm,
                   w_ssd_out, sconv_w, w_sconv_out, cmp_pe, cmp_w1, cmp_w2, w_nsa_out, w_out, w_mlp_up, w_mlp_down):
    b = np.cumsum((0,) + IN_SIZES)
    wi = w_in[l]
    seg = lambda k: wi[:, b[k]:b[k + 1]]
    w_main = jnp.concatenate([seg(1), seg(7), seg(0), seg(3), seg(4), seg(5), seg(6), seg(9)], axis=1).astype(BF16)
    w_small = jnp.concatenate([seg(2), seg(8)], axis=1)
    w_small = jnp.pad(w_small, ((0, 0), (0, LANES - w_small.shape[1]))).astype(BF16)
    return dict(
        w_main=w_main, w_small=w_small,
        norm_mix=norm_mix[l].reshape(1, -1), norm_mlp=norm_mlp[l].reshape(1, -1),
        convw=_pad_rows(ssd_conv_w[l]), convb=ssd_conv_b[l].reshape(1, -1),
        dtb=_pad_lanes(ssd_dt_bias[l]), alog=_pad_lanes(ssd_a_log[l]), dvec=_pad_lanes(ssd_d[l]),
        ssdn=ssd_norm[l].reshape(1, -1), scw=_pad_rows(sconv_w[l]),
        w_ssd_out=w_ssd_out[l].astype(BF16), w_sconv_out=w_sconv_out[l].astype(BF16),
        w_nsa_out=w_nsa_out[l].astype(BF16), w_out=w_out[l].astype(BF16),
        cmp_pe=cmp_pe[l].reshape(2, 2, CMP_STRIDE, HEAD_DIM).transpose(0, 2, 1, 3).reshape(2, CMP_STRIDE, LANES),
        cmp_w1=cmp_w1[l].reshape(2, 2, CMP_STRIDE, HEAD_DIM, CMP_HID).transpose(0, 2, 1, 3, 4).reshape(
            2, CMP_STRIDE, LANES, CMP_HID).astype(BF16),
        cmp_w2=cmp_w2[l].astype(BF16),
        w_mlp_up=w_mlp_up[l].astype(BF16), w_mlp_down=w_mlp_down[l].astype(BF16),
    )


def _rope_tables(pos):
    half = HEAD_DIM // 2
    inv = ROPE_THETA ** (-jnp.arange(half, dtype=F32) / half)
    ang = pos.astype(F32)[:, None] * inv[None, :]
    cos, sin = jnp.cos(ang), jnp.sin(ang)
    cos_t = jnp.concatenate([cos, cos, cos, cos], axis=1)
    sin_t = jnp.concatenate([-sin, sin, -sin, sin], axis=1)
    return cos_t, sin_t


def _overlap(ncmp_pad, ncmp):
    c0 = np.arange(ncmp_pad)[:, None] * CMP_STRIDE
    s0 = np.arange(LANES)[None, :] * SEL_BLK
    ov = np.maximum(np.minimum(c0 + CMP_LEN, s0 + SEL_BLK) - np.maximum(c0, s0), 0).astype(np.float32) / CMP_LEN
    ov[ncmp:] = 0.0
    return jnp.asarray(ov)


def _tile(m, pref):
    t = pref
    while m % t:
        t //= 2
    return t


def _front(x2d, w, tm):
    u = _norm_matmul(x2d, w["norm_mix"], w["w_main"], tm, N_MAIN // 8)
    us = _norm_matmul(x2d, w["norm_mix"], w["w_small"], tm, LANES)
    return u, us


def _prompt_layer(x2d, w, bsz, t, tabs):
    m = bsz * t
    u, us = _front(x2d, w, _tile(m, 1024))
    cos_t, sin_t, ov = tabs
    qz, cmp_b, sel_b, win_b, cmp_t, sel_t, win_t = _rope(u, cos_t, sin_t, _tile(t, 512), bsz, t, True)
    zeros = lambda *s: jnp.zeros(s, F32)
    yn, ysc, h_new, ch_last = _seqmix(u, us, zeros(bsz, SUBLANES, CONV_DIM), zeros(bsz, SUBLANES, D_MODEL),
                                      zeros(bsz, D_MODEL, SSD_STATE), w, bsz, t)
    n_pg = t // CMP_PAGE
    kc, vc = _compress(cmp_b.reshape(1, bsz * n_pg, CMP_PAGE, 2 * KV_DIM), 0,
                       jnp.arange(bsz * n_pg, dtype=jnp.int32).reshape(bsz, n_pg),
                       w["cmp_pe"], w["cmp_w1"], w["cmp_w2"], False)
    tq = _tile(t, 256)
    oc, selb = _cmp_attn(qz, kc, vc, us, ov, bsz, t, tq, 0, N_TOP - N_FORCED, None)
    ta = _tile(t, 512)
    osel = _band_attn(qz, sel_b, us, selb, bsz, t, ta, ta, True)
    ow = _band_attn(qz, win_b, us, None, bsz, t, ta, ta, False)
    x1 = _merge(x2d, yn, ysc, oc, osel, ow, u, w, _tile(m, 256))
    x2 = _mlp(x1, w["norm_mlp"], w["w_mlp_up"], w["w_mlp_down"], _tile(m, 512), 1024)
    wl = min(WINDOW, t)
    u3 = u.reshape(bsz, t, N_MAIN)
    rows_major = lambda a: a.reshape(bsz, 2, N_KV, HEAD_DIM, a.shape[-1]).transpose(0, 4, 1, 2, 3)
    state = (rows_major(cmp_t), rows_major(sel_t), rows_major(win_t[:, :, t - wl:]),
             h_new.reshape(bsz, H_SSD, SSD_HEAD_DIM, SSD_STATE),
             u3[:, t - (SSD_CONV - 1):, U_XBC:U_XBC + CONV_DIM],
             ch_last[:, SUBLANES - (SC_WIDTH - 1):])
    return x2, state


def _feature_major(cache):
    nd = cache.ndim
    perm = tuple(range(nd - 4)) + (nd - 3, nd - 2, nd - 1, nd - 4)
    return jnp.transpose(cache, perm).reshape(cache.shape[:-4] + (2 * KV_DIM, cache.shape[-4]))


def _sample_layer(x2d, w, bsz, t, tabs, layer, past):
    ssm0, conv0, sc0, cmp_pool, sel_pool, win_cache, win_rows_old, page_table = past
    m = bsz * t
    u, us = _front(x2d, w, m)
    cos_t, sin_t, ov = tabs
    qz, cmp_rows, sel_rows, win_rows = _rope(u, cos_t, sin_t, m, bsz, t, False)
    yn, ysc, h_new, ch_last = _seqmix(u, us, _pad_rows(conv0, front=True), _pad_rows(sc0, front=True),
                                      ssm0.reshape(bsz, D_MODEL, SSD_STATE), w, bsz, t)
    kc, vc = _compress(cmp_pool, layer, page_table, w["cmp_pe"], w["cmp_w1"], w["cmp_w2"], True)
    past_len = page_table.shape[1] * cmp_pool.shape[3]
    oc, selb = _cmp_attn(qz, kc, vc, us, ov, bsz, t, t, past_len, N_TOP - N_FORCED, past_len // SEL_BLK)
    osel = _sel_paged(qz, sel_pool, layer, page_table, sel_rows, us, selb, t)
    ow = _win_cached(qz, win_cache, layer, win_rows, us, t)
    x1 = _merge(x2d, yn, ysc, oc, osel, ow, u, w, m)
    x2 = _mlp(x1, w["norm_mlp"], w["w_mlp_up"], w["w_mlp_down"], m, 1024)
    wl = win_rows_old.shape[1]
    win_all = jnp.concatenate([win_rows_old, win_rows.reshape(bsz, t, 2, N_KV, HEAD_DIM)], axis=1)
    keep = min(WINDOW, wl + t)
    u3 = u.reshape(bsz, t, N_MAIN)
    state = (cmp_rows.reshape(bsz, t, 2, N_KV, HEAD_DIM), sel_rows.reshape(bsz, t, 2, N_KV, HEAD_DIM),
             win_all[:, wl + t - keep:],
             h_new.reshape(bsz, H_SSD, SSD_HEAD_DIM, SSD_STATE),
             u3[:, t - (SSD_CONV - 1):, U_XBC:U_XBC + CONV_DIM],
             ch_last[:, SUBLANES - (SC_WIDTH - 1):])
    return x2, state


def kernel(x_prompt, x_sample, cache_cmp_kv, cache_sel_kv, cache_win_kv, state_ssm, state_ssd_conv, state_sconv,
           page_table, norm_mix, norm_mlp, norm_final, w_in, ssd_conv_w, ssd_conv_b, ssd_dt_bias, ssd_a_log, ssd_d,
           ssd_norm, w_ssd_out, sconv_w, w_sconv_out, cmp_pe, cmp_w1, cmp_w2, w_nsa_out, w_out, w_mlp_up,
           w_mlp_down):
    depth = w_in.shape[0]
    bp, tp, _ = x_prompt.shape
    bs, ts, _ = x_sample.shape
    n_pool, page = cache_cmp_kv.shape[1:3]
    past_len = page_table.shape[1] * page
    wl = cache_win_kv.shape[2]
    for rows in (tp, past_len):
        assert rows % (LANES * CMP_STRIDE) == 0 and (rows // CMP_STRIDE) % min(CMP_TILE, rows // CMP_STRIDE) == 0
        assert rows // SEL_BLK <= LANES
    assert ts == SUBLANES
    assert wl == WINDOW and past_len >= WINDOW
    assert page_table.shape[1] % SEL_PAGE_GROUP == 0 and (tp // CMP_PAGE) % PAGE_GROUP == 0
    assert (past_len + ts - CMP_LEN) // CMP_STRIDE + 1 == (past_len - CMP_LEN) // CMP_STRIDE + 1

    tabs_p = _rope_tables(jnp.tile(jnp.arange(tp), bp)) + (_overlap(tp // CMP_STRIDE, (tp - CMP_LEN) // CMP_STRIDE + 1),)
    tabs_s = _rope_tables(jnp.tile(past_len + jnp.arange(ts), bs)) + (
        _overlap(past_len // CMP_STRIDE, (past_len - CMP_LEN) // CMP_STRIDE + 1),)

    cmp_pool, sel_pool, win_cache = _feature_major(cache_cmp_kv), _feature_major(cache_sel_kv), _feature_major(cache_win_kv)
    xp = x_prompt.reshape(bp * tp, D_MODEL)
    xs = x_sample.reshape(bs * ts, D_MODEL)
    p_new = [[] for _ in range(6)]
    s_new = [[] for _ in range(6)]
    for l in range(depth):
        w = _layer_weights(l, w_in, norm_mix, norm_mlp, ssd_conv_w, ssd_conv_b, ssd_dt_bias, ssd_a_log, ssd_d,
                           ssd_norm, w_ssd_out, sconv_w, w_sconv_out, cmp_pe, cmp_w1, cmp_w2, w_nsa_out, w_out,
                           w_mlp_up, w_mlp_down)
        xp, st_p = _prompt_layer(xp, w, bp, tp, tabs_p)
        past = (state_ssm[l], state_ssd_conv[l], state_sconv[l], cmp_pool, sel_pool, win_cache, cache_win_kv[l],
                page_table)
        xs, st_s = _sample_layer(xs, w, bs, ts, tabs_s, l, past)
        for i in range(6):
            p_new[i].append(st_p[i])
            s_new[i].append(st_s[i])
    gf = norm_final.reshape(1, -1)
    y_prompt = _final_norm(xp, gf, _tile(bp * tp, 1024)).reshape(bp, tp, D_MODEL)
    y_sample = _final_norm(xs, gf, bs * ts).reshape(bs, ts, D_MODEL)
    return (y_prompt, y_sample) + tuple(jnp.stack(a) for a in p_new) + tuple(jnp.stack(a) for a in s_new)
```

```python
import functools
import math

import numpy as np
import jax
import jax.numpy as jnp
from jax import lax
from jax.experimental import pallas as pl
from jax.experimental.pallas import tpu as pltpu

F32 = jnp.float32
BF16 = jnp.bfloat16

D_MODEL = 1024
SSD_HEAD_DIM = 64
H_SSD = D_MODEL // SSD_HEAD_DIM
SSD_STATE = 128
SSD_GROUPS = 2
SSD_CONV = 4
SSD_CHUNK = 128
CONV_DIM = D_MODEL + 2 * SSD_GROUPS * SSD_STATE
SC_WIDTH = 3
HEAD_DIM = 64
H_ATT = D_MODEL // HEAD_DIM
N_KV = 4
GQA = H_ATT // N_KV
KV_DIM = N_KV * HEAD_DIM
CMP_LEN = 32
CMP_STRIDE = 16
CMP_HID = 4 * HEAD_DIM
SEL_BLK = 64
N_TOP = 16
WINDOW = 512
D_FF = 4 * D_MODEL
ROPE_THETA = 10000.0
EPS = 1e-6
NEG_INF = -1e30
BIG = 1e9
TAKEN = -3e38
N_FORCED = 3
LOG2E = 1.4426950408889634
MASKV = -1e30
M_INIT = -1e29
IN_SIZES = (D_MODEL, CONV_DIM, H_SSD, D_MODEL, D_MODEL, D_MODEL, D_MODEL, 6 * KV_DIM, 3 * H_ATT, 3 * D_MODEL)

LANES = 128
SUBLANES = 8
VMEM_LIMIT = 56 * 1024 * 1024

U_XBC, U_KV, U_Z, U_SCB, U_SCC, U_SCH, U_Q, U_G = 0, 1536, 3072, 4096, 5120, 6144, 7168, 8192
N_MAIN = 11264
GATE_COL0 = H_SSD

_NT = (((1,), (1,)), ((), ()))
_TN = (((0,), (0,)), ((), ()))


def _cparams(sem):
    return pltpu.CompilerParams(dimension_semantics=sem, vmem_limit_bytes=VMEM_LIMIT)


def _silu(x):
    return x * (1.0 / (1.0 + jnp.exp(-x)))


def _sigmoid(x):
    return 1.0 / (1.0 + jnp.exp(-x))


def _norm_matmul_kernel(x_ref, g_ref, w_ref, o_ref, h_ref):
    @pl.when(pl.program_id(1) == 0)
    def _():
        x = x_ref[...]
        ms = jnp.mean(x * x, axis=-1, keepdims=True)
        h_ref[...] = (x * lax.rsqrt(ms + EPS) * g_ref[...]).astype(h_ref.dtype)

    o_ref[...] = jnp.dot(h_ref[...], w_ref[...], preferred_element_type=F32).astype(o_ref.dtype)


def _norm_matmul(x, g, w, tm, tn):
    m, d = x.shape
    n = w.shape[1]
    return pl.pallas_call(
        _norm_matmul_kernel,
        grid=(m // tm, n // tn),
        in_specs=[pl.BlockSpec((tm, d), lambda i, j: (i, 0)),
                  pl.BlockSpec((1, d), lambda i, j: (0, 0)),
                  pl.BlockSpec((d, tn), lambda i, j: (0, j))],
        out_specs=pl.BlockSpec((tm, tn), lambda i, j: (i, j)),
        out_shape=jax.ShapeDtypeStruct((m, n), F32),
        scratch_shapes=[pltpu.VMEM((tm, d), BF16)],
        compiler_params=_cparams(("parallel", "arbitrary")),
        name="norm_matmul",
    )(x, g, w)


def _final_norm_kernel(x_ref, g_ref, o_ref):
    x = x_ref[...]
    ms = jnp.mean(x * x, axis=-1, keepdims=True)
    o_ref[...] = x * lax.rsqrt(ms + EPS) * g_ref[...]


def _final_norm(x, g, tm):
    m, d = x.shape
    return pl.pallas_call(
        _final_norm_kernel,
        grid=(m // tm,),
        in_specs=[pl.BlockSpec((tm, d), lambda i: (i, 0)), pl.BlockSpec((1, d), lambda i: (0, 0))],
        out_specs=pl.BlockSpec((tm, d), lambda i: (i, 0)),
        out_shape=jax.ShapeDtypeStruct((m, d), F32),
        compiler_params=_cparams(("parallel",)),
        name="final_norm",
    )(x, g)


def _rot_half(x, first_half):
    return jnp.where(first_half, pltpu.roll(x, LANES - HEAD_DIM // 2, axis=1), pltpu.roll(x, HEAD_DIM // 2, axis=1))


def _rope_kernel(q_ref, kv_ref, cos_ref, sin_ref, qz_ref, *out_refs, scale, feature_major):
    row_refs, t_refs = out_refs[:3], out_refs[3:]
    cos = cos_ref[...]
    sin = sin_ref[...]
    lane = lax.broadcasted_iota(jnp.int32, cos.shape, 1)
    first_half = (lane % HEAD_DIM) < (HEAD_DIM // 2)
    left = lane < HEAD_DIM

    def rope(x):
        return x * cos + _rot_half(x, first_half) * sin

    for c in range(D_MODEL // LANES):
        r = rope(q_ref[:, c * LANES:(c + 1) * LANES]) * scale
        sw = pltpu.roll(r, HEAD_DIM, axis=1)
        h = (2 * c) // GQA
        g = (2 * c) % GQA
        if h % 2 == 0:
            qz_ref[h, g] = jnp.where(left, r, 0.0).astype(qz_ref.dtype)
            qz_ref[h, g + 1] = jnp.where(left, sw, 0.0).astype(qz_ref.dtype)
        else:
            qz_ref[h, g] = jnp.where(left, 0.0, sw).astype(qz_ref.dtype)
            qz_ref[h, g + 1] = jnp.where(left, 0.0, r).astype(qz_ref.dtype)
    for br in range(3):
        base = br * 2 * KV_DIM
        for c in range(2 * KV_DIM // LANES):
            x = kv_ref[:, base + c * LANES: base + (c + 1) * LANES]
            if c < KV_DIM // LANES:
                x = rope(x)
            row_refs[br][:, c * LANES:(c + 1) * LANES] = x.astype(row_refs[br].dtype)
            if feature_major:
                t_refs[br][0, c * LANES:(c + 1) * LANES, :] = x.T


def _rope(u, cos, sin, tm, bsz, t, prompt):
    m = u.shape[0]
    nt = t // tm if prompt else 1
    act = BF16 if prompt else F32
    kv_spec = pl.BlockSpec((tm, 2 * KV_DIM), lambda i: (i, 0))
    out_specs = [pl.BlockSpec((N_KV, GQA, tm, LANES), lambda i: (0, 0, i, 0)), kv_spec, kv_spec, kv_spec]
    out_shape = [jax.ShapeDtypeStruct((N_KV, GQA, m, LANES), act)] + [jax.ShapeDtypeStruct((m, 2 * KV_DIM), act)] * 3
    if prompt:
        out_specs += [pl.BlockSpec((1, 2 * KV_DIM, tm), lambda i: (i // nt, 0, i % nt))] * 3
        out_shape += [jax.ShapeDtypeStruct((bsz, 2 * KV_DIM, t), F32)] * 3
    return pl.pallas_call(
        functools.partial(_rope_kernel, scale=HEAD_DIM ** -0.5 * LOG2E, feature_major=prompt),
        grid=(m // tm,),
        in_specs=[pl.BlockSpec((tm, D_MODEL), lambda i: (i, U_Q // D_MODEL)),
                  pl.BlockSpec((tm, 6 * KV_DIM), lambda i: (i, U_KV // (6 * KV_DIM))),
                  pl.BlockSpec((tm, LANES), lambda i: (i, 0)),
                  pl.BlockSpec((tm, LANES), lambda i: (i, 0))],
        out_specs=out_specs,
        out_shape=out_shape,
        compiler_params=_cparams(("parallel",)),
        name="rope",
    )(u, u, cos, sin)


def _seqmix_kernel(xbc_ref, z_ref, scb_ref, scc_ref, sch_ref, dtr_ref, conv0_ref, sc0_ref, h0_ref,
                   convw_ref, convb_ref, dtb_ref, alog_ref, dvec_ref, ssdn_ref, scw_ref,
                   yn_ref, ysc_ref, hout_ref, chlast_ref,
                   ext_ref, chext_ref, h_ref, y_ref, *, chunk, valid):
    c = pl.program_id(1)
    L = chunk
    P = SSD_HEAD_DIM
    N = SSD_STATE

    @pl.when(c == 0)
    def _():
        ext_ref[0:SUBLANES, :] = conv0_ref[0]
        chext_ref[0:SUBLANES, :] = sc0_ref[0]
        h_ref[...] = h0_ref[0]

    def load(ref):
        x = ref[...]
        if valid < L:
            x = jnp.concatenate([x, jnp.zeros((L - valid, x.shape[1]), x.dtype)], axis=0)
        return x

    ext_ref[SUBLANES:SUBLANES + L, :] = load(xbc_ref)
    conv = jnp.broadcast_to(convb_ref[...], (L, CONV_DIM))
    for k in range(SSD_CONV):
        off = SUBLANES - (SSD_CONV - 1) + k
        conv = conv + convw_ref[k:k + 1, :] * ext_ref[off:off + L, :]
    ext_ref[0:SUBLANES, :] = ext_ref[L:L + SUBLANES, :]
    xbc = _silu(conv)

    ch = load(scc_ref) * load(sch_ref)
    chext_ref[SUBLANES:SUBLANES + L, :] = ch
    cv = jnp.zeros((L, D_MODEL), F32)
    for k in range(SC_WIDTH):
        off = SUBLANES - (SC_WIDTH - 1) + k
        cv = cv + scw_ref[k:k + 1, :] * chext_ref[off:off + L, :]
    ysc_ref[...] = (load(scb_ref) * cv)[:valid].astype(ysc_ref.dtype)
    chlast_ref[0] = chext_ref[valid:valid + SUBLANES, :]
    chext_ref[0:SUBLANES, :] = chext_ref[L:L + SUBLANES, :]

    x = load(dtr_ref) + dtb_ref[...]
    dt = jnp.maximum(x, 0.0) + jnp.log1p(jnp.exp(-jnp.abs(x)))
    row = lax.broadcasted_iota(jnp.int32, (L, LANES), 0)
    if valid < L:
        dt = jnp.where(row < valid, dt, 0.0)
    a = -jnp.exp(alog_ref[...])
    da = dt * a
    ri = lax.broadcasted_iota(jnp.int32, (L, L), 0)
    ci = lax.broadcasted_iota(jnp.int32, (L, L), 1)
    tril = ri >= ci
    acum = jnp.dot(tril.astype(F32), da, precision=lax.Precision.HIGHEST, preferred_element_type=F32)
    acum_t = acum.T
    a_last = acum[L - 1:L, :]

    for g in range(SSD_GROUPS):
        bg = xbc[:, D_MODEL + g * N: D_MODEL + (g + 1) * N]
        cg = xbc[:, D_MODEL + SSD_GROUPS * N + g * N: D_MODEL + SSD_GROUPS * N + (g + 1) * N]
        cb = lax.dot_general(cg, bg, _NT, preferred_element_type=F32)
        for r in range(H_SSD // SSD_GROUPS):
            h = g * (H_SSD // SSD_GROUPS) + r
            a_col = acum[:, h:h + 1]
            a_row = acum_t[h:h + 1, :]
            seg = jnp.where(tril, a_col - a_row, NEG_INF)
            wts = jnp.exp(seg) * cb
            xs_h = xbc[:, h * P:(h + 1) * P]
            xdt = xs_h * dt[:, h:h + 1]
            hprev = h_ref[h * P:(h + 1) * P, :]
            y_h = jnp.dot(wts, xdt, preferred_element_type=F32)
            y_in = lax.dot_general(cg, hprev, _NT, preferred_element_type=F32)
            y_h = y_h + y_in * jnp.exp(a_col)
            y_h = y_h + dvec_ref[:, h:h + 1] * xs_h
            y_ref[:, h * P:(h + 1) * P] = y_h
            al = a_last[:, h:h + 1]
            xw = xdt * jnp.exp(al - a_col)
            s_h = lax.dot_general(xw, bg, _TN, preferred_element_type=F32)
            h_ref[h * P:(h + 1) * P, :] = jnp.exp(al) * hprev + s_h

    y = y_ref[0:valid, :] * _silu(z_ref[...])
    ms = jnp.mean(y * y, axis=-1, keepdims=True)
    yn_ref[...] = (y * lax.rsqrt(ms + EPS) * ssdn_ref[...]).astype(yn_ref.dtype)
    hout_ref[0] = h_ref[...]


def _seqmix(u, us, conv0, sc0, h0, prm, bsz, t):
    L = SSD_CHUNK
    valid = min(t, L)
    nc = t // valid
    m = bsz * t
    act = _act_dtype(valid)
    row = lambda b, c: b * nc + c
    spec_u = lambda w, col: pl.BlockSpec((valid, w), lambda b, c: (row(b, c), col // w))
    full = lambda shape: pl.BlockSpec(shape, lambda b, c: (0,) * len(shape))
    perb = lambda shape: pl.BlockSpec((1,) + shape, lambda b, c: (b,) + (0,) * len(shape))
    return pl.pallas_call(
        functools.partial(_seqmix_kernel, chunk=L, valid=valid),
        grid=(bsz, nc),
        in_specs=[spec_u(CONV_DIM, U_XBC), spec_u(D_MODEL, U_Z), spec_u(D_MODEL, U_SCB), spec_u(D_MODEL, U_SCC),
                  spec_u(D_MODEL, U_SCH),
                  pl.BlockSpec((valid, LANES), lambda b, c: (row(b, c), 0)),
                  perb((SUBLANES, CONV_DIM)), perb((SUBLANES, D_MODEL)), perb((D_MODEL, SSD_STATE)),
                  full((SUBLANES, CONV_DIM)), full((1, CONV_DIM)), full((1, LANES)), full((1, LANES)),
                  full((1, LANES)), full((1, D_MODEL)), full((SUBLANES, D_MODEL))],
        out_specs=[pl.BlockSpec((valid, D_MODEL), lambda b, c: (row(b, c), 0)),
                   pl.BlockSpec((valid, D_MODEL), lambda b, c: (row(b, c), 0)),
                   perb((D_MODEL, SSD_STATE)), perb((SUBLANES, D_MODEL))],
        out_shape=[jax.ShapeDtypeStruct((m, D_MODEL), act), jax.ShapeDtypeStruct((m, D_MODEL), act),
                   jax.ShapeDtypeStruct((bsz, D_MODEL, SSD_STATE), F32),
                   jax.ShapeDtypeStruct((bsz, SUBLANES, D_MODEL), F32)],
        scratch_shapes=[pltpu.VMEM((L + SUBLANES, CONV_DIM), F32), pltpu.VMEM((L + SUBLANES, D_MODEL), F32),
                        pltpu.VMEM((D_MODEL, SSD_STATE), F32), pltpu.VMEM((L, D_MODEL), F32)],
        compiler_params=_cparams(("parallel", "arbitrary")),
        name="seqmix",
    )(u, u, u, u, u, us, conv0, sc0, h0, prm["convw"], prm["convb"], prm["dtb"], prm["alog"], prm["dvec"],
      prm["ssdn"], prm["scw"])


CMP_TILE = 512
CMP_PAGE = 128
PAGE_GROUP = 16
SEL_PAGE_GROUP = 16


N_COMBO = 2 * N_KV


def _compress_fill(z_ref, page, g0, feature_major):
    R = page.shape[1] if feature_major else page.shape[0]
    lo = pl.ds(g0 + CMP_STRIDE, R)
    hi = pl.ds(g0, R)
    for pr in range(N_COMBO // 2):
        if feature_major:
            x2 = page[pr * LANES:(pr + 1) * LANES, :].T
        else:
            x2 = page[:, pr * LANES:(pr + 1) * LANES].astype(F32)
        sw = pltpu.roll(x2, HEAD_DIM, axis=1)
        z_ref[2 * pr, lo, 0:HEAD_DIM] = x2[:, 0:HEAD_DIM]
        z_ref[2 * pr, hi, HEAD_DIM:LANES] = sw[:, HEAD_DIM:LANES]
        z_ref[2 * pr + 1, lo, 0:HEAD_DIM] = sw[:, 0:HEAD_DIM]
        z_ref[2 * pr + 1, hi, HEAD_DIM:LANES] = x2[:, HEAD_DIM:LANES]


def _compress_tile(z_ref, row0, pe_ref, w1_ref, w2_ref, out_refs, out_row0, tile):
    for c in range(N_COMBO):
        kv, h = divmod(c, N_KV)
        acc = jnp.zeros((tile, CMP_HID), F32)
        for r in range(CMP_STRIDE):
            zr = z_ref[c, pl.ds(row0 + CMP_STRIDE + r, tile, stride=CMP_STRIDE), :] + pe_ref[kv, r:r + 1, :]
            acc = acc + jnp.dot(zr.astype(BF16), w1_ref[kv, r], preferred_element_type=F32)
        o = jnp.dot(_silu(acc).astype(BF16), w2_ref[kv], preferred_element_type=F32)
        out_refs[kv][0, pl.ds(out_row0, tile), pl.ds(h * HEAD_DIM, HEAD_DIM)] = o.astype(out_refs[kv].dtype)


def _compress_kernel(pt_ref, *refs, n_pages, page, group, feature_major):
    page_refs = refs[:group]
    pe_ref, w1_ref, w2_ref, kc_ref, vc_ref, z_ref = refs[group:]
    p = pl.program_id(1)
    for k in range(group):
        _compress_fill(z_ref, page_refs[k][0, 0], pl.multiple_of((p * group + k) * page, page), feature_major)

    @pl.when(p == n_pages // group - 1)
    def _():
        rows = n_pages * page
        z_ref[:, rows:rows + CMP_STRIDE, HEAD_DIM:LANES] = jnp.zeros((N_COMBO, CMP_STRIDE, LANES - HEAD_DIM), F32)

        tile = min(CMP_TILE, rows // CMP_STRIDE)

        def body(j, carry):
            _compress_tile(z_ref, pl.multiple_of(j * tile * CMP_STRIDE, tile * CMP_STRIDE),
                           pe_ref, w1_ref, w2_ref, (kc_ref, vc_ref), pl.multiple_of(j * tile, tile), tile)
            return carry

        lax.fori_loop(0, rows // (tile * CMP_STRIDE), body, 0)


def _page_shape(pool, feature_major):
    return (1, 1) + pool.shape[2:], pool.shape[3] if feature_major else pool.shape[2]


def _compress(pool, layer, page_table, pe2, w1p, w2, feature_major):
    bsz, n_pages = page_table.shape
    blk, page = _page_shape(pool, feature_major)
    rows = n_pages * page
    group = PAGE_GROUP
    out = jax.ShapeDtypeStruct((bsz, rows // CMP_STRIDE, KV_DIM), BF16)
    full = lambda shape: pl.BlockSpec(shape, lambda b, p, pt: (0,) * len(shape))
    page_spec = lambda k: pl.BlockSpec(blk, lambda b, p, pt: (layer, pt[b, p * group + k], 0, 0))
    grid_spec = pltpu.PrefetchScalarGridSpec(
        num_scalar_prefetch=1,
        grid=(bsz, n_pages // group),
        in_specs=[page_spec(k) for k in range(group)] + [full(pe2.shape), full(w1p.shape), full(w2.shape)],
        out_specs=[pl.BlockSpec((1, rows // CMP_STRIDE, KV_DIM), lambda b, p, pt: (b, 0, 0)),
                   pl.BlockSpec((1, rows // CMP_STRIDE, KV_DIM), lambda b, p, pt: (b, 0, 0))],
        scratch_shapes=[pltpu.VMEM((N_COMBO, rows + CMP_STRIDE, LANES), F32)],
    )
    return pl.pallas_call(
        functools.partial(_compress_kernel, n_pages=n_pages, page=page, group=group, feature_major=feature_major),
        grid_spec=grid_spec,
        out_shape=[out, out],
        compiler_params=_cparams(("parallel", "arbitrary")),
        name="compress",
    )(page_table, *([pool] * group), pe2, w1p, w2)


BF16_ROWS = 16


def _act_dtype(block_rows):
    return BF16 if block_rows % BF16_ROWS == 0 else F32


def _left_lanes(shape):
    return lax.broadcasted_iota(jnp.int32, shape, len(shape) - 1) < HEAD_DIM


def _own_half(left, h):
    return left if h % 2 == 0 else jnp.logical_not(left)


def _value_with_ones(v2, own):
    return jnp.where(own, v2, jnp.ones_like(v2))


def _normalise(acc, own):
    l = jnp.where(own, pltpu.roll(acc, HEAD_DIM, axis=1), acc)
    rinv = jnp.where(l > 0.0, 1.0 / l, 0.0)
    return acc * rinv, rinv


def _emit_heads(o_ref, pieces, halves):
    left = _left_lanes(pieces[0].shape)
    for c in range(H_ATT // 2):
        a, b = pieces[2 * c], pieces[2 * c + 1]
        a = a if halves[2 * c] == 0 else pltpu.roll(a, HEAD_DIM, axis=1)
        b = b if halves[2 * c + 1] == 1 else pltpu.roll(b, HEAD_DIM, axis=1)
        o_ref[:, c * LANES:(c + 1) * LANES] = jnp.where(left, a, b).astype(o_ref.dtype)


def _gate_col(gate, hq, branch):
    col = GATE_COL0 + hq * 3 + branch
    return gate[:, col:col + 1]


def _topk_lanes(vals, n_iter):
    lane_f = lax.broadcasted_iota(jnp.int32, vals.shape, 1).astype(F32)

    def body(_, vals):
        mx = jnp.max(vals, axis=-1, keepdims=True)
        first = jnp.min(jnp.where(vals == mx, lane_f, float(LANES)), axis=-1, keepdims=True)
        return jnp.where(lane_f == first, TAKEN, vals)

    return jnp.where(lax.fori_loop(0, n_iter, body, vals) == TAKEN, 1.0, 0.0)


def _topk_rows(vals, n_iter):
    row_f = lax.broadcasted_iota(jnp.int32, vals.shape, 0).astype(F32)

    def body(_, vals):
        mx = jnp.max(vals, axis=0, keepdims=True)
        first = jnp.min(jnp.where(vals == mx, row_f, float(LANES)), axis=0, keepdims=True)
        return jnp.where(row_f == first, TAKEN, vals)

    return jnp.where(lax.fori_loop(0, n_iter, body, vals) == TAKEN, 1.0, 0.0)


def _split_bf16x3(x):
    hi = x.astype(BF16)
    r1 = x - hi.astype(F32)
    mid = r1.astype(BF16)
    lo = (r1 - mid.astype(F32)).astype(BF16)
    return hi, mid, lo


def _cmp_attn_kernel(qz_ref, kc_ref, vc_ref, gate_ref, ov_ref, o_ref, selb_ref, *, tq, pos0, n_iter, n_past_blk):
    i = pl.program_id(1)
    n_all = kc_ref.shape[1]
    q_last = pos0 + (i + 1) * tq - 1
    n_vis = jnp.maximum(q_last - (CMP_LEN - 1), 0) // CMP_STRIDE + 1
    steps = jnp.minimum((n_vis + LANES - 1) // LANES, n_all // LANES)
    for w in range(1, n_all // LANES + 1):
        @pl.when(steps == w)
        def _(w=w):
            _cmp_attn_body(qz_ref, kc_ref, vc_ref, gate_ref, ov_ref, o_ref, selb_ref, tq=tq, pos0=pos0, n_iter=n_iter,
                           n_past_blk=n_past_blk, ncmp=w * LANES)


def _cmp_attn_body(qz_ref, kc_ref, vc_ref, gate_ref, ov_ref, o_ref, selb_ref, *, tq, pos0, n_iter, n_past_blk, ncmp):
    i = pl.program_id(1)
    rows = GQA * tq
    ov_b = ov_ref[0:ncmp, :].astype(BF16)
    qpos = pos0 + i * tq + lax.broadcasted_iota(jnp.int32, (tq, 1), 0)
    cend = lax.broadcasted_iota(jnp.int32, (1, ncmp), 1) * CMP_STRIDE + (CMP_LEN - 1)
    bias = jnp.where(cend <= qpos, 0.0, MASKV)
    gate = _sigmoid(gate_ref[...])
    left = _left_lanes((rows, LANES))
    transposed = tq % LANES == 0
    shape = (LANES, tq) if transposed else (tq, LANES)
    blk = lax.broadcasted_iota(jnp.int32, shape, 0 if transposed else 1)
    if n_past_blk is None:
        cur = (pos0 + i * tq + lax.broadcasted_iota(jnp.int32, shape, 1 if transposed else 0)) // SEL_BLK
        forced = (blk == 0) | (blk == cur) | (blk == cur - 1)
        allowed = blk <= cur
    else:
        forced = (blk == 0) | (blk == n_past_blk - 1)
        allowed = blk < n_past_blk
    pieces, halves = [None] * H_ATT, [0] * H_ATT

    def scores(h):
        pr = h // 2
        q = qz_ref[h].reshape(rows, LANES).astype(BF16)
        s = lax.dot_general(q, kc_ref[0, 0:ncmp, pr * LANES:(pr + 1) * LANES], _NT, preferred_element_type=F32)
        return (s.reshape(GQA, tq, ncmp) + bias[None]).reshape(rows, ncmp)

    vals = []
    s = scores(0)
    for h in range(N_KV):
        s_next = scores(h + 1) if h + 1 < N_KV else None
        pr = h // 2
        own = _own_half(left, h)
        v2 = vc_ref[0, 0:ncmp, pr * LANES:(pr + 1) * LANES]
        m = jnp.maximum(jnp.max(s, axis=-1, keepdims=True), M_INIT)
        e = jnp.exp2(s - m)
        acc = jnp.dot(e.astype(BF16), _value_with_ones(v2, _own_half(_left_lanes(v2.shape), h)),
                      preferred_element_type=F32)
        o, rinv = _normalise(acc, own)
        for g in range(GQA):
            hq = h * GQA + g
            pieces[hq] = o[g * tq:(g + 1) * tq] * _gate_col(gate, hq, 0)
            halves[hq] = h % 2
        p = e * jnp.concatenate([rinv] * (ncmp // LANES), axis=1)
        psum = jnp.sum(p.reshape(GQA, tq, ncmp), axis=0)
        imp = sum(jnp.dot(part, ov_b, preferred_element_type=F32) for part in _split_bf16x3(psum))
        vals.append(jnp.where(forced, TAKEN, jnp.where(allowed, imp.T if transposed else imp, NEG_INF)))
        s = s_next
    _emit_heads(o_ref, pieces, halves)
    if transposed:
        sels = [_topk_rows(vals[h], n_iter).T for h in range(N_KV)]
    elif N_KV * tq <= LANES:
        stacked = jnp.concatenate(vals + [jnp.full((LANES - N_KV * tq, LANES), NEG_INF, F32)], axis=0)
        sel = _topk_rows(stacked.T, n_iter).T
        sels = [sel[h * tq:(h + 1) * tq] for h in range(N_KV)]
    else:
        sels = [_topk_lanes(vals[h], n_iter) for h in range(N_KV)]
    for h in range(N_KV):
        selb_ref[0, h] = jnp.where(sels[h] > 0.5, 0.0, MASKV).astype(selb_ref.dtype)


def _cmp_attn(qz, kc, vc, us, ov, bsz, t, tq, pos0, n_iter, n_past_blk):
    nq = t // tq
    ncmp = kc.shape[1]
    return pl.pallas_call(
        functools.partial(_cmp_attn_kernel, tq=tq, pos0=pos0, n_iter=n_iter, n_past_blk=n_past_blk),
        grid=(bsz, nq),
        in_specs=[pl.BlockSpec((N_KV, GQA, tq, LANES), lambda b, i: (0, 0, b * nq + i, 0)),
                  pl.BlockSpec((1, ncmp, KV_DIM), lambda b, i: (b, 0, 0)),
                  pl.BlockSpec((1, ncmp, KV_DIM), lambda b, i: (b, 0, 0)),
                  pl.BlockSpec((tq, LANES), lambda b, i: (b * nq + i, 0)),
                  pl.BlockSpec(ov.shape, lambda b, i: (0, 0))],
        out_specs=[pl.BlockSpec((tq, D_MODEL), lambda b, i: (b * nq + i, 0)),
                   pl.BlockSpec((1, N_KV, tq, LANES), lambda b, i: (b, 0, i, 0))],
        out_shape=[jax.ShapeDtypeStruct((bsz * t, D_MODEL), _act_dtype(tq)),
                   jax.ShapeDtypeStruct((bsz, N_KV, t, LANES), BF16)],
        compiler_params=_cparams(("parallel", "parallel")),
        name="cmp_attn",
    )(qz, kc, vc, us, ov)


def _block_expand(k0, tk):
    blk = lax.broadcasted_iota(jnp.int32, (LANES, tk), 0)
    col = lax.broadcasted_iota(jnp.int32, (LANES, tk), 1)
    return (blk == (k0 + col) // SEL_BLK).astype(BF16)


def _band_attn_kernel(qz_ref, kv_ref, gate_ref, *rest, tq, tk, nk, nback, branch, use_sel):
    if use_sel:
        selb_ref, o_ref, m_ref, acc_ref = rest
    else:
        o_ref, m_ref, acc_ref = rest
    i = pl.program_id(1)
    kk = pl.program_id(2)
    rows = GQA * tq

    @pl.when(kk == 0)
    def _():
        m_ref[...] = jnp.full(m_ref.shape, M_INIT, F32)
        acc_ref[...] = jnp.zeros(acc_ref.shape, F32)

    if use_sel:
        kt = kk
        live = kt * tk <= i * tq + tq - 1
    else:
        kt = i * (tq // tk) - nback + kk
        live = kt >= 0

    @pl.when(live)
    def _():
        k0 = kt * tk
        dlt = (i * tq + lax.broadcasted_iota(jnp.int32, (tq, tk), 0)) - (k0 + lax.broadcasted_iota(jnp.int32, (tq, tk), 1))
        if use_sel:
            pos_bias = jnp.where(dlt >= 0, 0.0, MASKV)
            expand = _block_expand(k0, tk)
        else:
            pos_bias = jnp.where((dlt >= 0) & (dlt < WINDOW), 0.0, MASKV)
        left_v = _left_lanes((tk, LANES))

        def scores(h):
            pr = (h // 2) * LANES
            bias = pos_bias
            if use_sel:
                bias = bias + jnp.dot(selb_ref[0, h], expand, preferred_element_type=F32)
            q = qz_ref[h].reshape(rows, LANES)
            s = lax.dot_general(q, kv_ref[:, pr:pr + LANES], _NT, preferred_element_type=F32)
            return (s.reshape(GQA, tq, tk) + bias[None]).reshape(rows, tk)

        s = scores(0)
        for h in range(N_KV):
            s_next = scores(h + 1) if h + 1 < N_KV else None
            pr = (h // 2) * LANES
            vaug = _value_with_ones(kv_ref[:, KV_DIM + pr: KV_DIM + pr + LANES], _own_half(left_v, h))
            m_old = m_ref[h]
            m_new = jnp.maximum(m_old, jnp.max(s, axis=-1, keepdims=True))
            e = jnp.exp2((s - m_new[:, 0:1]).astype(BF16))
            alpha = jnp.exp2(m_old - m_new)
            acc_ref[h] = alpha * acc_ref[h] + jnp.dot(e, vaug, preferred_element_type=F32)
            m_ref[h] = m_new
            s = s_next

    @pl.when(kk == nk - 1)
    def _():
        gate = _sigmoid(gate_ref[...])
        left = _left_lanes((rows, LANES))
        pieces, halves = [None] * H_ATT, [0] * H_ATT
        for h in range(N_KV):
            o, _ = _normalise(acc_ref[h], _own_half(left, h))
            for g in range(GQA):
                hq = h * GQA + g
                pieces[hq] = o[g * tq:(g + 1) * tq] * _gate_col(gate, hq, branch)
                halves[hq] = h % 2
        _emit_heads(o_ref, pieces, halves)


def _band_attn(qz, kvb, us, selb, bsz, t, tq, tk, use_sel):
    nq = t // tq
    nkt = t // tk
    if use_sel:
        nk, nback, branch = nkt, 0, 1
        kv_idx = lambda b, i, kk: (b * nkt + jnp.minimum(kk, (i * tq + tq - 1) // tk), 0)
    else:
        nback = -(-(WINDOW - 1) // tk)
        nk, branch = nback + tq // tk, 2
        kv_idx = lambda b, i, kk: (b * nkt + jnp.maximum(i * (tq // tk) - nback + kk, 0), 0)
    in_specs = [pl.BlockSpec((N_KV, GQA, tq, LANES), lambda b, i, kk: (0, 0, b * nq + i, 0)),
                pl.BlockSpec((tk, 2 * KV_DIM), kv_idx),
                pl.BlockSpec((tq, LANES), lambda b, i, kk: (b * nq + i, 0))]
    args = [qz, kvb, us]
    if use_sel:
        in_specs.append(pl.BlockSpec((1, N_KV, tq, LANES), lambda b, i, kk: (b, 0, i, 0)))
        args.append(selb)
    return pl.pallas_call(
        functools.partial(_band_attn_kernel, tq=tq, tk=tk, nk=nk, nback=nback, branch=branch, use_sel=use_sel),
        grid=(bsz, nq, nk),
        in_specs=in_specs,
        out_specs=pl.BlockSpec((tq, D_MODEL), lambda b, i, kk: (b * nq + i, 0)),
        out_shape=jax.ShapeDtypeStruct((bsz * t, D_MODEL), BF16),
        scratch_shapes=[pltpu.VMEM((N_KV, GQA * tq, LANES), F32), pltpu.VMEM((N_KV, GQA * tq, LANES), F32)],
        compiler_params=_cparams(("parallel", "parallel", "arbitrary")),
        name="sel_attn" if use_sel else "win_attn",
    )(*args)


N_PAIR = N_KV // 2


def _pair_init(m_ref, l_ref, acc_ref):
    m_ref[...] = jnp.full(m_ref.shape, M_INIT, F32)
    l_ref[...] = jnp.zeros(l_ref.shape, F32)
    acc_ref[...] = jnp.zeros(acc_ref.shape, F32)


def _pair_queries(qz_ref, pr, tq):
    return jnp.concatenate([qz_ref[2 * pr].reshape(GQA * tq, LANES), qz_ref[2 * pr + 1].reshape(GQA * tq, LANES)],
                           axis=0).astype(BF16)


def _pair_update(pr, q, kv, bias_a, bias_b, m_ref, l_ref, acc_ref, feature_major=False):
    bias = jnp.concatenate([bias_a] * GQA + [bias_b] * GQA, axis=0)
    if feature_major:
        k2 = kv[pr * LANES:(pr + 1) * LANES, :]
        v2 = kv[KV_DIM + pr * LANES: KV_DIM + (pr + 1) * LANES, :]
        s = jnp.dot(q, k2, preferred_element_type=F32) + bias
        pv_dims = _NT
    else:
        k2 = kv[:, pr * LANES:(pr + 1) * LANES]
        v2 = kv[:, KV_DIM + pr * LANES: KV_DIM + (pr + 1) * LANES]
        s = lax.dot_general(q, k2, _NT, preferred_element_type=F32) + bias
        pv_dims = (((1,), (0,)), ((), ()))
    m_old = m_ref[pr]
    m_new = jnp.maximum(m_old, jnp.max(s, axis=-1, keepdims=True))
    e = jnp.exp2(s - m_new[:, 0:1])
    alpha = jnp.exp2(m_old - m_new)
    l_ref[pr] = alpha * l_ref[pr] + jnp.sum(e, axis=-1, keepdims=True)
    acc_ref[pr] = alpha * acc_ref[pr] + lax.dot_general(e.astype(BF16), v2, pv_dims, preferred_element_type=F32)
    m_ref[pr] = m_new


def _pair_finish(o_ref, gate, branch, tq, l_ref, acc_ref):
    pieces, halves = [None] * H_ATT, [0] * H_ATT
    for pr in range(N_PAIR):
        l = l_ref[pr]
        o = acc_ref[pr] * jnp.where(l > 0.0, 1.0 / l, 0.0)
        for side in range(2):
            for g in range(GQA):
                hq = (2 * pr + side) * GQA + g
                r0 = (side * GQA + g) * tq
                pieces[hq] = o[r0:r0 + tq] * _gate_col(gate, hq, branch)
                halves[hq] = side
    _emit_heads(o_ref, pieces, halves)


def _causal_bias(tq):
    qi = lax.broadcasted_iota(jnp.int32, (tq, tq), 0)
    ki = lax.broadcasted_iota(jnp.int32, (tq, tq), 1)
    return jnp.where(ki <= qi, 0.0, MASKV)


def _pair_scratch(tq):
    rows = 2 * GQA * tq
    return [pltpu.VMEM((N_PAIR, rows, LANES), F32)] * 3


def _sel_paged_kernel(pt_ref, *refs, tq, n_steps, page, group):
    page_refs = refs[:group]
    qz_ref, new_ref, gate_ref, selb_ref, o_ref, m_ref, l_ref, acc_ref = refs[group:]
    p = pl.program_id(1)

    @pl.when(p == 0)
    def _():
        _pair_init(m_ref, l_ref, acc_ref)

    @pl.when(p < n_steps)
    def _():
        kv = jnp.concatenate([r[0, 0] for r in page_refs], axis=1).astype(BF16)
        expand = _block_expand(p * (group * page), group * page)
        bias = [jnp.dot(selb_ref[0, h], expand, preferred_element_type=F32) for h in range(N_KV)]
        for pr in range(N_PAIR):
            _pair_update(pr, _pair_queries(qz_ref, pr, tq), kv, bias[2 * pr], bias[2 * pr + 1], m_ref, l_ref, acc_ref,
                         feature_major=True)

    @pl.when(p == n_steps)
    def _():
        kv = new_ref[...].astype(BF16)
        bias = _causal_bias(tq)
        for pr in range(N_PAIR):
            _pair_update(pr, _pair_queries(qz_ref, pr, tq), kv, bias, bias, m_ref, l_ref, acc_ref)
        _pair_finish(o_ref, _sigmoid(gate_ref[...]), 1, tq, l_ref, acc_ref)


def _sel_paged(qz, pool, layer, page_table, new_rows, us, selb, tq):
    bsz, n_pages = page_table.shape
    blk, page = _page_shape(pool, True)
    group = SEL_PAGE_GROUP
    n_steps = n_pages // group
    page_spec = lambda k: pl.BlockSpec(
        blk, lambda b, p, pt: (layer, pt[b, jnp.minimum(p, n_steps - 1) * group + k], 0, 0))
    grid_spec = pltpu.PrefetchScalarGridSpec(
        num_scalar_prefetch=1,
        grid=(bsz, n_steps + 1),
        in_specs=[page_spec(k) for k in range(group)] + [
            pl.BlockSpec((N_KV, GQA, tq, LANES), lambda b, p, pt: (0, 0, b, 0)),
            pl.BlockSpec((tq, 2 * KV_DIM), lambda b, p, pt: (b, 0)),
            pl.BlockSpec((tq, LANES), lambda b, p, pt: (b, 0)),
            pl.BlockSpec((1, N_KV, tq, LANES), lambda b, p, pt: (b, 0, 0, 0))],
        out_specs=pl.BlockSpec((tq, D_MODEL), lambda b, p, pt: (b, 0)),
        scratch_shapes=_pair_scratch(tq),
    )
    return pl.pallas_call(
        functools.partial(_sel_paged_kernel, tq=tq, n_steps=n_steps, page=page, group=group),
        grid_spec=grid_spec,
        out_shape=jax.ShapeDtypeStruct((bsz * tq, D_MODEL), F32),
        compiler_params=_cparams(("parallel", "arbitrary")),
        name="sel_paged",
    )(page_table, *([pool] * group), qz, new_rows, us, selb)


def _win_cached_kernel(qz_ref, wc_ref, new_ref, gate_ref, o_ref, m_ref, l_ref, acc_ref, *, tq, wlen):
    _pair_init(m_ref, l_ref, acc_ref)
    qi = lax.broadcasted_iota(jnp.int32, (tq, wlen), 0)
    kj = lax.broadcasted_iota(jnp.int32, (tq, wlen), 1)
    bias_c = jnp.where((wlen - kj + qi) < WINDOW, 0.0, MASKV)
    bias_n = _causal_bias(tq)
    kv_c = wc_ref[0, 0].astype(BF16)
    kv_n = new_ref[...].astype(BF16)
    for pr in range(N_PAIR):
        q = _pair_queries(qz_ref, pr, tq)
        _pair_update(pr, q, kv_c, bias_c, bias_c, m_ref, l_ref, acc_ref, feature_major=True)
        _pair_update(pr, q, kv_n, bias_n, bias_n, m_ref, l_ref, acc_ref)
    _pair_finish(o_ref, _sigmoid(gate_ref[...]), 2, tq, l_ref, acc_ref)


def _win_cached(qz, wcache, layer, new_rows, us, tq):
    bsz, wlen = wcache.shape[1], wcache.shape[3]
    return pl.pallas_call(
        functools.partial(_win_cached_kernel, tq=tq, wlen=wlen),
        grid=(bsz,),
        in_specs=[pl.BlockSpec((N_KV, GQA, tq, LANES), lambda b: (0, 0, b, 0)),
                  pl.BlockSpec((1, 1, 2 * KV_DIM, wlen), lambda b: (layer, b, 0, 0)),
                  pl.BlockSpec((tq, 2 * KV_DIM), lambda b: (b, 0)),
                  pl.BlockSpec((tq, LANES), lambda b: (b, 0))],
        out_specs=pl.BlockSpec((tq, D_MODEL), lambda b: (b, 0)),
        out_shape=jax.ShapeDtypeStruct((bsz * tq, D_MODEL), F32),
        scratch_shapes=_pair_scratch(tq),
        compiler_params=_cparams(("parallel",)),
        name="win_cached",
    )(qz, wcache, new_rows, us)


def _merge_kernel(x_ref, yn_ref, ysc_ref, oc_ref, os_ref, ow_ref, g1_ref, g2_ref, g3_ref,
                  wssd_ref, wsc_ref, wnsa_ref, wout_ref, o_ref):
    y_ssd = jnp.dot(yn_ref[...].astype(BF16), wssd_ref[...], preferred_element_type=F32)
    y_sc = jnp.dot(ysc_ref[...].astype(BF16), wsc_ref[...], preferred_element_type=F32)
    o = (oc_ref[...].astype(F32) + os_ref[...].astype(F32) + ow_ref[...].astype(F32)).astype(BF16)
    y_nsa = jnp.dot(o, wnsa_ref[...], preferred_element_type=F32)
    mix = _sigmoid(g1_ref[...]) * y_ssd + _sigmoid(g2_ref[...]) * y_sc + _sigmoid(g3_ref[...]) * y_nsa
    o_ref[...] = x_ref[...] + jnp.dot(mix.astype(BF16), wout_ref[...], preferred_element_type=F32)


def _merge(x, yn, ysc, oc, osel, ow, u, w, tm):
    m = x.shape[0]
    row = lambda c: pl.BlockSpec((tm, D_MODEL), lambda i: (i, c))
    wspec = pl.BlockSpec((D_MODEL, D_MODEL), lambda i: (0, 0))
    gcol = U_G // D_MODEL
    return pl.pallas_call(
        _merge_kernel,
        grid=(m // tm,),
        in_specs=[row(0)] * 6 + [row(gcol), row(gcol + 1), row(gcol + 2)] + [wspec] * 4,
        out_specs=row(0),
        out_shape=jax.ShapeDtypeStruct((m, D_MODEL), F32),
        compiler_params=_cparams(("parallel",)),
        name="merge",
    )(x, yn, ysc, oc, osel, ow, u, u, u, w["w_ssd_out"], w["w_sconv_out"], w["w_nsa_out"], w["w_out"])


def _mlp_kernel(x_ref, g_ref, wup_ref, wdn_ref, o_ref, h_ref, acc_ref, *, nf):
    j = pl.program_id(1)

    @pl.when(j == 0)
    def _():
        x = x_ref[...]
        ms = jnp.mean(x * x, axis=-1, keepdims=True)
        h_ref[...] = (x * lax.rsqrt(ms + EPS) * g_ref[...]).astype(h_ref.dtype)
        acc_ref[...] = jnp.zeros(acc_ref.shape, F32)

    up = jnp.dot(h_ref[...], wup_ref[...], preferred_element_type=F32)
    a = jnp.square(jnp.maximum(up, 0.0)).astype(BF16)
    acc_ref[...] += jnp.dot(a, wdn_ref[...], preferred_element_type=F32)

    @pl.when(j == nf - 1)
    def _():
        o_ref[...] = x_ref[...] + acc_ref[...]


def _mlp(x, g, wup, wdn, tm, tf):
    m = x.shape[0]
    nf = D_FF // tf
    return pl.pallas_call(
        functools.partial(_mlp_kernel, nf=nf),
        grid=(m // tm, nf),
        in_specs=[pl.BlockSpec((tm, D_MODEL), lambda i, j: (i, 0)),
                  pl.BlockSpec((1, D_MODEL), lambda i, j: (0, 0)),
                  pl.BlockSpec((D_MODEL, tf), lambda i, j: (0, j)),
                  pl.BlockSpec((tf, D_MODEL), lambda i, j: (j, 0))],
        out_specs=pl.BlockSpec((tm, D_MODEL), lambda i, j: (i, 0)),
        out_shape=jax.ShapeDtypeStruct((m, D_MODEL), F32),
        scratch_shapes=[pltpu.VMEM((tm, D_MODEL), BF16), pltpu.VMEM((tm, D_MODEL), F32)],
        compiler_params=_cparams(("parallel", "arbitrary")),
        name="mlp",
    )(x, g, wup, wdn)


def _pad_lanes(v, width=LANES):
    v = v.reshape(1, -1).astype(F32)
    return jnp.pad(v, ((0, 0), (0, width - v.shape[1])))


def _pad_rows(v, rows=SUBLANES, front=False):
    pad = rows - v.shape[-2]
    cfg = [(0, 0)] * (v.ndim - 2) + [((pad, 0) if front else (0, pad)), (0, 0)]
    return jnp.pad(v, cfg)


def _layer_weights(l, w_in, norm_mix, norm_mlp, ssd_conv_w, ssd_conv_b, ssd_dt_bias, ssd_a_log, ssd_d, ssd_norm,
                   w_ssd_out, sconv_w, w_sconv_out, cmp_pe, cmp_w1, cmp_w2, w_nsa_out, w_out, w_mlp_up, w_mlp_down):
    b = np.cumsum((0,) + IN_SIZES)
    wi = w_in[l]
    seg = lambda k: wi[:, b[k]:b[k + 1]]
    w_main = jnp.concatenate([seg(1), seg(7), seg(0), seg(3), seg(4), seg(5), seg(6), seg(9)], axis=1).astype(BF16)
    w_small = jnp.concatenate([seg(2), seg(8)], axis=1)
    w_small = jnp.pad(w_small, ((0, 0), (0, LANES - w_small.shape[1]))).astype(BF16)
    return dict(
        w_main=w_main, w_small=w_small,
        norm_mix=norm_mix[l].reshape(1, -1), norm_mlp=norm_mlp[l].reshape(1, -1),
        convw=_pad_rows(ssd_conv_w[l]), convb=ssd_conv_b[l].reshape(1, -1),
        dtb=_pad_lanes(ssd_dt_bias[l]), alog=_pad_lanes(ssd_a_log[l]), dvec=_pad_lanes(ssd_d[l]),
        ssdn=ssd_norm[l].reshape(1, -1), scw=_pad_rows(sconv_w[l]),
        w_ssd_out=w_ssd_out[l].astype(BF16), w_sconv_out=w_sconv_out[l].astype(BF16),
        w_nsa_out=w_nsa_out[l].astype(BF16), w_out=w_out[l].astype(BF16),
        cmp_pe=cmp_pe[l].reshape(2, 2, CMP_STRIDE, HEAD_DIM).transpose(0, 2, 1, 3).reshape(2, CMP_STRIDE, LANES),
        cmp_w1=cmp_w1[l].reshape(2, 2, CMP_STRIDE, HEAD_DIM, CMP_HID).transpose(0, 2, 1, 3, 4).reshape(
            2, CMP_STRIDE, LANES, CMP_HID).astype(BF16),
        cmp_w2=cmp_w2[l].astype(BF16),
        w_mlp_up=w_mlp_up[l].astype(BF16), w_mlp_down=w_mlp_down[l].astype(BF16),
    )


def _rope_tables(pos):
    half = HEAD_DIM // 2
    inv = ROPE_THETA ** (-jnp.arange(half, dtype=F32) / half)
    ang = pos.astype(F32)[:, None] * inv[None, :]
    cos, sin = jnp.cos(ang), jnp.sin(ang)
    cos_t = jnp.concatenate([cos, cos, cos, cos], axis=1)
    sin_t = jnp.concatenate([-sin, sin, -sin, sin], axis=1)
    return cos_t, sin_t


def _overlap(ncmp_pad, ncmp):
    c0 = np.arange(ncmp_pad)[:, None] * CMP_STRIDE
    s0 = np.arange(LANES)[None, :] * SEL_BLK
    ov = np.maximum(np.minimum(c0 + CMP_LEN, s0 + SEL_BLK) - np.maximum(c0, s0), 0).astype(np.float32) / CMP_LEN
    ov[ncmp:] = 0.0
    return jnp.asarray(ov)


def _tile(m, pref):
    t = pref
    while m % t:
        t //= 2
    return t


def _front(x2d, w, tm):
    u = _norm_matmul(x2d, w["norm_mix"], w["w_main"], tm, N_MAIN // 8)
    us = _norm_matmul(x2d, w["norm_mix"], w["w_small"], tm, LANES)
    return u, us


def _prompt_layer(x2d, w, bsz, t, tabs):
    m = bsz * t
    u, us = _front(x2d, w, _tile(m, 1024))
    cos_t, sin_t, ov = tabs
    qz, cmp_b, sel_b, win_b, cmp_t, sel_t, win_t = _rope(u, cos_t, sin_t, _tile(t, 512), bsz, t, True)
    zeros = lambda *s: jnp.zeros(s, F32)
    yn, ysc, h_new, ch_last = _seqmix(u, us, zeros(bsz, SUBLANES, CONV_DIM), zeros(bsz, SUBLANES, D_MODEL),
                                      zeros(bsz, D_MODEL, SSD_STATE), w, bsz, t)
    n_pg = t // CMP_PAGE
    kc, vc = _compress(cmp_b.reshape(1, bsz * n_pg, CMP_PAGE, 2 * KV_DIM), 0,
                       jnp.arange(bsz * n_pg, dtype=jnp.int32).reshape(bsz, n_pg),
                       w["cmp_pe"], w["cmp_w1"], w["cmp_w2"], False)
    tq = _tile(t, 256)
    oc, selb = _cmp_attn(qz, kc, vc, us, ov, bsz, t, tq, 0, N_TOP - N_FORCED, None)
    ta = _tile(t, 512)
    osel = _band_attn(qz, sel_b, us, selb, bsz, t, ta, ta, True)
    ow = _band_attn(qz, win_b, us, None, bsz, t, ta, ta, False)
    x1 = _merge(x2d, yn, ysc, oc, osel, ow, u, w, _tile(m, 256))
    x2 = _mlp(x1, w["norm_mlp"], w["w_mlp_up"], w["w_mlp_down"], _tile(m, 1024), 1024)
    wl = min(WINDOW, t)
    u3 = u.reshape(bsz, t, N_MAIN)
    rows_major = lambda a: a.reshape(bsz, 2, N_KV, HEAD_DIM, a.shape[-1]).transpose(0, 4, 1, 2, 3)
    state = (rows_major(cmp_t), rows_major(sel_t), rows_major(win_t[:, :, t - wl:]),
             h_new.reshape(bsz, H_SSD, SSD_HEAD_DIM, SSD_STATE),
             u3[:, t - (SSD_CONV - 1):, U_XBC:U_XBC + CONV_DIM],
             ch_last[:, SUBLANES - (SC_WIDTH - 1):])
    return x2, state


def _feature_major(cache):
    nd = cache.ndim
    perm = tuple(range(nd - 4)) + (nd - 3, nd - 2, nd - 1, nd - 4)
    return jnp.transpose(cache, perm).reshape(cache.shape[:-4] + (2 * KV_DIM, cache.shape[-4]))


def _sample_layer(x2d, w, bsz, t, tabs, layer, past):
    ssm0, conv0, sc0, cmp_pool, sel_pool, win_cache, win_rows_old, page_table = past
    m = bsz * t
    u, us = _front(x2d, w, m)
    cos_t, sin_t, ov = tabs
    qz, cmp_rows, sel_rows, win_rows = _rope(u, cos_t, sin_t, m, bsz, t, False)
    yn, ysc, h_new, ch_last = _seqmix(u, us, _pad_rows(conv0, front=True), _pad_rows(sc0, front=True),
                                      ssm0.reshape(bsz, D_MODEL, SSD_STATE), w, bsz, t)
    kc, vc = _compress(cmp_pool, layer, page_table, w["cmp_pe"], w["cmp_w1"], w["cmp_w2"], True)
    past_len = page_table.shape[1] * cmp_pool.shape[3]
    oc, selb = _cmp_attn(qz, kc, vc, us, ov, bsz, t, t, past_len, N_TOP - N_FORCED, past_len // SEL_BLK)
    osel = _sel_paged(qz, sel_pool, layer, page_table, sel_rows, us, selb, t)
    ow = _win_cached(qz, win_cache, layer, win_rows, us, t)
    x1 = _merge(x2d, yn, ysc, oc, osel, ow, u, w, m)
    x2 = _mlp(x1, w["norm_mlp"], w["w_mlp_up"], w["w_mlp_down"], m, 1024)
    wl = win_rows_old.shape[1]
    win_all = jnp.concatenate([win_rows_old, win_rows.reshape(bsz, t, 2, N_KV, HEAD_DIM)], axis=1)
    keep = min(WINDOW, wl + t)
    u3 = u.reshape(bsz, t, N_MAIN)
    state = (cmp_rows.reshape(bsz, t, 2, N_KV, HEAD_DIM), sel_rows.reshape(bsz, t, 2, N_KV, HEAD_DIM),
             win_all[:, wl + t - keep:],
             h_new.reshape(bsz, H_SSD, SSD_HEAD_DIM, SSD_STATE),
             u3[:, t - (SSD_CONV - 1):, U_XBC:U_XBC + CONV_DIM],
             ch_last[:, SUBLANES - (SC_WIDTH - 1):])
    return x2, state


def kernel(x_prompt, x_sample, cache_cmp_kv, cache_sel_kv, cache_win_kv, state_ssm, state_ssd_conv, state_sconv,
           page_table, norm_mix, norm_mlp, norm_final, w_in, ssd_conv_w, ssd_conv_b, ssd_dt_bias, ssd_a_log, ssd_d,
           ssd_norm, w_ssd_out, sconv_w, w_sconv_out, cmp_pe, cmp_w1, cmp_w2, w_nsa_out, w_out, w_mlp_up,
           w_mlp_down):
    depth = w_in.shape[0]
    bp, tp, _ = x_prompt.shape
    bs, ts, _ = x_sample.shape
    n_pool, page = cache_cmp_kv.shape[1:3]
    past_len = page_table.shape[1] * page
    wl = cache_win_kv.shape[2]
    for rows in (tp, past_len):
        assert rows % (LANES * CMP_STRIDE) == 0 and (rows // CMP_STRIDE) % min(CMP_TILE, rows // CMP_STRIDE) == 0
        assert rows // SEL_BLK <= LANES
    assert ts == SUBLANES
    assert wl == WINDOW and past_len >= WINDOW
    assert page_table.shape[1] % SEL_PAGE_GROUP == 0 and (tp // CMP_PAGE) % PAGE_GROUP == 0
    assert (past_len + ts - CMP_LEN) // CMP_STRIDE + 1 == (past_len - CMP_LEN) // CMP_STRIDE + 1

    tabs_p = _rope_tables(jnp.tile(jnp.arange(tp), bp)) + (_overlap(tp // CMP_STRIDE, (tp - CMP_LEN) // CMP_STRIDE + 1),)
    tabs_s = _rope_tables(jnp.tile(past_len + jnp.arange(ts), bs)) + (
        _overlap(past_len // CMP_STRIDE, (past_len - CMP_LEN) // CMP_STRIDE + 1),)

    cmp_pool, sel_pool, win_cache = _feature_major(cache_cmp_kv), _feature_major(cache_sel_kv), _feature_major(cache_win_kv)
    xp = x_prompt.reshape(bp * tp, D_MODEL)
    xs = x_sample.reshape(bs * ts, D_MODEL)
    p_new = [[] for _ in range(6)]
    s_new = [[] for _ in range(6)]
    for l in range(depth):
        w = _layer_weights(l, w_in, norm_mix, norm_mlp, ssd_conv_w, ssd_conv_b, ssd_dt_bias, ssd_a_log, ssd_d,
                           ssd_norm, w_ssd_out, sconv_w, w_sconv_out, cmp_pe, cmp_w1, cmp_w2, w_nsa_out, w_out,
                           w_mlp_up, w_mlp_down)
        xp, st_p = _prompt_layer(xp, w, bp, tp, tabs_p)
        past = (state_ssm[l], state_ssd_conv[l], state_sconv[l], cmp_pool, sel_pool, win_cache, cache_win_kv[l],
                page_table)
        xs, st_s = _sample_layer(xs, w, bs, ts, tabs_s, l, past)
        for i in range(6):
            p_new[i].append(st_p[i])
            s_new[i].append(st_s[i])
    gf = norm_final.reshape(1, -1)
    y_prompt = _final_norm(xp, gf, _tile(bp * tp, 1024)).reshape(bp, tp, D_MODEL)
    y_sample = _final_norm(xs, gf, bs * ts).reshape(bs, ts, D_MODEL)
    return (y_prompt, y_sample) + tuple(jnp.stack(a) for a in p_new) + tuple(jnp.stack(a) for a in s_new)
```
